```python
import math
import jax, jax.numpy as jnp
from jax import lax
import numpy as np

D_MODEL = 1024
BATCH = 4
SEQ = 4096
DEPTH = 1
DEC_BATCH = 8
DEC_SEQ = 4096
PAST_LEN = 128

ATT_HEADS = 8
ATT_HEAD_DIM = 64
ATT_WIDTH = ATT_HEADS * ATT_HEAD_DIM
DILATED_CFG = ((128, 1), (512, 4), (2048, 16))
ATT_QBLOCK = 64
ML_HEADS = 4
ML_HEAD_DIM = 128
ML_WIDTH = ML_HEADS * ML_HEAD_DIM
ML_CHUNK = 128
CONV_W = 5
N_GATES = 2 * 2 * ML_HEADS
MIX_WIDTH = ATT_WIDTH + ML_WIDTH
IN_SIZES = (ATT_WIDTH, ATT_WIDTH, ATT_WIDTH, ML_WIDTH, ML_WIDTH, ML_WIDTH, ML_WIDTH, N_GATES)
IN_WIDTH = sum(IN_SIZES)
N_EXPERTS = 32
TOP_K = 4
D_FF = 1024
SWIGLU_LIMIT = 7.0
SWIGLU_ALPHA = 1.702
MOE_BLOCK = 256
EPS = 1e-6
NEG = -1e30

kernel_name = "hymba_dilated_mlstm_moe_encoder"


def rms_norm(x, g):
    xf = x.astype(jnp.float32)
    y = xf * lax.rsqrt(jnp.mean(xf * xf, axis=-1, keepdims=True) + EPS)
    return (y * g.astype(jnp.float32)).astype(x.dtype)


def _dilated_branch(q, k, v, window, dilation, slopes):
    B, S, H, hd = q.shape
    half = window // (2 * dilation)
    M = S // dilation
    BD = B * dilation
    QB = ATT_QBLOCK
    nb = -(-M // QB)
    Mp = nb * QB
    KB = QB + 2 * half

    def to_phase(t):
        return t.reshape(B, M, dilation, H, hd).transpose(0, 2, 1, 3, 4).reshape(BD, M, H, hd)

    qb = jnp.pad(to_phase(q), ((0, 0), (0, Mp - M), (0, 0), (0, 0))).reshape(BD, nb, QB, H, hd)
    kv_pad = ((0, 0), (half, half + Mp - M), (0, 0), (0, 0))
    key_idx = jnp.arange(nb)[:, None] * QB + jnp.arange(KB)[None, :]
    kb = jnp.pad(to_phase(k), kv_pad)[:, key_idx]
    vb = jnp.pad(to_phase(v), kv_pad)[:, key_idx].astype(jnp.float32)

    scores = jnp.einsum('bnqhd,bnkhd->bnhqk', qb, kb,
                        preferred_element_type=jnp.float32) * (hd ** -0.5)
    offset = jnp.arange(KB)[None, :] - half - jnp.arange(QB)[:, None]
    key_m = key_idx - half
    valid = (jnp.abs(offset) <= half)[None] & ((key_m >= 0) & (key_m < M))[:, None, :]
    alibi = -slopes[:, None, None] * (dilation * jnp.abs(offset)).astype(jnp.float32)[None]
    scores = jnp.where(valid[None, :, None], scores + alibi[None, None], NEG)
    mx = jnp.max(scores, axis=-1, keepdims=True)
    p = jnp.exp(scores - mx)
    s = jnp.sum(p, axis=-1)
    o = jnp.einsum('bnhqk,bnkhd->bnqhd', p, vb) / jnp.swapaxes(s, 2, 3)[..., None]
    lse = jnp.swapaxes(mx[..., 0] + jnp.log(s), 2, 3)
    o = o.reshape(BD, Mp, H, hd)[:, :M]
    lse = lse.reshape(BD, Mp, H)[:, :M]
    o = o.reshape(B, dilation, M, H, hd).transpose(0, 2, 1, 3, 4).reshape(B, S, H, hd)
    lse = lse.reshape(B, dilation, M, H).transpose(0, 2, 1, 3).reshape(B, S, H)
    return o, lse


def dilated_attention(q, k, v):
    B, S, _ = q.shape
    shp = (B, S, ATT_HEADS, ATT_HEAD_DIM)
    q, k, v = q.reshape(shp), k.reshape(shp), v.reshape(shp)
    slopes = jnp.exp2(-(8.0 / ATT_HEADS) * (jnp.arange(ATT_HEADS, dtype=jnp.float32) + 1.0))
    outs, lses = [], []
    for window, dilation in DILATED_CFG:
        o, l = _dilated_branch(q, k, v, window, dilation, slopes)
        outs.append(o)
        lses.append(l)
    w = jax.nn.softmax(jnp.stack(lses), axis=0)
    o = jnp.sum(w[..., None] * jnp.stack(outs), axis=0)
    return o.reshape(B, S, ATT_WIDTH).astype(v.dtype)


def _mlstm_scan(q, k, v, ig, fg):
    B, H, S, dk = q.shape
    dv = v.shape[-1]
    L = ML_CHUNK
    NC = S // L
    q = q.reshape(B, H, NC, L, dk) * (dk ** -0.5)
    k = k.reshape(B, H, NC, L, dk)
    v = v.reshape(B, H, NC, L, dv)
    ig = ig.reshape(B, H, NC, L)
    b = jnp.cumsum(jax.nn.log_sigmoid(fg).reshape(B, H, NC, L), axis=-1)
    bL = b[..., -1]
    a = bL[..., None] - b + ig
    m_loc = jnp.max(a, axis=-1)
    wloc = jnp.exp(a - m_loc[..., None])
    C_loc = jnp.einsum('bhcs,bhcsd,bhcse->bhcde', wloc, k, v)
    n_loc = jnp.einsum('bhcs,bhcsd->bhcd', wloc, k)

    def step(carry, inp):
        C, n, m = carry
        bl, ml, Cl, nl = inp
        m_new = jnp.maximum(bl + m, ml)
        sp = jnp.exp(bl + m - m_new)
        sl = jnp.exp(ml - m_new)
        C_new = sp[..., None, None] * C + sl[..., None, None] * Cl
        n_new = sp[..., None] * n + sl[..., None] * nl
        return (C_new, n_new, m_new), (C, n, m)

    init = (jnp.zeros((B, H, dk, dv), jnp.float32), jnp.zeros((B, H, dk), jnp.float32),
            jnp.full((B, H), NEG, jnp.float32))
    xs = (jnp.moveaxis(bL, 2, 0), jnp.moveaxis(m_loc, 2, 0),
          jnp.moveaxis(C_loc, 2, 0), jnp.moveaxis(n_loc, 2, 0))
    _, (C_prev, n_prev, m_prev) = lax.scan(step, init, xs)
    C_prev = jnp.moveaxis(C_prev, 0, 2)
    n_prev = jnp.moveaxis(n_prev, 0, 2)
    m_prev = jnp.moveaxis(m_prev, 0, 2)

    causal = jnp.tril(jnp.ones((L, L), dtype=bool))
    D = jnp.where(causal, b[..., :, None] - b[..., None, :] + ig[..., None, :], NEG)
    inter = b + m_prev[..., None]
    m_t = jnp.maximum(jnp.max(D, axis=-1), inter)
    Dw = jnp.exp(D - m_t[..., None])
    iw = jnp.exp(inter - m_t)
    qk = jnp.einsum('bhctd,bhcsd->bhcts', q, k) * Dw
    num = jnp.einsum('bhcts,bhcse->bhcte', qk, v) + iw[..., None] * jnp.einsum('bhctd,bhcde->bhcte', q, C_prev)
    den = jnp.sum(qk, axis=-1) + iw * jnp.einsum('bhctd,bhcd->bhct', q, n_prev)
    h = num / jnp.maximum(jnp.abs(den), jnp.exp(-m_t))[..., None]
    return h.reshape(B, H, S, dv)


def mlstm_mixer(q_m, k_m, v_m, o_m, gates, conv_w, conv_b, b_gates, g_head):
    B, S, _ = q_m.shape
    qk = jnp.concatenate([q_m, k_m], axis=-1).astype(jnp.float32)
    pad = CONV_W // 2
    xp = jnp.pad(qk, ((0, 0), (pad, pad), (0, 0)))
    cw = conv_w.astype(jnp.float32)
    acc = conv_b.astype(jnp.float32) + xp[:, 0:S] * cw[0]
    for j in range(1, CONV_W):
        acc = acc + xp[:, j:j + S] * cw[j]
    qk = jax.nn.silu(acc)
    q, k = jnp.split(qk, 2, axis=-1)

    def heads(t):
        return t.reshape(B, S, ML_HEADS, ML_HEAD_DIM).transpose(0, 2, 1, 3)

    q, k, v = heads(q), heads(k), heads(v_m.astype(jnp.float32))
    g = (gates.astype(jnp.float32) + b_gates.astype(jnp.float32))
    g = g.reshape(B, S, 2, 2, ML_HEADS).transpose(2, 3, 0, 4, 1)
    h_fw = _mlstm_scan(q, k, v, g[0, 0], g[0, 1])

    def flip(t):
        return jnp.flip(t, axis=2)

    h_bw = flip(_mlstm_scan(flip(q), flip(k), flip(v), flip(g[1, 0]), flip(g[1, 1])))
    h = h_fw + h_bw
    h = h * lax.rsqrt(jnp.mean(h * h, axis=-1, keepdims=True) + EPS)
    h = h * g_head.astype(jnp.float32).reshape(ML_HEADS, 1, ML_HEAD_DIM)
    h = h.transpose(0, 2, 1, 3).reshape(B, S, ML_WIDTH)
    return (jax.nn.sigmoid(o_m.astype(jnp.float32)) * h).astype(q_m.dtype)


def moe(x, w_router, b_router, w_gate_up, b_gate_up, w_down, b_down):
    T, D = x.shape
    logits = x.astype(jnp.float32) @ w_router.astype(jnp.float32) + b_router.astype(jnp.float32)
    top_val, top_idx = lax.top_k(logits, TOP_K)
    gate = jax.nn.softmax(top_val, axis=-1)
    TK = T * TOP_K
    flat_e = top_idx.reshape(-1)
    flat_tok = jnp.arange(TK, dtype=jnp.int32) // TOP_K
    flat_gate = gate.reshape(-1)
    order = jnp.argsort(flat_e)
    se = flat_e[order]
    counts = jnp.bincount(flat_e, length=N_EXPERTS)
    padded = (counts + MOE_BLOCK - 1) // MOE_BLOCK * MOE_BLOCK
    cum_p = jnp.cumsum(padded)
    start_p = cum_p - padded
    start_s = jnp.cumsum(counts) - counts
    dest = start_p[se] + jnp.arange(TK) - start_s[se]
    n_blocks = -(-TK // MOE_BLOCK) + N_EXPERTS
    P = n_blocks * MOE_BLOCK
    slot_tok = jnp.full((P,), T, jnp.int32).at[dest].set(flat_tok[order])
    slot_gate = jnp.zeros((P,), jnp.float32).at[dest].set(flat_gate[order])
    block_e = jnp.minimum(jnp.searchsorted(cum_p, jnp.arange(n_blocks) * MOE_BLOCK, side='right'),
                          N_EXPERTS - 1)
    x_pad = jnp.concatenate([x, jnp.zeros((1, D), x.dtype)], axis=0)
    xb = x_pad[slot_tok].reshape(n_blocks, MOE_BLOCK, D)

    def expert_block(args):
        xe, e = args
        gu = xe @ w_gate_up[e] + b_gate_up[e]
        g, u = jnp.split(gu, 2, axis=-1)
        g = jnp.minimum(g, SWIGLU_LIMIT)
        u = jnp.clip(u, -SWIGLU_LIMIT, SWIGLU_LIMIT)
        hdn = (u + 1.0) * (g * jax.nn.sigmoid(SWIGLU_ALPHA * g))
        return hdn @ w_down[e] + b_down[e]

    yb = lax.map(expert_block, (xb, block_e))
    y = yb.reshape(P, D).astype(jnp.float32) * slot_gate[:, None]
    out = jnp.zeros((T + 1, D), jnp.float32).at[slot_tok].add(y)[:T]
    return out.astype(x.dtype)


def _layer(x, g_mix, w_in, conv_w, conv_b, b_gates, g_head, w_out,
           g_ffn, w_router, b_router, w_gate_up, b_gate_up, w_down, b_down):
    B, S, D = x.shape
    xn = rms_norm(x, g_mix)
    proj = xn @ w_in
    split_at = [int(i) for i in np.cumsum(IN_SIZES[:-1])]
    q_a, k_a, v_a, q_m, k_m, v_m, o_m, gates = jnp.split(proj, split_at, axis=-1)
    attn = dilated_attention(q_a, k_a, v_a)
    mlstm = mlstm_mixer(q_m, k_m, v_m, o_m, gates, conv_w, conv_b, b_gates, g_head)
    h = x + jnp.concatenate([attn, mlstm], axis=-1) @ w_out
    hn = rms_norm(h, g_ffn).reshape(B * S, D)
    h = h + moe(hn, w_router, b_router, w_gate_up, b_gate_up, w_down, b_down).reshape(B, S, D)
    return h


def setup_inputs(seed: int = 0) -> dict:
    key = jax.random.key(seed)
    ks = jax.random.split(key, 20)
    f32 = jnp.float32
    nrm = lambda k, shp: jax.random.normal(k, shp, f32)
    i_b = 0.1 * nrm(ks[6], (2, 1, ML_HEADS))
    f_b = jnp.linspace(3.0, 6.0, ML_HEADS, dtype=f32)[None, None, :] + 0.1 * nrm(ks[7], (2, 1, ML_HEADS))
    return {
        "x_prompt": nrm(ks[0], (BATCH, SEQ, D_MODEL)),
        "x_sample": nrm(ks[1], (DEC_BATCH, DEC_SEQ, D_MODEL)),
        "g_mix": 1.0 + 0.02 * nrm(ks[2], (D_MODEL,)),
        "w_in": nrm(ks[3], (D_MODEL, IN_WIDTH)) * D_MODEL ** -0.5,
        "conv_w": nrm(ks[4], (CONV_W, 2 * ML_WIDTH)) * CONV_W ** -0.5,
        "conv_b": 0.02 * nrm(ks[5], (2 * ML_WIDTH,)),
        "b_gates": jnp.concatenate([i_b, f_b], axis=1).reshape(N_GATES),
        "g_head": 1.0 + 0.02 * nrm(ks[8], (ML_WIDTH,)),
        "w_out": nrm(ks[9], (MIX_WIDTH, D_MODEL)) * MIX_WIDTH ** -0.5,
        "g_ffn": 1.0 + 0.02 * nrm(ks[10], (D_MODEL,)),
        "w_router": nrm(ks[11], (D_MODEL, N_EXPERTS)) * D_MODEL ** -0.5,
        "b_router": 0.01 * nrm(ks[12], (N_EXPERTS,)),
        "w_gate_up": nrm(ks[13], (N_EXPERTS, D_MODEL, 2 * D_FF)) * D_MODEL ** -0.5,
        "b_gate_up": 0.02 * nrm(ks[14], (N_EXPERTS, 2 * D_FF)),
        "w_down": nrm(ks[15], (N_EXPERTS, D_FF, D_MODEL)) * D_FF ** -0.5,
        "b_down": 0.02 * nrm(ks[16], (N_EXPERTS, D_MODEL)),
        "g_final": 1.0 + 0.02 * nrm(ks[17], (D_MODEL,)),
    }


def reference(x_prompt, x_sample, g_mix, w_in, conv_w, conv_b, b_gates, g_head, w_out,
              g_ffn, w_router, b_router, w_gate_up, b_gate_up, w_down, b_down, g_final):
    y_prompt = x_prompt
    y_sample = x_sample
    for _ in range(DEPTH):
        y_prompt = _layer(y_prompt, g_mix, w_in, conv_w, conv_b, b_gates, g_head, w_out,
                          g_ffn, w_router, b_router, w_gate_up, b_gate_up, w_down, b_down)
        y_sample = _layer(y_sample, g_mix, w_in, conv_w, conv_b, b_gates, g_head, w_out,
                          g_ffn, w_router, b_router, w_gate_up, b_gate_up, w_down, b_down)
    y_prompt = rms_norm(y_prompt, g_final)
    y_sample = rms_norm(y_sample, g_final)
    return (y_prompt, y_sample)
```

```python
import functools

import jax
import jax.numpy as jnp
from jax import lax
from jax.experimental import pallas as pl
from jax.experimental.pallas import tpu as pltpu

F32 = jnp.float32
BF16 = jnp.bfloat16

D_MODEL = 1024
ATT_HEADS = 8
ATT_HEAD_DIM = 64
ATT_WIDTH = ATT_HEADS * ATT_HEAD_DIM
DILATED_CFG = ((128, 1), (512, 4), (2048, 16))
ML_HEADS = 4
ML_HEAD_DIM = 128
ML_WIDTH = ML_HEADS * ML_HEAD_DIM
ML_CHUNK = 128
CONV_W = 5
N_GATES = 2 * 2 * ML_HEADS
N_EXPERTS = 32
TOP_K = 4
D_FF = 1024
SWIGLU_LIMIT = 7.0
SWIGLU_ALPHA = 1.702
MOE_BLOCK = 256
EPS = 1e-6
NEG = -1e30

LANES = 128
IN_PAD = 7 * 512 + LANES
VMEM_LIMIT = 56 * 1024 * 1024


def _cparams(sem):
    return pltpu.CompilerParams(dimension_semantics=sem, vmem_limit_bytes=VMEM_LIMIT)


def _inproj_kernel(x_ref, g_ref, w_ref, qa_ref, ka_ref, va_ref, qkm_ref, vm_ref, om_ref, gt_ref):
    x = x_ref[...]
    xn = x * lax.rsqrt(jnp.mean(x * x, axis=-1, keepdims=True) + EPS)
    xn = (xn * g_ref[...]).astype(BF16)

    def proj(lo, hi):
        return jnp.dot(xn, w_ref[:, lo:hi], preferred_element_type=F32)

    qa_ref[...] = (proj(0, 512) * (ATT_HEAD_DIM ** -0.5)).astype(BF16)
    ka_ref[...] = proj(512, 1024).astype(BF16)
    va_ref[...] = proj(1024, 1536).astype(BF16)
    qkm_ref[...] = proj(1536, 2560)
    vm_ref[...] = proj(2560, 3072).astype(BF16)
    om_ref[...] = proj(3072, 3584)
    gt_ref[...] = proj(3584, IN_PAD)


def _inproj(x2d, g_mix, w_in_p, tm, interpret):
    T = x2d.shape[0]
    row = lambda w: pl.BlockSpec((tm, w), lambda i: (i, 0))
    return pl.pallas_call(
        _inproj_kernel,
        grid=(T // tm,),
        in_specs=[row(D_MODEL),
                  pl.BlockSpec((1, D_MODEL), lambda i: (0, 0)),
                  pl.BlockSpec((D_MODEL, IN_PAD), lambda i: (0, 0))],
        out_specs=[row(512), row(512), row(512), row(1024), row(512), row(512), row(LANES)],
        out_shape=[jax.ShapeDtypeStruct((T, 512), BF16),
                   jax.ShapeDtypeStruct((T, 512), BF16),
                   jax.ShapeDtypeStruct((T, 512), BF16),
                   jax.ShapeDtypeStruct((T, 1024), F32),
                   jax.ShapeDtypeStruct((T, 512), BF16),
                   jax.ShapeDtypeStruct((T, 512), F32),
                   jax.ShapeDtypeStruct((T, LANES), F32)],
        compiler_params=_cparams(("arbitrary",)),
        name="inproj",
        interpret=interpret,
    )(x2d, g_mix.reshape(1, D_MODEL), w_in_p)


ATT_HALF = 64


def _attn_kernel(q_ref, k_ref, v_ref, o_ref, lse_ref, *, dil, qb, kb, m_len):
    i = pl.program_id(2)
    start = jnp.clip(i * qb - ATT_HALF, 0, m_len - kb)
    start = pl.multiple_of(start, ATT_HALF)
    off = start - i * qb
    q = q_ref[0]
    k = k_ref[0, pl.ds(start, kb), :]
    v = v_ref[0, pl.ds(start, kb), :]
    row = lax.broadcasted_iota(jnp.int32, (qb, kb), 0)
    col = lax.broadcasted_iota(jnp.int32, (qb, kb), 1)
    absd = jnp.abs(col - row + off).astype(F32)
    valid = absd <= float(ATT_HALF)
    lo = lax.broadcasted_iota(jnp.int32, (1, LANES), 1) < ATT_HEAD_DIM
    lane = lax.broadcasted_iota(jnp.int32, (1, LANES), 1)
    lse_tile = jnp.zeros((qb, LANES), F32)
    zero = jnp.zeros((), BF16)
    for pair in range(ATT_HEADS // 2):
        sl = slice(pair * LANES, (pair + 1) * LANES)
        qp, kp, vp = q[:, sl], k[:, sl], v[:, sl]
        outs = []
        for hh in range(2):
            h = 2 * pair + hh
            slope = 2.0 ** (-(8.0 / ATT_HEADS) * (h + 1))
            qh = jnp.where(lo if hh == 0 else jnp.logical_not(lo), qp, zero)
            s = lax.dot_general(qh, kp, (((1,), (1,)), ((), ())), preferred_element_type=F32)
            s = jnp.where(valid, s - absd * (slope * dil), NEG)
            mx = jnp.max(s, axis=-1, keepdims=True)
            p = jnp.exp(s - mx)
            l = jnp.sum(p, axis=-1, keepdims=True)
            o = jnp.dot(p.astype(BF16), vp, preferred_element_type=F32)
            outs.append(o / l)
            lse_tile = jnp.where(lane == h, mx + jnp.log(l), lse_tile)
        o_ref[0, :, sl] = jnp.where(lo, outs[0], outs[1]).astype(BF16)
    lse_ref[0] = lse_tile


def _attn_branch(qa, ka, va, B, S, dil, interpret):
    m_len = S // dil
    qb = 128
    kb = min(qb + 2 * ATT_HALF, m_len)
    view = lambda a: a.reshape(B, m_len, dil * ATT_WIDTH)
    qspec = pl.BlockSpec((1, qb, ATT_WIDTH), lambda b, p, i: (b, i, p))
    kvspec = pl.BlockSpec((1, m_len, ATT_WIDTH), lambda b, p, i: (b, 0, p))
    o, lse = pl.pallas_call(
        functools.partial(_attn_kernel, dil=dil, qb=qb, kb=kb, m_len=m_len),
        grid=(B, dil, m_len // qb),
        in_specs=[qspec, kvspec, kvspec],
        out_specs=[qspec, pl.BlockSpec((1, qb, LANES), lambda b, p, i: (b, i, p))],
        out_shape=[jax.ShapeDtypeStruct((B, m_len, dil * ATT_WIDTH), BF16),
                   jax.ShapeDtypeStruct((B, m_len, dil * LANES), F32)],
        compiler_params=_cparams(("arbitrary", "arbitrary", "arbitrary")),
        name=f"attn_d{dil}",
        interpret=interpret,
    )(view(qa), view(ka), view(va))
    return o.reshape(B * S, ATT_WIDTH), lse.reshape(B * S, LANES)


HALO = 8


def _mlstm_kernel(*refs, direction, nc):
    if direction == 0:
        (cur_ref, prv_ref, nxt_ref, v_ref, gt_ref, cw_ref, cb_ref, bg_ref,
         h_ref, xbuf, c_st, n_st, m_st) = refs
    else:
        (cur_ref, prv_ref, nxt_ref, v_ref, gt_ref, cw_ref, cb_ref, bg_ref,
         hfw_ref, om_ref, gh_ref, h_ref, xbuf, c_st, n_st, m_st) = refs
    L = ML_CHUNK
    step = pl.program_id(1)
    c = step if direction == 0 else nc - 1 - step

    @pl.when(step == 0)
    def _():
        c_st[...] = jnp.zeros_like(c_st)
        n_st[...] = jnp.zeros_like(n_st)
        m_st[...] = jnp.full_like(m_st, NEG)

    xbuf[0:HALO, :] = jnp.where(c > 0, prv_ref[0, 0], 0.0)
    xbuf[HALO:HALO + L, :] = cur_ref[0]
    xbuf[HALO + L:, :] = jnp.where(c < nc - 1, nxt_ref[0, 0], 0.0)
    acc = cb_ref[...] + xbuf[HALO - 2:HALO - 2 + L, :] * cw_ref[0:1, :]
    for j in range(1, CONV_W):
        acc = acc + xbuf[HALO - 2 + j:HALO - 2 + j + L, :] * cw_ref[j:j + 1, :]
    qk = acc * jax.nn.sigmoid(acc)

    g = gt_ref[0] + bg_ref[...]
    logf = jax.nn.log_sigmoid(g)
    t_idx = lax.broadcasted_iota(jnp.int32, (L, L), 0)
    s_idx = lax.broadcasted_iota(jnp.int32, (L, L), 1)
    tri = (s_idx <= t_idx) if direction == 0 else (s_idx >= t_idx)
    b_all = jnp.dot(tri.astype(F32), logf, preferred_element_type=F32,
                    precision=lax.Precision.HIGHEST)
    g_t = g.T
    b_t = b_all.T
    last = L - 1 if direction == 0 else 0

    outs = []
    for h in range(ML_HEADS):
        ic = direction * 8 + h
        fc = direction * 8 + 4 + h
        hs = slice(h * ML_HEAD_DIM, (h + 1) * ML_HEAD_DIM)
        q = qk[:, hs] * (ML_HEAD_DIM ** -0.5)
        k = qk[:, ML_WIDTH + h * ML_HEAD_DIM:ML_WIDTH + (h + 1) * ML_HEAD_DIM]
        v = v_ref[0, :, hs]
        qb16 = q.astype(BF16)
        b_col = b_all[:, fc:fc + 1]
        b_row = b_t[fc:fc + 1, :]
        i_col = g[:, ic:ic + 1]
        i_row = g_t[ic:ic + 1, :]
        b_end = b_row[:, last:last + 1]
        m_prev = m_st[h:h + 1, 0:1]
        c_prev = c_st[h]
        n_prev = n_st[h:h + 1, :]

        dmat = jnp.where(tri, b_col - b_row + i_row, NEG)
        inter = b_col + m_prev
        m_t = jnp.maximum(jnp.max(dmat, axis=-1, keepdims=True), inter)
        dw = jnp.exp(dmat - m_t)
        iw = jnp.exp(inter - m_t)
        qk_s = lax.dot_general(qb16, k.astype(BF16), (((1,), (1,)), ((), ())),
                               preferred_element_type=F32) * dw
        num = jnp.dot(qk_s.astype(BF16), v, preferred_element_type=F32)
        num = num + iw * jnp.dot(qb16, c_prev.astype(BF16), preferred_element_type=F32)
        den = jnp.sum(qk_s, axis=-1, keepdims=True) + iw * jnp.sum(q * n_prev, axis=-1, keepdims=True)
        outs.append(num / jnp.maximum(jnp.abs(den), jnp.exp(-m_t)))

        a_col = b_end - b_col + i_col
        a_row = b_end - b_row + i_row
        m_loc = jnp.max(a_row, axis=-1, keepdims=True)
        kw = k * jnp.exp(a_col - m_loc)
        c_loc = jnp.dot(kw.T.astype(BF16), v, preferred_element_type=F32)
        n_loc = jnp.sum(kw, axis=0, keepdims=True)
        m_new = jnp.maximum(b_end + m_prev, m_loc)
        sp = jnp.exp(b_end + m_prev - m_new)
        sl = jnp.exp(m_loc - m_new)
        c_st[h] = sp * c_prev + sl * c_loc
        n_st[h:h + 1, :] = sp * n_prev + sl * n_loc
        m_st[h:h + 1, :] = jnp.broadcast_to(m_new, (1, LANES))

    hdir = jnp.concatenate(outs, axis=-1)
    if direction == 0:
        h_ref[0] = hdir
    else:
        hsum = hfw_ref[0] + hdir
        normed = []
        for h in range(ML_HEADS):
            hs = slice(h * ML_HEAD_DIM, (h + 1) * ML_HEAD_DIM)
            hh = hsum[:, hs]
            normed.append(hh * lax.rsqrt(jnp.mean(hh * hh, axis=-1, keepdims=True) + EPS))
        hn = jnp.concatenate(normed, axis=-1) * gh_ref[...]
        h_ref[0] = (jax.nn.sigmoid(om_ref[0]) * hn).astype(BF16)


def _mlstm_dir(qkm, vm, gates, cw_p, cb, bg_p, direction, B, S, interpret, hfw=None, om=None, gh=None):
    L = ML_CHUNK
    nc = S // L
    per = L // HALO
    cidx = (lambda s: s) if direction == 0 else (lambda s: nc - 1 - s)
    blk = lambda w: pl.BlockSpec((1, L, w), lambda b, s: (b, cidx(s), 0))
    const = lambda shape: pl.BlockSpec(shape, lambda b, s: (0,) * len(shape))
    halo_prev = pl.BlockSpec((1, 1, HALO, 2 * ML_WIDTH),
                             lambda b, s: (b, jnp.maximum(cidx(s) * per - 1, 0), 0, 0))
    halo_next = pl.BlockSpec((1, 1, HALO, 2 * ML_WIDTH),
                             lambda b, s: (b, jnp.minimum(cidx(s) * per + per, nc * per - 1), 0, 0))
    qk3 = qkm.reshape(B, S, 2 * ML_WIDTH)
    qk4 = qkm.reshape(B, S // HALO, HALO, 2 * ML_WIDTH)
    in_specs = [blk(2 * ML_WIDTH), halo_prev, halo_next, blk(ML_WIDTH), blk(LANES),
                const((8, 2 * ML_WIDTH)), const((1, 2 * ML_WIDTH)), const((1, LANES))]
    args = [qk3, qk4, qk4, vm.reshape(B, S, ML_WIDTH), gates.reshape(B, S, LANES), cw_p, cb, bg_p]
    if direction == 1:
        in_specs += [blk(ML_WIDTH), blk(ML_WIDTH), const((1, ML_WIDTH))]
        args += [hfw, om.reshape(B, S, ML_WIDTH), gh]
    out_dtype = F32 if direction == 0 else BF16
    return pl.pallas_call(
        functools.partial(_mlstm_kernel, direction=direction, nc=nc),
        grid=(B, nc),
        in_specs=in_specs,
        out_specs=blk(ML_WIDTH),
        out_shape=jax.ShapeDtypeStruct((B, S, ML_WIDTH), out_dtype),
        scratch_shapes=[pltpu.VMEM((L + 2 * HALO, 2 * ML_WIDTH), F32),
                        pltpu.VMEM((ML_HEADS, ML_HEAD_DIM, ML_HEAD_DIM), F32),
                        pltpu.VMEM((8, ML_HEAD_DIM), F32),
                        pltpu.VMEM((8, LANES), F32)],
        compiler_params=_cparams(("arbitrary", "arbitrary")),
        name=f"mlstm_dir{direction}",
        interpret=interpret,
    )(*args)


def _outproj_kernel(x_ref, o1_ref, o2_ref, o3_ref, l1_ref, l2_ref, l3_ref, ml_ref, wo_ref,
                    gf_ref, wr_ref, br_ref, h_ref, hn_ref, idx_ref, gate_ref, cnt_ref, carry,
                    *, tm):
    step = pl.program_id(0)

    @pl.when(step == 0)
    def _():
        carry[...] = jnp.zeros_like(carry)

    l1, l2, l3 = l1_ref[...], l2_ref[...], l3_ref[...]
    mx = jnp.maximum(jnp.maximum(l1, l2), l3)
    e1, e2, e3 = jnp.exp(l1 - mx), jnp.exp(l2 - mx), jnp.exp(l3 - mx)
    inv = 1.0 / (e1 + e2 + e3)
    er = lax.broadcasted_iota(jnp.int32, (LANES, ATT_WIDTH), 0)
    ec = lax.broadcasted_iota(jnp.int32, (LANES, ATT_WIDTH), 1)
    expand = (ec // ATT_HEAD_DIM == er).astype(BF16)

    def widen(w):
        hi = w.astype(BF16)
        lo = (w - hi.astype(F32)).astype(BF16)
        return (jnp.dot(hi, expand, preferred_element_type=F32)
                + jnp.dot(lo, expand, preferred_element_type=F32))

    attn = (widen(e1 * inv) * o1_ref[...].astype(F32)
            + widen(e2 * inv) * o2_ref[...].astype(F32)
            + widen(e3 * inv) * o3_ref[...].astype(F32))
    mix = jnp.concatenate([attn.astype(BF16), ml_ref[...]], axis=-1)
    h = x_ref[...] + jnp.dot(mix, wo_ref[...], preferred_element_type=F32)
    h_ref[...] = h
    hn = h * lax.rsqrt(jnp.mean(h * h, axis=-1, keepdims=True) + EPS) * gf_ref[...]
    hn_ref[...] = hn

    lane = lax.broadcasted_iota(jnp.int32, (tm, LANES), 1)
    logits = jnp.dot(hn.astype(BF16), wr_ref[...], preferred_element_type=F32) + br_ref[...]
    work = jnp.where(lane < N_EXPERTS, logits, -jnp.inf)
    vals, idxs, hots = [], [], []
    for _ in range(TOP_K):
        top = jnp.max(work, axis=-1, keepdims=True)
        idx = jnp.min(jnp.where(work == top, lane, LANES), axis=-1, keepdims=True)
        hot = lane == idx
        work = jnp.where(hot, -jnp.inf, work)
        vals.append(top)
        idxs.append(idx)
        hots.append(hot)
    exps = [jnp.exp(v - vals[0]) for v in vals]
    inv_den = 1.0 / (exps[0] + exps[1] + exps[2] + exps[3])

    cnt = (hots[0] | hots[1] | hots[2] | hots[3]).astype(BF16)
    r_idx = lax.broadcasted_iota(jnp.int32, (tm, tm), 0)
    c_idx = lax.broadcasted_iota(jnp.int32, (tm, tm), 1)
    before = (c_idx < r_idx).astype(BF16)
    rank_all = jnp.dot(before, cnt, preferred_element_type=F32) + carry[0:1, :]
    carry[...] = carry[...] + jnp.sum(cnt.astype(F32), axis=0, keepdims=True)
    cnt_ref[...] = carry[...]

    idx_tile = jnp.zeros((tm, LANES), jnp.int32)
    gate_tile = jnp.zeros((tm, LANES), F32)
    for kk in range(TOP_K):
        rank = jnp.sum(jnp.where(hots[kk], rank_all, 0.0), axis=-1, keepdims=True).astype(jnp.int32)
        idx_tile = jnp.where(lane == kk, idxs[kk], idx_tile)
        idx_tile = jnp.where(lane == TOP_K + kk, rank, idx_tile)
        gate_tile = jnp.where(lane == kk, exps[kk] * inv_den, gate_tile)
    idx_ref[...] = idx_tile
    gate_ref[...] = gate_tile


def _outproj(x2d, o1, o2, o3, l1, l2, l3, ml, w_out_b, g_ffn, w_r_p, b_r_p, tm, interpret):
    T = x2d.shape[0]
    row = lambda w: pl.BlockSpec((tm, w), lambda i: (i, 0))
    const = lambda shape: pl.BlockSpec(shape, lambda i: (0,) * len(shape))
    return pl.pallas_call(
        functools.partial(_outproj_kernel, tm=tm),
        grid=(T // tm,),
        in_specs=[row(D_MODEL), row(512), row(512), row(512), row(LANES), row(LANES), row(LANES),
                  row(512), const((D_MODEL, D_MODEL)), const((1, D_MODEL)),
                  const((D_MODEL, LANES)), const((1, LANES))],
        out_specs=[row(D_MODEL), row(D_MODEL), row(LANES), row(LANES), const((8, LANES))],
        out_shape=[jax.ShapeDtypeStruct((T, D_MODEL), F32),
                   jax.ShapeDtypeStruct((T, D_MODEL), F32),
                   jax.ShapeDtypeStruct((T, LANES), jnp.int32),
                   jax.ShapeDtypeStruct((T, LANES), F32),
                   jax.ShapeDtypeStruct((8, LANES), F32)],
        scratch_shapes=[pltpu.VMEM((8, LANES), F32)],
        compiler_params=_cparams(("arbitrary",)),
        name="outproj_router",
        interpret=interpret,
    )(x2d, o1, o2, o3, l1, l2, l3, ml, w_out_b, g_ffn.reshape(1, D_MODEL), w_r_p, b_r_p)


def _gather_kernel(tok_ref, src_ref, out_ref, sem, *, rows):
    def issue(r, carry):
        tok = tok_ref[0, 0, r]
        pltpu.make_async_copy(src_ref.at[pl.ds(tok, 1)], out_ref.at[pl.ds(r, 1)], sem).start()
        return carry

    lax.fori_loop(0, rows, issue, 0, unroll=8)

    def drain(r, carry):
        pltpu.make_async_copy(src_ref.at[pl.ds(0, 1)], out_ref.at[pl.ds(r, 1)], sem).wait()
        return carry

    lax.fori_loop(0, rows, drain, 0, unroll=8)


def _gather_rows(src, slot_tok, rows, interpret):
    P = slot_tok.shape[0]
    width = src.shape[1]
    return pl.pallas_call(
        functools.partial(_gather_kernel, rows=rows),
        grid=(P // rows,),
        in_specs=[pl.BlockSpec((1, 1, rows), lambda j: (j, 0, 0), memory_space=pltpu.SMEM),
                  pl.BlockSpec(memory_space=pl.ANY)],
        out_specs=pl.BlockSpec((rows, width), lambda j: (j, 0)),
        out_shape=jax.ShapeDtypeStruct((P, width), src.dtype),
        scratch_shapes=[pltpu.SemaphoreType.DMA],
        compiler_params=_cparams(("arbitrary",)),
        name="moe_gather",
        interpret=interpret,
    )(slot_tok.reshape(P // rows, 1, rows), src)


def _expert_kernel(be_ref, nb_ref, x_ref, wgu_ref, bgu_ref, wd_ref, bd_ref, y_ref, wgu_b, wd_b):
    j = pl.program_id(0)
    active = j < nb_ref[0]
    fresh = jnp.logical_or(j == 0, be_ref[j] != be_ref[jnp.maximum(j - 1, 0)])

    @pl.when(jnp.logical_and(active, fresh))
    def _():
        wgu_b[...] = wgu_ref[0].astype(BF16)
        wd_b[...] = wd_ref[0].astype(BF16)

    @pl.when(active)
    def _():
        x = x_ref[...].astype(BF16)
        gu = jnp.dot(x, wgu_b[...], preferred_element_type=F32) + bgu_ref[0]
        g = jnp.minimum(gu[:, :D_FF], SWIGLU_LIMIT)
        u = jnp.clip(gu[:, D_FF:], -SWIGLU_LIMIT, SWIGLU_LIMIT)
        hdn = (u + 1.0) * (g * jax.nn.sigmoid(SWIGLU_ALPHA * g))
        y_ref[...] = jnp.dot(hdn.astype(BF16), wd_b[...], preferred_element_type=F32) + bd_ref[0]

    @pl.when(jnp.logical_not(active))
    def _():
        y_ref[...] = jnp.zeros_like(y_ref)


def _experts(xs, block_e, n_used, w_gate_up, b_gate_up, w_down, b_down, interpret):
    P = xs.shape[0]
    nb = P // MOE_BLOCK
    grid_spec = pltpu.PrefetchScalarGridSpec(
        num_scalar_prefetch=2,
        grid=(nb,),
        in_specs=[pl.BlockSpec((MOE_BLOCK, D_MODEL), lambda j, be, nu: (j, 0)),
                  pl.BlockSpec((1, D_MODEL, 2 * D_FF), lambda j, be, nu: (be[j], 0, 0)),
                  pl.BlockSpec((1, 1, 2 * D_FF), lambda j, be, nu: (be[j], 0, 0)),
                  pl.BlockSpec((1, D_FF, D_MODEL), lambda j, be, nu: (be[j], 0, 0)),
                  pl.BlockSpec((1, 1, D_MODEL), lambda j, be, nu: (be[j], 0, 0))],
        out_specs=pl.BlockSpec((MOE_BLOCK, D_MODEL), lambda j, be, nu: (j, 0)),
        scratch_shapes=[pltpu.VMEM((D_MODEL, 2 * D_FF), BF16),
                        pltpu.VMEM((D_FF, D_MODEL), BF16)],
    )
    return pl.pallas_call(
        _expert_kernel,
        grid_spec=grid_spec,
        out_shape=jax.ShapeDtypeStruct((P, D_MODEL), F32),
        compiler_params=_cparams(("arbitrary",)),
        name="moe_experts",
        interpret=interpret,
    )(block_e, n_used, xs, w_gate_up, b_gate_up.reshape(N_EXPERTS, 1, 2 * D_FF),
      w_down, b_down.reshape(N_EXPERTS, 1, D_MODEL))


def _combine_kernel(dest_ref, y_ref, h_ref, gate_ref, gfin_ref, out_ref, ybuf, sem, *, tm):
    def issue(r, carry):
        for kk in range(TOP_K):
            d = dest_ref[0, 0, r * TOP_K + kk]
            pltpu.make_async_copy(y_ref.at[pl.ds(d, 1)], ybuf.at[kk, pl.ds(r, 1)], sem).start()
        return carry

    lax.fori_loop(0, tm, issue, 0, unroll=4)

    def drain(r, carry):
        for kk in range(TOP_K):
            pltpu.make_async_copy(y_ref.at[pl.ds(0, 1)], ybuf.at[kk, pl.ds(r, 1)], sem).wait()
        return carry

    lax.fori_loop(0, tm, drain, 0, unroll=4)

    gate = gate_ref[...]
    moe = gate[:, 0:1] * ybuf[0]
    for kk in range(1, TOP_K):
        moe = moe + gate[:, kk:kk + 1] * ybuf[kk]
    h = h_ref[...] + moe
    out_ref[...] = h * lax.rsqrt(jnp.mean(h * h, axis=-1, keepdims=True) + EPS) * gfin_ref[...]


def _combine(y, dest, h, gate, g_final, tm, interpret):
    T = h.shape[0]
    return pl.pallas_call(
        functools.partial(_combine_kernel, tm=tm),
        grid=(T // tm,),
        in_specs=[pl.BlockSpec((1, 1, tm * TOP_K), lambda i: (i, 0, 0), memory_space=pltpu.SMEM),
                  pl.BlockSpec(memory_space=pl.ANY),
                  pl.BlockSpec((tm, D_MODEL), lambda i: (i, 0)),
                  pl.BlockSpec((tm, LANES), lambda i: (i, 0)),
                  pl.BlockSpec((1, D_MODEL), lambda i: (0, 0))],
        out_specs=pl.BlockSpec((tm, D_MODEL), lambda i: (i, 0)),
        out_shape=jax.ShapeDtypeStruct((T, D_MODEL), F32),
        scratch_shapes=[pltpu.VMEM((TOP_K, tm, D_MODEL), F32), pltpu.SemaphoreType.DMA],
        compiler_params=_cparams(("arbitrary",)),
        name="moe_combine",
        interpret=interpret,
    )(dest.reshape(T // tm, 1, tm * TOP_K), y, h, gate, g_final.reshape(1, D_MODEL))


def _routing_tables(idx_tile, counts, T):
    top_idx = idx_tile[:, :TOP_K]
    rank = idx_tile[:, TOP_K:2 * TOP_K]
    counts = counts.astype(jnp.int32)
    blocks_e = (counts + MOE_BLOCK - 1) // MOE_BLOCK
    cum_blocks = jnp.cumsum(blocks_e)
    start_p = (cum_blocks - blocks_e) * MOE_BLOCK
    dest = start_p[top_idx] + rank
    n_blocks = (T * TOP_K) // MOE_BLOCK + N_EXPERTS
    tok = jnp.broadcast_to(jnp.arange(T, dtype=jnp.int32)[:, None], (T, TOP_K))
    slot_tok = jnp.zeros((n_blocks * MOE_BLOCK,), jnp.int32).at[dest.reshape(-1)].set(tok.reshape(-1))
    n_used = cum_blocks[-1:].astype(jnp.int32)
    block_e = jnp.searchsorted(cum_blocks, jnp.arange(n_blocks, dtype=jnp.int32), side='right')
    block_e = jnp.minimum(block_e, jnp.take(block_e, jnp.maximum(n_used[0] - 1, 0))).astype(jnp.int32)
    return dest.astype(jnp.int32), slot_tok, block_e, n_used


def _layer(x, g_mix, w_in, conv_w, conv_b, b_gates, g_head, w_out, g_ffn, w_router, b_router,
           w_gate_up, b_gate_up, w_down, b_down, g_final, interpret=False):
    B, S, D = x.shape
    T = B * S
    x2d = x.reshape(T, D)

    w_in_p = jnp.pad(w_in, ((0, 0), (0, IN_PAD - w_in.shape[1]))).astype(BF16)
    qa, ka, va, qkm, vm, om, gates = _inproj(x2d, g_mix, w_in_p, 512, interpret)

    branches = [_attn_branch(qa, ka, va, B, S, dil, interpret) for _, dil in DILATED_CFG]

    cw_p = jnp.pad(conv_w, ((0, 8 - CONV_W), (0, 0)))
    cb = conv_b.reshape(1, 2 * ML_WIDTH)
    bg_p = jnp.pad(b_gates, (0, LANES - N_GATES)).reshape(1, LANES)
    hfw = _mlstm_dir(qkm, vm, gates, cw_p, cb, bg_p, 0, B, S, interpret)
    ml = _mlstm_dir(qkm, vm, gates, cw_p, cb, bg_p, 1, B, S, interpret,
                    hfw=hfw, om=om, gh=g_head.reshape(1, ML_WIDTH))

    w_r_p = jnp.pad(w_router, ((0, 0), (0, LANES - N_EXPERTS))).astype(BF16)
    b_r_p = jnp.pad(b_router, (0, LANES - N_EXPERTS)).reshape(1, LANES)
    (o1, l1), (o2, l2), (o3, l3) = branches
    h, hn, idx_tile, gate_tile, counts = _outproj(
        x2d, o1, o2, o3, l1, l2, l3, ml.reshape(T, ML_WIDTH), w_out.astype(BF16), g_ffn,
        w_r_p, b_r_p, 512, interpret)

    dest, slot_tok, block_e, n_used = _routing_tables(idx_tile, counts[0, :N_EXPERTS], T)
    xs = _gather_rows(hn, slot_tok, MOE_BLOCK, interpret)
    y = _experts(xs, block_e, n_used, w_gate_up, b_gate_up, w_down, b_down, interpret)
    out = _combine(y, dest, h, gate_tile, g_final, 256, interpret)
    return out.reshape(B, S, D)


def kernel(x_prompt, x_sample, g_mix, w_in, conv_w, conv_b, b_gates, g_head, w_out, g_ffn, w_router,
           b_router, w_gate_up, b_gate_up, w_down, b_down, g_final):
    nb = x_prompt.shape[0]
    x = jnp.concatenate([x_prompt, x_sample], axis=0)
    y = _layer(x, g_mix, w_in, conv_w, conv_b, b_gates, g_head, w_out, g_ffn, w_router, b_router,
               w_gate_up, b_gate_up, w_down, b_down, g_final)
    return (y[:nb], y[nb:])
```

```python
import functools

import jax
import jax.numpy as jnp
from jax import lax
from jax.experimental import pallas as pl
from jax.experimental.pallas import tpu as pltpu

F32 = jnp.float32
BF16 = jnp.bfloat16

D_MODEL = 1024
ATT_HEADS = 8
ATT_HEAD_DIM = 64
ATT_WIDTH = ATT_HEADS * ATT_HEAD_DIM
DILATED_CFG = ((128, 1), (512, 4), (2048, 16))
ML_HEADS = 4
ML_HEAD_DIM = 128
ML_WIDTH = ML_HEADS * ML_HEAD_DIM
ML_CHUNK = 128
CONV_W = 5
N_GATES = 2 * 2 * ML_HEADS
N_EXPERTS = 32
TOP_K = 4
D_FF = 1024
SWIGLU_LIMIT = 7.0
SWIGLU_ALPHA = 1.702
MOE_BLOCK = 512
EPS = 1e-6
NEG = -1e30

LANES = 128
IN_PAD = 7 * 512 + LANES
VMEM_LIMIT = 56 * 1024 * 1024


def _cparams(sem):
    return pltpu.CompilerParams(dimension_semantics=sem, vmem_limit_bytes=VMEM_LIMIT)


def _inproj_kernel(x_ref, g_ref, w_ref, qa_ref, ka_ref, va_ref, qkm_ref, vm_ref, om_ref, gt_ref):
    x = x_ref[...]
    xn = x * lax.rsqrt(jnp.mean(x * x, axis=-1, keepdims=True) + EPS)
    xn = (xn * g_ref[...]).astype(BF16)

    def proj(lo, hi):
        return jnp.dot(xn, w_ref[:, lo:hi], preferred_element_type=F32)

    qa_ref[...] = (proj(0, 512) * (ATT_HEAD_DIM ** -0.5)).astype(BF16)
    ka_ref[...] = proj(512, 1024).astype(BF16)
    va_ref[...] = proj(1024, 1536).astype(BF16)
    qkm_ref[...] = proj(1536, 2560)
    vm_ref[...] = proj(2560, 3072).astype(BF16)
    om_ref[...] = proj(3072, 3584)
    gt_ref[...] = proj(3584, IN_PAD)


def _inproj(x2d, g_mix, w_in_p, tm, interpret):
    T = x2d.shape[0]
    row = lambda w: pl.BlockSpec((tm, w), lambda i: (i, 0))
    return pl.pallas_call(
        _inproj_kernel,
        grid=(T // tm,),
        in_specs=[row(D_MODEL),
                  pl.BlockSpec((1, D_MODEL), lambda i: (0, 0)),
                  pl.BlockSpec((D_MODEL, IN_PAD), lambda i: (0, 0))],
        out_specs=[row(512), row(512), row(512), row(1024), row(512), row(512), row(LANES)],
        out_shape=[jax.ShapeDtypeStruct((T, 512), BF16),
                   jax.ShapeDtypeStruct((T, 512), BF16),
                   jax.ShapeDtypeStruct((T, 512), BF16),
                   jax.ShapeDtypeStruct((T, 1024), F32),
                   jax.ShapeDtypeStruct((T, 512), BF16),
                   jax.ShapeDtypeStruct((T, 512), F32),
                   jax.ShapeDtypeStruct((T, LANES), F32)],
        compiler_params=_cparams(("arbitrary",)),
        name="inproj",
        interpret=interpret,
    )(x2d, g_mix.reshape(1, D_MODEL), w_in_p)


ATT_HALF = 64


def _attn_kernel(q_ref, k_ref, v_ref, o_ref, lse_ref, *, dil, qb, kb, m_len):
    i = pl.program_id(2)
    start = jnp.clip(i * qb - ATT_HALF, 0, m_len - kb)
    start = pl.multiple_of(start, ATT_HALF)
    off = start - i * qb
    q = q_ref[0]
    k = k_ref[0, pl.ds(start, kb), :]
    v = v_ref[0, pl.ds(start, kb), :]
    row = lax.broadcasted_iota(jnp.int32, (qb, kb), 0)
    col = lax.broadcasted_iota(jnp.int32, (qb, kb), 1)
    absd = jnp.abs(col - row + off).astype(F32)
    valid = absd <= float(ATT_HALF)
    lo = lax.broadcasted_iota(jnp.int32, (1, LANES), 1) < ATT_HEAD_DIM
    lane = lax.broadcasted_iota(jnp.int32, (1, LANES), 1)
    lse_tile = jnp.zeros((qb, LANES), F32)
    zero = jnp.zeros((), BF16)
    for pair in range(ATT_HEADS // 2):
        sl = slice(pair * LANES, (pair + 1) * LANES)
        qp, kp, vp = q[:, sl], k[:, sl], v[:, sl]
        outs = []
        for hh in range(2):
            h = 2 * pair + hh
            slope = 2.0 ** (-(8.0 / ATT_HEADS) * (h + 1))
            qh = jnp.where(lo if hh == 0 else jnp.logical_not(lo), qp, zero)
            s = lax.dot_general(qh, kp, (((1,), (1,)), ((), ())), preferred_element_type=F32)
            s = jnp.where(valid, s - absd * (slope * dil), NEG)
            mx = jnp.max(s, axis=-1, keepdims=True)
            p = jnp.exp(s - mx)
            l = jnp.sum(p, axis=-1, keepdims=True)
            o = jnp.dot(p.astype(BF16), vp, preferred_element_type=F32)
            outs.append(o / l)
            lse_tile = jnp.where(lane == h, mx + jnp.log(l), lse_tile)
        o_ref[0, :, sl] = jnp.where(lo, outs[0], outs[1]).astype(BF16)
    lse_ref[0] = lse_tile


def _attn_branch(qa, ka, va, B, S, dil, interpret):
    m_len = S // dil
    qb = 128
    kb = min(qb + 2 * ATT_HALF, m_len)
    view = lambda a: a.reshape(B, m_len, dil * ATT_WIDTH)
    qspec = pl.BlockSpec((1, qb, ATT_WIDTH), lambda b, p, i: (b, i, p))
    kvspec = pl.BlockSpec((1, m_len, ATT_WIDTH), lambda b, p, i: (b, 0, p))
    o, lse = pl.pallas_call(
        functools.partial(_attn_kernel, dil=dil, qb=qb, kb=kb, m_len=m_len),
        grid=(B, dil, m_len // qb),
        in_specs=[qspec, kvspec, kvspec],
        out_specs=[qspec, pl.BlockSpec((1, qb, LANES), lambda b, p, i: (b, i, p))],
        out_shape=[jax.ShapeDtypeStruct((B, m_len, dil * ATT_WIDTH), BF16),
                   jax.ShapeDtypeStruct((B, m_len, dil * LANES), F32)],
        compiler_params=_cparams(("arbitrary", "arbitrary", "arbitrary")),
        name=f"attn_d{dil}",
        interpret=interpret,
    )(view(qa), view(ka), view(va))
    return o.reshape(B * S, ATT_WIDTH), lse.reshape(B * S, LANES)


HALO = 8


def _mlstm_kernel(*refs, direction, nc):
    if direction == 0:
        (cur_ref, prv_ref, nxt_ref, v_ref, gt_ref, cw_ref, cb_ref, bg_ref,
         h_ref, xbuf, c_st, n_st, m_st) = refs
    else:
        (cur_ref, prv_ref, nxt_ref, v_ref, gt_ref, cw_ref, cb_ref, bg_ref,
         hfw_ref, om_ref, gh_ref, h_ref, xbuf, c_st, n_st, m_st) = refs
    L = ML_CHUNK
    step = pl.program_id(1)
    c = step if direction == 0 else nc - 1 - step

    @pl.when(step == 0)
    def _():
        c_st[...] = jnp.zeros_like(c_st)
        n_st[...] = jnp.zeros_like(n_st)
        m_st[...] = jnp.full_like(m_st, NEG)

    xbuf[0:HALO, :] = jnp.where(c > 0, prv_ref[0, 0], 0.0)
    xbuf[HALO:HALO + L, :] = cur_ref[0]
    xbuf[HALO + L:, :] = jnp.where(c < nc - 1, nxt_ref[0, 0], 0.0)
    acc = cb_ref[...] + xbuf[HALO - 2:HALO - 2 + L, :] * cw_ref[0:1, :]
    for j in range(1, CONV_W):
        acc = acc + xbuf[HALO - 2 + j:HALO - 2 + j + L, :] * cw_ref[j:j + 1, :]
    qk = acc * jax.nn.sigmoid(acc)

    g = gt_ref[0] + bg_ref[...]
    logf = jax.nn.log_sigmoid(g)
    t_idx = lax.broadcasted_iota(jnp.int32, (L, L), 0)
    s_idx = lax.broadcasted_iota(jnp.int32, (L, L), 1)
    tri = (s_idx <= t_idx) if direction == 0 else (s_idx >= t_idx)
    b_all = jnp.dot(tri.astype(F32), logf, preferred_element_type=F32,
                    precision=lax.Precision.HIGHEST)
    g_t = g.T
    b_t = b_all.T
    last = L - 1 if direction == 0 else 0

    outs = []
    for h in range(ML_HEADS):
        ic = direction * 8 + h
        fc = direction * 8 + 4 + h
        hs = slice(h * ML_HEAD_DIM, (h + 1) * ML_HEAD_DIM)
        q = qk[:, hs] * (ML_HEAD_DIM ** -0.5)
        k = qk[:, ML_WIDTH + h * ML_HEAD_DIM:ML_WIDTH + (h + 1) * ML_HEAD_DIM]
        v = v_ref[0, :, hs]
        qb16 = q.astype(BF16)
        b_col = b_all[:, fc:fc + 1]
        b_row = b_t[fc:fc + 1, :]
        i_col = g[:, ic:ic + 1]
        i_row = g_t[ic:ic + 1, :]
        b_end = b_row[:, last:last + 1]
        m_prev = m_st[h:h + 1, 0:1]
        c_prev = c_st[h]
        n_prev = n_st[h:h + 1, :]

        dmat = jnp.where(tri, b_col - b_row + i_row, NEG)
        inter = b_col + m_prev
        m_t = jnp.maximum(jnp.max(dmat, axis=-1, keepdims=True), inter)
        dw = jnp.exp(dmat - m_t)
        iw = jnp.exp(inter - m_t)
        qk_s = lax.dot_general(qb16, k.astype(BF16), (((1,), (1,)), ((), ())),
                               preferred_element_type=F32) * dw
        num = jnp.dot(qk_s.astype(BF16), v, preferred_element_type=F32)
        num = num + iw * jnp.dot(qb16, c_prev.astype(BF16), preferred_element_type=F32)
        den = jnp.sum(qk_s, axis=-1, keepdims=True) + iw * jnp.sum(q * n_prev, axis=-1, keepdims=True)
        outs.append(num / jnp.maximum(jnp.abs(den), jnp.exp(-m_t)))

        a_col = b_end - b_col + i_col
        a_row = b_end - b_row + i_row
        m_loc = jnp.max(a_row, axis=-1, keepdims=True)
        kw = k * jnp.exp(a_col - m_loc)
        c_loc = jnp.dot(kw.T.astype(BF16), v, preferred_element_type=F32)
        n_loc = jnp.sum(kw, axis=0, keepdims=True)
        m_new = jnp.maximum(b_end + m_prev, m_loc)
        sp = jnp.exp(b_end + m_prev - m_new)
        sl = jnp.exp(m_loc - m_new)
        c_st[h] = sp * c_prev + sl * c_loc
        n_st[h:h + 1, :] = sp * n_prev + sl * n_loc
        m_st[h:h + 1, :] = jnp.broadcast_to(m_new, (1, LANES))

    hdir = jnp.concatenate(outs, axis=-1)
    if direction == 0:
        h_ref[0] = hdir
    else:
        hsum = hfw_ref[0] + hdir
        normed = []
        for h in range(ML_HEADS):
            hs = slice(h * ML_HEAD_DIM, (h + 1) * ML_HEAD_DIM)
            hh = hsum[:, hs]
            normed.append(hh * lax.rsqrt(jnp.mean(hh * hh, axis=-1, keepdims=True) + EPS))
        hn = jnp.concatenate(normed, axis=-1) * gh_ref[...]
        h_ref[0] = (jax.nn.sigmoid(om_ref[0]) * hn).astype(BF16)


def _mlstm_dir(qkm, vm, gates, cw_p, cb, bg_p, direction, B, S, interpret, hfw=None, om=None, gh=None):
    L = ML_CHUNK
    nc = S // L
    per = L // HALO
    cidx = (lambda s: s) if direction == 0 else (lambda s: nc - 1 - s)
    blk = lambda w: pl.BlockSpec((1, L, w), lambda b, s: (b, cidx(s), 0))
    const = lambda shape: pl.BlockSpec(shape, lambda b, s: (0,) * len(shape))
    halo_prev = pl.BlockSpec((1, 1, HALO, 2 * ML_WIDTH),
                             lambda b, s: (b, jnp.maximum(cidx(s) * per - 1, 0), 0, 0))
    halo_next = pl.BlockSpec((1, 1, HALO, 2 * ML_WIDTH),
                             lambda b, s: (b, jnp.minimum(cidx(s) * per + per, nc * per - 1), 0, 0))
    qk3 = qkm.reshape(B, S, 2 * ML_WIDTH)
    qk4 = qkm.reshape(B, S // HALO, HALO, 2 * ML_WIDTH)
    in_specs = [blk(2 * ML_WIDTH), halo_prev, halo_next, blk(ML_WIDTH), blk(LANES),
                const((8, 2 * ML_WIDTH)), const((1, 2 * ML_WIDTH)), const((1, LANES))]
    args = [qk3, qk4, qk4, vm.reshape(B, S, ML_WIDTH), gates.reshape(B, S, LANES), cw_p, cb, bg_p]
    if direction == 1:
        in_specs += [blk(ML_WIDTH), blk(ML_WIDTH), const((1, ML_WIDTH))]
        args += [hfw, om.reshape(B, S, ML_WIDTH), gh]
    out_dtype = F32 if direction == 0 else BF16
    return pl.pallas_call(
        functools.partial(_mlstm_kernel, direction=direction, nc=nc),
        grid=(B, nc),
        in_specs=in_specs,
        out_specs=blk(ML_WIDTH),
        out_shape=jax.ShapeDtypeStruct((B, S, ML_WIDTH), out_dtype),
        scratch_shapes=[pltpu.VMEM((L + 2 * HALO, 2 * ML_WIDTH), F32),
                        pltpu.VMEM((ML_HEADS, ML_HEAD_DIM, ML_HEAD_DIM), F32),
                        pltpu.VMEM((8, ML_HEAD_DIM), F32),
                        pltpu.VMEM((8, LANES), F32)],
        compiler_params=_cparams(("arbitrary", "arbitrary")),
        name=f"mlstm_dir{direction}",
        interpret=interpret,
    )(*args)


def _outproj_kernel(x_ref, o1_ref, o2_ref, o3_ref, l1_ref, l2_ref, l3_ref, ml_ref, wo_ref,
                    gf_ref, wr_ref, br_ref, h_ref, hn_ref, idx_ref, gate_ref, cnt_ref, carry,
                    *, tm):
    step = pl.program_id(0)

    @pl.when(step == 0)
    def _():
        carry[...] = jnp.zeros_like(carry)

    l1, l2, l3 = l1_ref[...], l2_ref[...], l3_ref[...]
    mx = jnp.maximum(jnp.maximum(l1, l2), l3)
    e1, e2, e3 = jnp.exp(l1 - mx), jnp.exp(l2 - mx), jnp.exp(l3 - mx)
    inv = 1.0 / (e1 + e2 + e3)
    er = lax.broadcasted_iota(jnp.int32, (LANES, ATT_WIDTH), 0)
    ec = lax.broadcasted_iota(jnp.int32, (LANES, ATT_WIDTH), 1)
    expand = (ec // ATT_HEAD_DIM == er).astype(BF16)

    def widen(w):
        hi = w.astype(BF16)
        lo = (w - hi.astype(F32)).astype(BF16)
        return (jnp.dot(hi, expand, preferred_element_type=F32)
                + jnp.dot(lo, expand, preferred_element_type=F32))

    attn = (widen(e1 * inv) * o1_ref[...].astype(F32)
            + widen(e2 * inv) * o2_ref[...].astype(F32)
            + widen(e3 * inv) * o3_ref[...].astype(F32))
    mix = jnp.concatenate([attn.astype(BF16), ml_ref[...]], axis=-1)
    h = x_ref[...] + jnp.dot(mix, wo_ref[...], preferred_element_type=F32)
    h_ref[...] = h
    hn = h * lax.rsqrt(jnp.mean(h * h, axis=-1, keepdims=True) + EPS) * gf_ref[...]
    hn_ref[...] = hn

    lane = lax.broadcasted_iota(jnp.int32, (tm, LANES), 1)
    logits = jnp.dot(hn.astype(BF16), wr_ref[...], preferred_element_type=F32) + br_ref[...]
    work = jnp.where(lane < N_EXPERTS, logits, -jnp.inf)
    vals, idxs, hots = [], [], []
    for _ in range(TOP_K):
        top = jnp.max(work, axis=-1, keepdims=True)
        idx = jnp.min(jnp.where(work == top, lane, LANES), axis=-1, keepdims=True)
        hot = lane == idx
        work = jnp.where(hot, -jnp.inf, work)
        vals.append(top)
        idxs.append(idx)
        hots.append(hot)
    exps = [jnp.exp(v - vals[0]) for v in vals]
    inv_den = 1.0 / (exps[0] + exps[1] + exps[2] + exps[3])

    cnt = (hots[0] | hots[1] | hots[2] | hots[3]).astype(BF16)
    r_idx = lax.broadcasted_iota(jnp.int32, (tm, tm), 0)
    c_idx = lax.broadcasted_iota(jnp.int32, (tm, tm), 1)
    before = (c_idx < r_idx).astype(BF16)
    rank_all = jnp.dot(before, cnt, preferred_element_type=F32) + carry[0:1, :]
    carry[...] = carry[...] + jnp.sum(cnt.astype(F32), axis=0, keepdims=True)
    cnt_ref[...] = carry[...]

    idx_tile = jnp.zeros((tm, LANES), jnp.int32)
    gate_tile = jnp.zeros((tm, LANES), F32)
    for kk in range(TOP_K):
        rank = jnp.sum(jnp.where(hots[kk], rank_all, 0.0), axis=-1, keepdims=True).astype(jnp.int32)
        idx_tile = jnp.where(lane == kk, idxs[kk], idx_tile)
        idx_tile = jnp.where(lane == TOP_K + kk, rank, idx_tile)
        gate_tile = jnp.where(lane == kk, exps[kk] * inv_den, gate_tile)
    idx_ref[...] = idx_tile
    gate_ref[...] = gate_tile


def _outproj(x2d, o1, o2, o3, l1, l2, l3, ml, w_out_b, g_ffn, w_r_p, b_r_p, tm, interpret):
    T = x2d.shape[0]
    row = lambda w: pl.BlockSpec((tm, w), lambda i: (i, 0))
    const = lambda shape: pl.BlockSpec(shape, lambda i: (0,) * len(shape))
    return pl.pallas_call(
        functools.partial(_outproj_kernel, tm=tm),
        grid=(T // tm,),
        in_specs=[row(D_MODEL), row(512), row(512), row(512), row(LANES), row(LANES), row(LANES),
                  row(512), const((D_MODEL, D_MODEL)), const((1, D_MODEL)),
                  const((D_MODEL, LANES)), const((1, LANES))],
        out_specs=[row(D_MODEL), row(D_MODEL), row(LANES), row(LANES), const((8, LANES))],
        out_shape=[jax.ShapeDtypeStruct((T, D_MODEL), F32),
                   jax.ShapeDtypeStruct((T, D_MODEL), F32),
                   jax.ShapeDtypeStruct((T, LANES), jnp.int32),
                   jax.ShapeDtypeStruct((T, LANES), F32),
                   jax.ShapeDtypeStruct((8, LANES), F32)],
        scratch_shapes=[pltpu.VMEM((8, LANES), F32)],
        compiler_params=_cparams(("arbitrary",)),
        name="outproj_router",
        interpret=interpret,
    )(x2d, o1, o2, o3, l1, l2, l3, ml, w_out_b, g_ffn.reshape(1, D_MODEL), w_r_p, b_r_p)


def _push_kernel(lb_ref, nu_ref, dest_ref, hn_ref, xs_ref, zbuf, sem, zsem, *, tm, n_blocks):
    def zero_block(j):
        return pltpu.make_async_copy(zbuf, xs_ref.at[pl.ds(j * MOE_BLOCK, MOE_BLOCK)], zsem)

    @pl.when(pl.program_id(0) == 0)
    def _():
        zbuf[...] = jnp.zeros_like(zbuf)
        for phase in ("start", "wait"):
            for e in range(N_EXPERTS):
                @pl.when(lb_ref[e] >= 0)
                def _():
                    getattr(zero_block(lb_ref[e]), phase)()

            def tail(j, carry):
                getattr(zero_block(j), phase)()
                return carry

            lax.fori_loop(nu_ref[0], n_blocks, tail, 0)

    def issue(r, carry):
        for kk in range(TOP_K):
            d = dest_ref[0, 0, r * TOP_K + kk]
            pltpu.make_async_copy(hn_ref.at[pl.ds(r, 1)], xs_ref.at[pl.ds(d, 1)], sem).start()
        return carry

    lax.fori_loop(0, tm, issue, 0, unroll=4)
    for _ in range(TOP_K):
        pltpu.make_async_copy(hn_ref, xs_ref.at[pl.ds(0, tm)], sem).wait()


def _push_rows(hn, dest, last_block, n_used, n_blocks, tm, interpret):
    T = hn.shape[0]
    grid_spec = pltpu.PrefetchScalarGridSpec(
        num_scalar_prefetch=2,
        grid=(T // tm,),
        in_specs=[pl.BlockSpec((1, 1, tm * TOP_K), lambda i, lb, nu: (i, 0, 0), memory_space=pltpu.SMEM),
                  pl.BlockSpec((tm, D_MODEL), lambda i, lb, nu: (i, 0))],
        out_specs=pl.BlockSpec(memory_space=pl.ANY),
        scratch_shapes=[pltpu.VMEM((MOE_BLOCK, D_MODEL), F32),
                        pltpu.SemaphoreType.DMA, pltpu.SemaphoreType.DMA],
    )
    return pl.pallas_call(
        functools.partial(_push_kernel, tm=tm, n_blocks=n_blocks),
        grid_spec=grid_spec,
        out_shape=jax.ShapeDtypeStruct((n_blocks * MOE_BLOCK, D_MODEL), F32),
        compiler_params=_cparams(("arbitrary",)),
        name="moe_push",
        interpret=interpret,
    )(last_block, n_used, dest.reshape(T // tm, 1, tm * TOP_K), hn)


def _expert_kernel(be_ref, nb_ref, x_ref, wgu_ref, bgu_ref, wd_ref, bd_ref, y_ref, wgu_b, wd_b):
    j = pl.program_id(0)
    active = j < nb_ref[0]
    fresh = jnp.logical_or(j == 0, be_ref[j] != be_ref[jnp.maximum(j - 1, 0)])

    @pl.when(jnp.logical_and(active, fresh))
    def _():
        wgu_b[...] = wgu_ref[0].astype(BF16)
        wd_b[...] = wd_ref[0].astype(BF16)

    @pl.when(active)
    def _():
        x = x_ref[...].astype(BF16)
        gu = jnp.dot(x, wgu_b[...], preferred_element_type=F32) + bgu_ref[0]
        g = jnp.minimum(gu[:, :D_FF], SWIGLU_LIMIT)
        u = jnp.clip(gu[:, D_FF:], -SWIGLU_LIMIT, SWIGLU_LIMIT)
        hdn = (u + 1.0) * (g * jax.nn.sigmoid(SWIGLU_ALPHA * g))
        y_ref[...] = jnp.dot(hdn.astype(BF16), wd_b[...], preferred_element_type=F32) + bd_ref[0]

    @pl.when(jnp.logical_not(active))
    def _():
        y_ref[...] = jnp.zeros_like(y_ref)


def _experts(xs, block_e, n_used, w_gate_up, b_gate_up, w_down, b_down, interpret):
    P = xs.shape[0]
    nb = P // MOE_BLOCK
    row = lambda j, be, nu: (j, 0)
    grid_spec = pltpu.PrefetchScalarGridSpec(
        num_scalar_prefetch=2,
        grid=(nb,),
        in_specs=[pl.BlockSpec((MOE_BLOCK, D_MODEL), row),
                  pl.BlockSpec((1, D_MODEL, 2 * D_FF), lambda j, be, nu: (be[j], 0, 0)),
                  pl.BlockSpec((1, 1, 2 * D_FF), lambda j, be, nu: (be[j], 0, 0)),
                  pl.BlockSpec((1, D_FF, D_MODEL), lambda j, be, nu: (be[j], 0, 0)),
                  pl.BlockSpec((1, 1, D_MODEL), lambda j, be, nu: (be[j], 0, 0))],
        out_specs=pl.BlockSpec((MOE_BLOCK, D_MODEL), row),
        scratch_shapes=[pltpu.VMEM((D_MODEL, 2 * D_FF), BF16),
                        pltpu.VMEM((D_FF, D_MODEL), BF16)],
    )
    return pl.pallas_call(
        _expert_kernel,
        grid_spec=grid_spec,
        out_shape=jax.ShapeDtypeStruct((P, D_MODEL), F32),
        compiler_params=_cparams(("arbitrary",)),
        name="moe_experts",
        interpret=interpret,
    )(block_e, n_used, xs, w_gate_up, b_gate_up.reshape(N_EXPERTS, 1, 2 * D_FF),
      w_down, b_down.reshape(N_EXPERTS, 1, D_MODEL))


def _combine_kernel(dest_ref, y_ref, h_ref, gate_ref, gfin_ref, out_ref, ybuf, sem, *, tm):
    def issue(r, carry):
        for kk in range(TOP_K):
            d = dest_ref[0, 0, r * TOP_K + kk]
            pltpu.make_async_copy(y_ref.at[pl.ds(d, 1)], ybuf.at[kk, pl.ds(r, 1)], sem).start()
        return carry

    lax.fori_loop(0, tm, issue, 0, unroll=4)
    for kk in range(TOP_K):
        pltpu.make_async_copy(y_ref.at[pl.ds(0, tm)], ybuf.at[kk], sem).wait()

    gate = gate_ref[...]
    moe = gate[:, 0:1] * ybuf[0]
    for kk in range(1, TOP_K):
        moe = moe + gate[:, kk:kk + 1] * ybuf[kk]
    h = h_ref[...] + moe
    out_ref[...] = h * lax.rsqrt(jnp.mean(h * h, axis=-1, keepdims=True) + EPS) * gfin_ref[...]


def _combine(y, dest, h, gate, g_final, tm, interpret):
    T = h.shape[0]
    return pl.pallas_call(
        functools.partial(_combine_kernel, tm=tm),
        grid=(T // tm,),
        in_specs=[pl.BlockSpec((1, 1, tm * TOP_K), lambda i: (i, 0, 0), memory_space=pltpu.SMEM),
                  pl.BlockSpec(memory_space=pl.ANY),
                  pl.BlockSpec((tm, D_MODEL), lambda i: (i, 0)),
                  pl.BlockSpec((tm, LANES), lambda i: (i, 0)),
                  pl.BlockSpec((1, D_MODEL), lambda i: (0, 0))],
        out_specs=pl.BlockSpec((tm, D_MODEL), lambda i: (i, 0)),
        out_shape=jax.ShapeDtypeStruct((T, D_MODEL), F32),
        scratch_shapes=[pltpu.VMEM((TOP_K, tm, D_MODEL), F32), pltpu.SemaphoreType.DMA],
        compiler_params=_cparams(("arbitrary",)),
        name="moe_combine",
        interpret=interpret,
    )(dest.reshape(T // tm, 1, tm * TOP_K), y, h, gate, g_final.reshape(1, D_MODEL))


def _routing_tables(idx_tile, counts, T):
    top_idx = idx_tile[:, :TOP_K]
    rank = idx_tile[:, TOP_K:2 * TOP_K]
    counts = counts.astype(jnp.int32)
    blocks_e = (counts + MOE_BLOCK - 1) // MOE_BLOCK
    cum_blocks = jnp.cumsum(blocks_e)
    start_row = (cum_blocks - blocks_e) * MOE_BLOCK
    experts = jnp.arange(N_EXPERTS, dtype=jnp.int32)
    start_of = jnp.sum(jnp.where(top_idx[..., None] == experts, start_row, 0), axis=-1)
    dest = (start_of + rank).astype(jnp.int32)
    n_blocks = -(-(T * TOP_K) // MOE_BLOCK) + N_EXPERTS
    n_used = cum_blocks[-1:].astype(jnp.int32)
    blocks = jnp.arange(n_blocks, dtype=jnp.int32)
    block_e = jnp.sum((blocks[:, None] >= cum_blocks[None, :]).astype(jnp.int32), axis=-1)
    last_e = jnp.sum((n_used - 1 >= cum_blocks).astype(jnp.int32))
    block_e = jnp.minimum(block_e, last_e).astype(jnp.int32)
    last_block = jnp.where(blocks_e > 0, cum_blocks - 1, -1).astype(jnp.int32)
    return dest, block_e, n_used, last_block, n_blocks


def _layer(x, g_mix, w_in, conv_w, conv_b, b_gates, g_head, w_out, g_ffn, w_router, b_router,
           w_gate_up, b_gate_up, w_down, b_down, g_final, interpret=False):
    B, S, D = x.shape
    T = B * S
    x2d = x.reshape(T, D)

    w_in_p = jnp.pad(w_in, ((0, 0), (0, IN_PAD - w_in.shape[1]))).astype(BF16)
    qa, ka, va, qkm, vm, om, gates = _inproj(x2d, g_mix, w_in_p, 512, interpret)

    branches = [_attn_branch(qa, ka, va, B, S, dil, interpret) for _, dil in DILATED_CFG]

    cw_p = jnp.pad(conv_w, ((0, 8 - CONV_W), (0, 0)))
    cb = conv_b.reshape(1, 2 * ML_WIDTH)
    bg_p = jnp.pad(b_gates, (0, LANES - N_GATES)).reshape(1, LANES)
    hfw = _mlstm_dir(qkm, vm, gates, cw_p, cb, bg_p, 0, B, S, interpret)
    ml = _mlstm_dir(qkm, vm, gates, cw_p, cb, bg_p, 1, B, S, interpret,
                    hfw=hfw, om=om, gh=g_head.reshape(1, ML_WIDTH))

    w_r_p = jnp.pad(w_router, ((0, 0), (0, LANES - N_EXPERTS))).astype(BF16)
    b_r_p = jnp.pad(b_router, (0, LANES - N_EXPERTS)).reshape(1, LANES)
    (o1, l1), (o2, l2), (o3, l3) = branches
    h, hn, idx_tile, gate_tile, counts = _outproj(
        x2d, o1, o2, o3, l1, l2, l3, ml.reshape(T, ML_WIDTH), w_out.astype(BF16), g_ffn,
        w_r_p, b_r_p, 512, interpret)

    dest, block_e, n_used, last_block, n_blocks = _routing_tables(idx_tile, counts[0, :N_EXPERTS], T)
    xs = _push_rows(hn, dest, last_block, n_used, n_blocks, 256, interpret)
    y = _experts(xs, block_e, n_used, w_gate_up, b_gate_up, w_down, b_down, interpret)
    out = _combine(y, dest, h, gate_tile, g_final, 256, interpret)
    return out.reshape(B, S, D)


def kernel(x_prompt, x_sample, g_mix, w_in, conv_w, conv_b, b_gates, g_head, w_out, g_ffn, w_router,
           b_router, w_gate_up, b_gate_up, w_down, b_down, g_final):
    nb = x_prompt.shape[0]
    x = jnp.concatenate([x_prompt, x_sample], axis=0)
    y = _layer(x, g_mix, w_in, conv_w, conv_b, b_gates, g_head, w_out, g_ffn, w_router, b_router,
               w_gate_up, b_gate_up, w_down, b_down, g_final)
    return (y[:nb], y[nb:])
```

```python
import functools

import jax
import jax.numpy as jnp
from jax import lax
from jax.experimental import pallas as pl
from jax.experimental.pallas import tpu as pltpu

F32 = jnp.float32
BF16 = jnp.bfloat16

D_MODEL = 1024
ATT_HEADS = 8
ATT_HEAD_DIM = 64
ATT_WIDTH = ATT_HEADS * ATT_HEAD_DIM
DILATIONS = (1, 4, 16)
ATT_HALF = 64
ML_HEADS = 4
ML_HEAD_DIM = 128
ML_WIDTH = ML_HEADS * ML_HEAD_DIM
ML_CHUNK = 128
CONV_W = 5
N_GATES = 2 * 2 * ML_HEADS
N_EXPERTS = 32
TOP_K = 4
D_FF = 1024
SWIGLU_LIMIT = 7.0
SWIGLU_ALPHA = 1.702
MOE_BLOCK = 512
EPS = 1e-6
NEG = -1e30

LANES = 128
SUBLANES = 8
IN_PAD = 7 * 512 + LANES
TOK_TILE = 512
VMEM_LIMIT = 56 * 1024 * 1024


def _cparams(sem):
    return pltpu.CompilerParams(dimension_semantics=sem, vmem_limit_bytes=VMEM_LIMIT)


def _dual_rows(n_first, width, rows=TOK_TILE):
    first = pl.BlockSpec((rows, width), lambda i: (jnp.minimum(i, n_first - 1), 0))
    second = pl.BlockSpec((rows, width), lambda i: (jnp.maximum(i - n_first, 0), 0))
    return first, second


def _inproj_kernel(xp_ref, xs_ref, pp_ref, pn_ref, sp_ref, sn_ref, g_ref, w_ref, cw_ref, cb_ref,
                   q1_ref, k1_ref, v1_ref, q4_ref, k4_ref, v4_ref, q16_ref, k16_ref, v16_ref,
                   qm_ref, kt_ref, vm_ref, om_ref, gt_ref,
                   ext, stage_q, stage_k, stage_v, *, n_first, tiles_per_seq):
    i = pl.program_id(0)
    first = i < n_first
    ti = lax.rem(i, tiles_per_seq)
    tm = TOK_TILE

    def norm(v):
        vn = v * lax.rsqrt(jnp.mean(v * v, axis=-1, keepdims=True) + EPS)
        return (vn * g_ref[...]).astype(BF16)

    xn = norm(jnp.where(first, xp_ref[...], xs_ref[...]))
    prev = jnp.where(ti > 0, jnp.where(first, pp_ref[0], sp_ref[0]), 0.0)
    nxt = jnp.where(ti < tiles_per_seq - 1, jnp.where(first, pn_ref[0], sn_ref[0]), 0.0)
    xn_halo = norm(jnp.concatenate([prev, nxt], axis=0))

    def proj(lo, hi, lhs=xn):
        return jnp.dot(lhs, w_ref[:, lo:hi], preferred_element_type=F32)

    def emit_attn(col, scale, stage, r1, r4, r16):
        p = proj(col, col + ATT_WIDTH)
        if scale is not None:
            p = p * scale
        r1[...] = p.astype(BF16)
        for j in range(ATT_WIDTH // LANES):
            stage[j] = p[:, j * LANES:(j + 1) * LANES]
        for d, ref in ((4, r4), (16, r16)):
            for ph in range(d):
                for j in range(ATT_WIDTH // LANES):
                    ref[ph, :, j * LANES:(j + 1) * LANES] = (
                        stage[j, pl.ds(ph, tm // d, stride=d), :].astype(BF16))

    emit_attn(0, ATT_HEAD_DIM ** -0.5, stage_q, q1_ref, q4_ref, q16_ref)
    emit_attn(512, None, stage_k, k1_ref, k4_ref, k16_ref)
    emit_attn(1024, None, stage_v, v1_ref, v4_ref, v16_ref)

    halo_qk = proj(1536, 2560, xn_halo)
    ext[0:SUBLANES, :] = halo_qk[0:SUBLANES]
    ext[SUBLANES:SUBLANES + tm, :] = proj(1536, 2560)
    ext[SUBLANES + tm:, :] = halo_qk[SUBLANES:]
    base = SUBLANES - CONV_W // 2
    acc = cb_ref[...] + ext[base:base + tm, :] * cw_ref[0:1, :]
    for j in range(1, CONV_W):
        acc = acc + ext[base + j:base + j + tm, :] * cw_ref[j:j + 1, :]
    act = acc * jax.nn.sigmoid(acc)
    qm_ref[...] = (act[:, :ML_WIDTH] * (ML_HEAD_DIM ** -0.5)).astype(BF16)
    for c in range(tm // ML_CHUNK):
        kt_ref[c] = act[c * ML_CHUNK:(c + 1) * ML_CHUNK, ML_WIDTH:].T.astype(BF16)

    vm_ref[...] = proj(2560, 3072).astype(BF16)
    om_ref[...] = proj(3072, 3584)
    gt_ref[...] = proj(3584, IN_PAD)


def _inproj(xp, xs, S, g_mix, w_in_p, cw_p, cb, interpret):
    tm = TOK_TILE
    n_first = xp.shape[0] // tm
    T = xp.shape[0] + xs.shape[0]
    B = T // S
    tps = S // tm
    per = tm // SUBLANES
    row = lambda w: pl.BlockSpec((tm, w), lambda i: (i, 0))
    const = lambda shape: pl.BlockSpec(shape, lambda i: (0,) * len(shape))
    xp_spec, xs_spec = _dual_rows(n_first, D_MODEL)

    def halo(n_tiles, shift, offset):
        last = n_tiles * per - 1
        return pl.BlockSpec(
            (1, SUBLANES, D_MODEL),
            lambda i: (jnp.clip((jnp.clip(i - shift, 0, n_tiles - 1)) * per + offset, 0, last), 0, 0))

    n_second = xs.shape[0] // tm
    xp3 = xp.reshape(xp.shape[0] // SUBLANES, SUBLANES, D_MODEL)
    xs3 = xs.reshape(xs.shape[0] // SUBLANES, SUBLANES, D_MODEL)
    phase = lambda d, w: pl.BlockSpec((None, d, tm // d, w), lambda i: (i // tps, 0, i % tps, 0))
    bf = lambda shape: jax.ShapeDtypeStruct(shape, BF16)
    nat = bf((T, ATT_WIDTH))
    ph4 = bf((B, 4, S // 4, ATT_WIDTH))
    ph16 = bf((B, 16, S // 16, ATT_WIDTH))
    return pl.pallas_call(
        functools.partial(_inproj_kernel, n_first=n_first, tiles_per_seq=tps),
        grid=(T // tm,),
        in_specs=[xp_spec, xs_spec,
                  halo(n_first, 0, -1), halo(n_first, 0, per),
                  halo(n_second, n_first, -1), halo(n_second, n_first, per),
                  const((1, D_MODEL)), const((D_MODEL, IN_PAD)),
                  const((SUBLANES, 2 * ML_WIDTH)), const((1, 2 * ML_WIDTH))],
        out_specs=[row(512), row(512), row(512),
                   phase(4, 512), phase(4, 512), phase(4, 512),
                   phase(16, 512), phase(16, 512), phase(16, 512),
                   row(512),
                   pl.BlockSpec((None, tm // ML_CHUNK, ML_WIDTH, ML_CHUNK), lambda i: (i // tps, i % tps, 0, 0)),
                   row(512), row(512), row(LANES)],
        out_shape=[nat, nat, nat, ph4, ph4, ph4, ph16, ph16, ph16,
                   bf((T, ML_WIDTH)),
                   bf((B, S // ML_CHUNK, ML_WIDTH, ML_CHUNK)),
                   bf((T, ML_WIDTH)),
                   jax.ShapeDtypeStruct((T, ML_WIDTH), F32),
                   jax.ShapeDtypeStruct((T, LANES), F32)],
        scratch_shapes=[pltpu.VMEM((tm + 2 * SUBLANES, 2 * ML_WIDTH), F32),
                        pltpu.VMEM((ATT_WIDTH // LANES, tm, LANES), F32),
                        pltpu.VMEM((ATT_WIDTH // LANES, tm, LANES), F32),
                        pltpu.VMEM((ATT_WIDTH // LANES, tm, LANES), F32)],
        compiler_params=_cparams(("arbitrary",)),
        name="inproj",
        interpret=interpret,
    )(xp, xs, xp3, xp3, xs3, xs3, g_mix.reshape(1, D_MODEL), w_in_p, cw_p, cb)


def _attn_kernel(q_ref, k_ref, v_ref, o_ref, lse_ref, *, dil, qb, kb, m_len):
    i = pl.program_id(2)
    start = jnp.clip(i * qb - ATT_HALF, 0, m_len - kb)
    start = pl.multiple_of(start, ATT_HALF)
    off = start - i * qb
    q = q_ref[...]
    k = k_ref[pl.ds(start, kb), :]
    v = v_ref[pl.ds(start, kb), :]
    row = lax.broadcasted_iota(jnp.int32, (qb, kb), 0)
    col = lax.broadcasted_iota(jnp.int32, (qb, kb), 1)
    absd = jnp.abs(col - row + off).astype(F32)
    valid = absd <= float(ATT_HALF)
    lane = lax.broadcasted_iota(jnp.int32, (1, LANES), 1)
    lo = lane < ATT_HEAD_DIM
    lse_tile = jnp.zeros((qb, LANES), F32)
    zero = jnp.zeros((), BF16)
    for pair in range(ATT_HEADS // 2):
        sl = slice(pair * LANES, (pair + 1) * LANES)
        qp, kp, vp = q[:, sl], k[:, sl], v[:, sl]
        outs = []
        for hh in range(2):
            h = 2 * pair + hh
            slope = 2.0 ** (-(8.0 / ATT_HEADS) * (h + 1))
            qh = jnp.where(lo if hh == 0 else jnp.logical_not(lo), qp, zero)
            s = lax.dot_general(qh, kp, (((1,), (1,)), ((), ())), preferred_element_type=F32)
            s = jnp.where(valid, s - absd * (slope * dil), NEG)
            mx = jnp.max(s, axis=-1, keepdims=True)
            p = jnp.exp(s - mx)
            l = jnp.sum(p, axis=-1, keepdims=True)
            o = jnp.dot(p.astype(BF16), vp, preferred_element_type=F32)
            outs.append(o / l)
            lse_tile = jnp.where(lane == h, mx + jnp.log(l), lse_tile)
        o_ref[:, sl] = jnp.where(lo, outs[0], outs[1]).astype(BF16)
    lse_ref[...] = lse_tile


def _attn_branch(q, k, v, dil, interpret):
    B, _, m_len, _ = q.shape
    qb = 128
    kb = min(qb + 2 * ATT_HALF, m_len)
    qspec = pl.BlockSpec((None, None, qb, ATT_WIDTH), lambda b, p, i: (b, p, i, 0))
    kvspec = pl.BlockSpec((None, None, m_len, ATT_WIDTH), lambda b, p, i: (b, p, 0, 0))
    return pl.pallas_call(
        functools.partial(_attn_kernel, dil=dil, qb=qb, kb=kb, m_len=m_len),
        grid=(B, dil, m_len // qb),
        in_specs=[qspec, kvspec, kvspec],
        out_specs=[qspec, pl.BlockSpec((None, None, qb, LANES), lambda b, p, i: (b, p, i, 0))],
        out_shape=[jax.ShapeDtypeStruct((B, dil, m_len, ATT_WIDTH), BF16),
                   jax.ShapeDtypeStruct((B, dil, m_len, LANES), F32)],
        compiler_params=_cparams(("arbitrary", "arbitrary", "arbitrary")),
        name=f"attn_d{dil}",
        interpret=interpret,
    )(q, k, v)


def _mlstm_kernel(qf_ref, ktf_ref, vf_ref, gf_ref, qb_ref, ktb_ref, vb_ref, gb_ref, bg_ref,
                  hf_ref, hb_ref, cn_st, m_st):
    L = ML_CHUNK
    hd = ML_HEAD_DIM

    @pl.when(pl.program_id(1) == 0)
    def _():
        cn_st[...] = jnp.zeros_like(cn_st)
        m_st[...] = jnp.full_like(m_st, NEG)

    t_idx = lax.broadcasted_iota(jnp.int32, (L, L), 0)
    s_idx = lax.broadcasted_iota(jnp.int32, (L, L), 1)
    ones_col = (lax.broadcasted_iota(jnp.int32, (L, LANES), 1) == 0).astype(BF16)

    for direction, (q_ref, kt_ref, v_ref, g_ref, h_ref) in enumerate(
            ((qf_ref, ktf_ref, vf_ref, gf_ref, hf_ref), (qb_ref, ktb_ref, vb_ref, gb_ref, hb_ref))):
        tri = (s_idx <= t_idx) if direction == 0 else (s_idx >= t_idx)
        last = L - 1 if direction == 0 else 0
        g = g_ref[...] + bg_ref[...]
        logf = jax.nn.log_sigmoid(g)
        b_all = jnp.dot(tri.astype(F32), logf, preferred_element_type=F32,
                        precision=lax.Precision.HIGHEST)
        g_t = g.T
        b_t = b_all.T
        outs = []
        for h in range(ML_HEADS):
            ic = direction * 8 + h
            fc = direction * 8 + 4 + h
            hs = slice(h * hd, (h + 1) * hd)
            st = direction * ML_HEADS + h
            q = q_ref[:, hs]
            kt = kt_ref[hs, :]
            v_aug = jnp.concatenate([v_ref[:, hs], ones_col], axis=-1)
            b_col = b_all[:, fc:fc + 1]
            b_row = b_t[fc:fc + 1, :]
            i_row = g_t[ic:ic + 1, :]
            b_end = b_row[:, last:last + 1]
            m_prev = m_st[st:st + 1, 0:1]
            cn_prev = cn_st[st]

            dmat = jnp.where(tri, b_col - (b_row - i_row), NEG)
            inter = b_col + m_prev
            m_t = jnp.maximum(jnp.max(dmat, axis=-1, keepdims=True), inter)
            dw = jnp.exp(dmat - m_t)
            iw = jnp.exp(inter - m_t)
            qk_s = jnp.dot(q, kt, preferred_element_type=F32) * dw
            intra = jnp.dot(qk_s.astype(BF16), v_aug, preferred_element_type=F32)
            carried = jnp.dot(q, cn_prev.astype(BF16), preferred_element_type=F32)
            num = intra[:, :hd] + iw * carried[:, :hd]
            den = intra[:, hd:hd + 1] + iw * carried[:, hd:hd + 1]
            outs.append(num / jnp.maximum(jnp.abs(den), jnp.exp(-m_t)))

            a_row = b_end - b_row + i_row
            m_loc = jnp.max(a_row, axis=-1, keepdims=True)
            kw = (kt.astype(F32) * jnp.exp(a_row - m_loc)).astype(BF16)
            cn_loc = jnp.dot(kw, v_aug, preferred_element_type=F32)
            m_new = jnp.maximum(b_end + m_prev, m_loc)
            sp = jnp.exp(b_end + m_prev - m_new)
            sl = jnp.exp(m_loc - m_new)
            cn_st[st] = sp * cn_prev + sl * cn_loc
            m_st[st:st + 1, :] = jnp.broadcast_to(m_new, (1, LANES))
        h_ref[...] = jnp.concatenate(outs, axis=-1)


def _mlstm(qm, kt, vm, gates, bg_p, B, S, interpret):
    L = ML_CHUNK
    nc = S // L
    fwd = lambda s: s
    bwd = lambda s: nc - 1 - s

    def specs(cidx):
        blk = lambda w: pl.BlockSpec((None, L, w), lambda b, s: (b, cidx(s), 0))
        ktspec = pl.BlockSpec((None, None, ML_WIDTH, L), lambda b, s: (b, cidx(s), 0, 0))
        return [blk(ML_WIDTH), ktspec, blk(ML_WIDTH), blk(LANES)]

    q3 = qm.reshape(B, S, ML_WIDTH)
    v3 = vm.reshape(B, S, ML_WIDTH)
    g3 = gates.reshape(B, S, LANES)
    hblk = lambda cidx: pl.BlockSpec((None, L, ML_WIDTH), lambda b, s: (b, cidx(s), 0))
    return pl.pallas_call(
        _mlstm_kernel,
        grid=(B, nc),
        in_specs=specs(fwd) + specs(bwd) + [pl.BlockSpec((1, LANES), lambda b, s: (0, 0))],
        out_specs=[hblk(fwd), hblk(bwd)],
        out_shape=[jax.ShapeDtypeStruct((B, S, ML_WIDTH), F32),
                   jax.ShapeDtypeStruct((B, S, ML_WIDTH), F32)],
        scratch_shapes=[pltpu.VMEM((2 * ML_HEADS, ML_HEAD_DIM, 2 * ML_HEAD_DIM), F32),
                        pltpu.VMEM((2 * ML_HEADS, LANES), F32)],
        compiler_params=_cparams(("arbitrary", "arbitrary")),
        name="mlstm",
        interpret=interpret,
    )(q3, kt, v3, g3, q3, kt, v3, g3, bg_p)


def _outproj_kernel(xp_ref, xs_ref, o1_ref, o4_ref, o16_ref, l1_ref, l4_ref, l16_ref,
                    hf_ref, hb_ref, om_ref, gh_ref, wo_ref, gf_ref, wrt_ref, br_ref,
                    h_ref, hn_ref, idx_ref, gate_ref, cnt_ref,
                    carry, nat4, nat16, lnat4, lnat16, *, n_first):
    tm = TOK_TILE
    step = pl.program_id(0)

    @pl.when(step == 0)
    def _():
        carry[...] = jnp.zeros_like(carry)

    for d, o_ref, l_ref, nat, lnat in ((4, o4_ref, l4_ref, nat4, lnat4), (16, o16_ref, l16_ref, nat16, lnat16)):
        for ph in range(d):
            rows = pl.ds(ph, tm // d, stride=d)
            lnat[rows, :] = l_ref[ph]
            for j in range(ATT_WIDTH // LANES):
                nat[j, rows, :] = o_ref[ph, :, j * LANES:(j + 1) * LANES].astype(F32)

    l1, l2, l3 = l1_ref[...], lnat4[...], lnat16[...]
    mx = jnp.maximum(jnp.maximum(l1, l2), l3)
    e1, e2, e3 = jnp.exp(l1 - mx), jnp.exp(l2 - mx), jnp.exp(l3 - mx)
    inv = 1.0 / (e1 + e2 + e3)
    er = lax.broadcasted_iota(jnp.int32, (LANES, ATT_WIDTH), 0)
    ec = lax.broadcasted_iota(jnp.int32, (LANES, ATT_WIDTH), 1)
    expand = (ec // ATT_HEAD_DIM == er).astype(BF16)

    def widen(w):
        hi = w.astype(BF16)
        lo = (w - hi.astype(F32)).astype(BF16)
        return (jnp.dot(hi, expand, preferred_element_type=F32)
                + jnp.dot(lo, expand, preferred_element_type=F32))

    slabs = lambda nat: jnp.concatenate([nat[j] for j in range(ATT_WIDTH // LANES)], axis=-1)
    attn = (widen(e1 * inv) * o1_ref[...].astype(F32)
            + widen(e2 * inv) * slabs(nat4)
            + widen(e3 * inv) * slabs(nat16))

    hsum = hf_ref[...] + hb_ref[...]
    normed = []
    for hh in range(ML_HEADS):
        hv = hsum[:, hh * ML_HEAD_DIM:(hh + 1) * ML_HEAD_DIM]
        normed.append(hv * lax.rsqrt(jnp.mean(hv * hv, axis=-1, keepdims=True) + EPS))
    ml = jax.nn.sigmoid(om_ref[...]) * (jnp.concatenate(normed, axis=-1) * gh_ref[...])

    mix = jnp.concatenate([attn.astype(BF16), ml.astype(BF16)], axis=-1)
    x = jnp.where(step < n_first, xp_ref[...], xs_ref[...])
    h = x + jnp.dot(mix, wo_ref[...], preferred_element_type=F32)
    h_ref[...] = h
    hn = h * lax.rsqrt(jnp.mean(h * h, axis=-1, keepdims=True) + EPS) * gf_ref[...]
    hn_ref[...] = hn

    logits = lax.dot_general(wrt_ref[...], hn.astype(BF16), (((1,), (1,)), ((), ())),
                             preferred_element_type=F32) + br_ref[:, 0:1]
    eid = lax.broadcasted_iota(jnp.int32, (N_EXPERTS, tm), 0)
    work = logits
    vals, idxs, hots = [], [], []
    for _ in range(TOP_K):
        top = jnp.max(work, axis=0, keepdims=True)
        idx = jnp.min(jnp.where(work == top, eid, N_EXPERTS), axis=0, keepdims=True)
        hot = eid == idx
        work = jnp.where(hot, -jnp.inf, work)
        vals.append(top)
        idxs.append(idx)
        hots.append(hot)
    exps = [jnp.exp(v - vals[0]) for v in vals]
    inv_den = 1.0 / (exps[0] + exps[1] + exps[2] + exps[3])

    cnt = jnp.where(hots[0] | hots[1] | hots[2] | hots[3], 1.0, 0.0)
    r_idx = lax.broadcasted_iota(jnp.int32, (tm, tm), 0)
    c_idx = lax.broadcasted_iota(jnp.int32, (tm, tm), 1)
    before = (r_idx < c_idx).astype(BF16)
    rank_all = jnp.dot(cnt.astype(BF16), before, preferred_element_type=F32) + carry[:, 0:1]
    carry[...] = carry[...] + jnp.sum(cnt, axis=1, keepdims=True)
    cnt_ref[...] = carry[...]

    ranks = [jnp.sum(jnp.where(hots[kk], rank_all, 0.0), axis=0, keepdims=True).astype(jnp.int32)
             for kk in range(TOP_K)]
    idx_ref[...] = jnp.concatenate(idxs + ranks, axis=0)
    gate_ref[...] = jnp.concatenate([e * inv_den for e in exps] + [jnp.zeros((TOP_K, tm), F32)], axis=0)


def _outproj(xp, xs, S, o1, o4, o16, l1, l4, l16, hf, hb, om, g_head, w_out_b, g_ffn, w_rt, b_r, interpret):
    tm = TOK_TILE
    n_first = xp.shape[0] // tm
    T = xp.shape[0] + xs.shape[0]
    tps = S // tm
    row = lambda w: pl.BlockSpec((tm, w), lambda i: (i, 0))
    col = lambda r: pl.BlockSpec((r, tm), lambda i: (0, i))
    const = lambda shape: pl.BlockSpec(shape, lambda i: (0,) * len(shape))
    phase = lambda d, w: pl.BlockSpec((None, d, tm // d, w), lambda i: (i // tps, 0, i % tps, 0))
    xp_spec, xs_spec = _dual_rows(n_first, D_MODEL)
    return pl.pallas_call(
        functools.partial(_outproj_kernel, n_first=n_first),
        grid=(T // tm,),
        in_specs=[xp_spec, xs_spec, row(512), phase(4, 512), phase(16, 512),
                  row(LANES), phase(4, LANES), phase(16, LANES),
                  row(512), row(512), row(512), const((1, ML_WIDTH)),
                  const((D_MODEL, D_MODEL)), const((1, D_MODEL)),
                  const((N_EXPERTS, D_MODEL)), const((N_EXPERTS, LANES))],
        out_specs=[row(D_MODEL), row(D_MODEL), col(2 * TOP_K), col(2 * TOP_K), const((N_EXPERTS, LANES))],
        out_shape=[jax.ShapeDtypeStruct((T, D_MODEL), F32),
                   jax.ShapeDtypeStruct((T, D_MODEL), F32),
                   jax.ShapeDtypeStruct((2 * TOP_K, T), jnp.int32),
                   jax.ShapeDtypeStruct((2 * TOP_K, T), F32),
                   jax.ShapeDtypeStruct((N_EXPERTS, LANES), F32)],
        scratch_shapes=[pltpu.VMEM((N_EXPERTS, LANES), F32),
                        pltpu.VMEM((ATT_WIDTH // LANES, tm, LANES), F32),
                        pltpu.VMEM((ATT_WIDTH // LANES, tm, LANES), F32),
                        pltpu.VMEM((tm, LANES), F32), pltpu.VMEM((tm, LANES), F32)],
        compiler_params=_cparams(("arbitrary",)),
        name="outproj_router",
        interpret=interpret,
    )(xp, xs, o1, o4, o16, l1, l4, l16, hf, hb, om, g_head.reshape(1, ML_WIDTH), w_out_b,
      g_ffn.reshape(1, D_MODEL), w_rt, b_r)


def _push_kernel(lb_ref, nu_ref, dest_ref, hn_ref, xs_ref, zbuf, sem, zsem, *, tm, n_blocks):
    def zero_block(j):
        return pltpu.make_async_copy(zbuf, xs_ref.at[pl.ds(j * MOE_BLOCK, MOE_BLOCK)], zsem)

    @pl.when(pl.program_id(0) == 0)
    def _():
        zbuf[...] = jnp.zeros_like(zbuf)
        for phase in ("start", "wait"):
            for e in range(N_EXPERTS):
                @pl.when(lb_ref[e] >= 0)
                def _():
                    getattr(zero_block(lb_ref[e]), phase)()

            def tail(j, carry):
                getattr(zero_block(j), phase)()
                return carry

            lax.fori_loop(nu_ref[0], n_blocks, tail, 0)

    def issue(r, carry):
        for kk in range(TOP_K):
            d = dest_ref[0, 0, kk * tm + r]
            pltpu.make_async_copy(hn_ref.at[pl.ds(r, 1)], xs_ref.at[pl.ds(d, 1)], sem).start()
        return carry

    lax.fori_loop(0, tm, issue, 0, unroll=4)
    for _ in range(TOP_K):
        pltpu.make_async_copy(hn_ref, xs_ref.at[pl.ds(0, tm)], sem).wait()


def _push_rows(hn, dest_tiles, last_block, n_used, n_blocks, tm, interpret):
    T = hn.shape[0]
    grid_spec = pltpu.PrefetchScalarGridSpec(
        num_scalar_prefetch=2,
        grid=(T // tm,),
        in_specs=[pl.BlockSpec((1, 1, tm * TOP_K), lambda i, lb, nu: (i, 0, 0), memory_space=pltpu.SMEM),
                  pl.BlockSpec((tm, D_MODEL), lambda i, lb, nu: (i, 0))],
        out_specs=pl.BlockSpec(memory_space=pl.ANY),
        scratch_shapes=[pltpu.VMEM((MOE_BLOCK, D_MODEL), F32),
                        pltpu.SemaphoreType.DMA, pltpu.SemaphoreType.DMA],
    )
    return pl.pallas_call(
        functools.partial(_push_kernel, tm=tm, n_blocks=n_blocks),
        grid_spec=grid_spec,
        out_shape=jax.ShapeDtypeStruct((n_blocks * MOE_BLOCK, D_MODEL), F32),
        compiler_params=_cparams(("arbitrary",)),
        name="moe_push",
        interpret=interpret,
    )(last_block, n_used, dest_tiles, hn)


def _expert_kernel(be_ref, nb_ref, x_ref, wgu_ref, bgu_ref, wd_ref, bd_ref, y_ref, wgu_b, wd_b):
    j = pl.program_id(0)
    active = j < nb_ref[0]
    fresh = jnp.logical_or(j == 0, be_ref[j] != be_ref[jnp.maximum(j - 1, 0)])

    @pl.when(jnp.logical_and(active, fresh))
    def _():
        wgu_b[...] = wgu_ref[0].astype(BF16)
        wd_b[...] = wd_ref[0].astype(BF16)

    @pl.when(active)
    def _():
        x = x_ref[...].astype(BF16)
        gu = jnp.dot(x, wgu_b[...], preferred_element_type=F32) + bgu_ref[0]
        g = jnp.minimum(gu[:, :D_FF], SWIGLU_LIMIT)
        u = jnp.clip(gu[:, D_FF:], -SWIGLU_LIMIT, SWIGLU_LIMIT)
        hdn = (u + 1.0) * (g * jax.nn.sigmoid(SWIGLU_ALPHA * g))
        y_ref[...] = jnp.dot(hdn.astype(BF16), wd_b[...], preferred_element_type=F32) + bd_ref[0]

    @pl.when(jnp.logical_not(active))
    def _():
        y_ref[...] = jnp.zeros_like(y_ref)


def _experts(xs, block_e, n_used, w_gate_up, b_gate_up, w_down, b_down, interpret):
    P = xs.shape[0]
    nb = P // MOE_BLOCK
    row = lambda j, be, nu: (j, 0)
    grid_spec = pltpu.PrefetchScalarGridSpec(
        num_scalar_prefetch=2,
        grid=(nb,),
        in_specs=[pl.BlockSpec((MOE_BLOCK, D_MODEL), row),
                  pl.BlockSpec((1, D_MODEL, 2 * D_FF), lambda j, be, nu: (be[j], 0, 0)),
                  pl.BlockSpec((1, 1, 2 * D_FF), lambda j, be, nu: (be[j], 0, 0)),
                  pl.BlockSpec((1, D_FF, D_MODEL), lambda j, be, nu: (be[j], 0, 0)),
                  pl.BlockSpec((1, 1, D_MODEL), lambda j, be, nu: (be[j], 0, 0))],
        out_specs=pl.BlockSpec((MOE_BLOCK, D_MODEL), row),
        scratch_shapes=[pltpu.VMEM((D_MODEL, 2 * D_FF), BF16),
                        pltpu.VMEM((D_FF, D_MODEL), BF16)],
    )
    return pl.pallas_call(
        _expert_kernel,
        grid_spec=grid_spec,
        out_shape=jax.ShapeDtypeStruct((P, D_MODEL), F32),
        compiler_params=_cparams(("arbitrary",)),
        name="moe_experts",
        interpret=interpret,
    )(block_e, n_used, xs, w_gate_up, b_gate_up.reshape(N_EXPERTS, 1, 2 * D_FF),
      w_down, b_down.reshape(N_EXPERTS, 1, D_MODEL))


def _combine_kernel(dest_ref, y_ref, h_ref, gate_ref, gfin_ref, outp_ref, outs_ref, ybuf, sem, *, tm, n_first):
    def issue(r, carry):
        for kk in range(TOP_K):
            d = dest_ref[0, 0, kk * tm + r]
            pltpu.make_async_copy(y_ref.at[pl.ds(d, 1)], ybuf.at[kk, pl.ds(r, 1)], sem).start()
        return carry

    lax.fori_loop(0, tm, issue, 0, unroll=4)
    for kk in range(TOP_K):
        pltpu.make_async_copy(y_ref.at[pl.ds(0, tm)], ybuf.at[kk], sem).wait()

    gate = gate_ref[...]
    moe = gate[:, 0:1] * ybuf[0]
    for kk in range(1, TOP_K):
        moe = moe + gate[:, kk:kk + 1] * ybuf[kk]
    h = h_ref[...] + moe
    res = h * lax.rsqrt(jnp.mean(h * h, axis=-1, keepdims=True) + EPS) * gfin_ref[...]

    @pl.when(pl.program_id(0) < n_first)
    def _():
        outp_ref[...] = res

    @pl.when(pl.program_id(0) >= n_first)
    def _():
        outs_ref[...] = res


def _combine(y, dest_tiles, h, gate_tm, g_final, t_first, tm, interpret):
    T = h.shape[0]
    n_first = t_first // tm
    outp_spec, outs_spec = _dual_rows(n_first, D_MODEL, tm)
    return pl.pallas_call(
        functools.partial(_combine_kernel, tm=tm, n_first=n_first),
        grid=(T // tm,),
        in_specs=[pl.BlockSpec((1, 1, tm * TOP_K), lambda i: (i, 0, 0), memory_space=pltpu.SMEM),
                  pl.BlockSpec(memory_space=pl.ANY),
                  pl.BlockSpec((tm, D_MODEL), lambda i: (i, 0)),
                  pl.BlockSpec((tm, 2 * TOP_K), lambda i: (i, 0)),
                  pl.BlockSpec((1, D_MODEL), lambda i: (0, 0))],
        out_specs=[outp_spec, outs_spec],
        out_shape=[jax.ShapeDtypeStruct((t_first, D_MODEL), F32),
                   jax.ShapeDtypeStruct((T - t_first, D_MODEL), F32)],
        scratch_shapes=[pltpu.VMEM((TOP_K, tm, D_MODEL), F32), pltpu.SemaphoreType.DMA],
        compiler_params=_cparams(("arbitrary",)),
        name="moe_combine",
        interpret=interpret,
    )(dest_tiles, y, h, gate_tm, g_final.reshape(1, D_MODEL))


def _routing_tables(idx_rows, counts, T, tm):
    top_idx = idx_rows[:TOP_K]
    rank = idx_rows[TOP_K:]
    counts = counts.astype(jnp.int32)
    blocks_e = (counts + MOE_BLOCK - 1) // MOE_BLOCK
    cum_blocks = jnp.cumsum(blocks_e)
    start_row = (cum_blocks - blocks_e) * MOE_BLOCK
    experts = jnp.arange(N_EXPERTS, dtype=jnp.int32)
    start_of = jnp.sum(jnp.where(top_idx[..., None] == experts, start_row, 0), axis=-1)
    dest = (start_of + rank).astype(jnp.int32)
    dest_tiles = dest.reshape(TOP_K, T // tm, tm).transpose(1, 0, 2).reshape(T // tm, 1, TOP_K * tm)
    n_blocks = -(-(T * TOP_K) // MOE_BLOCK) + N_EXPERTS
    n_used = cum_blocks[-1:].astype(jnp.int32)
    blocks = jnp.arange(n_blocks, dtype=jnp.int32)
    block_e = jnp.sum((blocks[:, None] >= cum_blocks[None, :]).astype(jnp.int32), axis=-1)
    last_e = jnp.sum((n_used - 1 >= cum_blocks).astype(jnp.int32))
    block_e = jnp.minimum(block_e, last_e).astype(jnp.int32)
    last_block = jnp.where(blocks_e > 0, cum_blocks - 1, -1).astype(jnp.int32)
    return dest_tiles, block_e, n_used, last_block, n_blocks


def _layer(xp, xs, S, g_mix, w_in, conv_w, conv_b, b_gates, g_head, w_out, g_ffn, w_router, b_router,
           w_gate_up, b_gate_up, w_down, b_down, g_final, interpret=False):
    T = xp.shape[0] + xs.shape[0]
    B = T // S

    w_in_p = jnp.pad(w_in, ((0, 0), (0, IN_PAD - w_in.shape[1]))).astype(BF16)
    cw_p = jnp.pad(conv_w, ((0, SUBLANES - CONV_W), (0, 0)))
    cb = conv_b.reshape(1, 2 * ML_WIDTH)
    (q1, k1, v1, q4, k4, v4, q16, k16, v16, qm, kt, vm, om, gates) = _inproj(
        xp, xs, S, g_mix, w_in_p, cw_p, cb, interpret)

    as_phase = lambda a: a.reshape(B, 1, S, ATT_WIDTH)
    o1, l1 = _attn_branch(as_phase(q1), as_phase(k1), as_phase(v1), 1, interpret)
    o4, l4 = _attn_branch(q4, k4, v4, 4, interpret)
    o16, l16 = _attn_branch(q16, k16, v16, 16, interpret)

    bg_p = jnp.pad(b_gates, (0, LANES - N_GATES)).reshape(1, LANES)
    hf, hb = _mlstm(qm, kt, vm, gates, bg_p, B, S, interpret)

    w_rt = w_router.T.astype(BF16)
    b_r = jnp.broadcast_to(b_router[:, None], (N_EXPERTS, LANES))
    h, hn, idx_rows, gate_rows, counts = _outproj(
        xp, xs, S, o1.reshape(T, ATT_WIDTH), o4, o16, l1.reshape(T, LANES), l4, l16,
        hf.reshape(T, ML_WIDTH), hb.reshape(T, ML_WIDTH), om, g_head, w_out.astype(BF16), g_ffn,
        w_rt, b_r, interpret)

    tm = 256
    dest_tiles, block_e, n_used, last_block, n_blocks = _routing_tables(idx_rows, counts[:, 0], T, tm)
    xsort = _push_rows(hn, dest_tiles, last_block, n_used, n_blocks, tm, interpret)
    y = _experts(xsort, block_e, n_used, w_gate_up, b_gate_up, w_down, b_down, interpret)
    return _combine(y, dest_tiles, h, gate_rows.T, g_final, xp.shape[0], tm, interpret)


def kernel(x_prompt, x_sample, g_mix, w_in, conv_w, conv_b, b_gates, g_head, w_out, g_ffn, w_router,
           b_router, w_gate_up, b_gate_up, w_down, b_down, g_final):
    S = x_prompt.shape[1]
    yp, ys = _layer(x_prompt.reshape(-1, D_MODEL), x_sample.reshape(-1, D_MODEL), S,
                    g_mix, w_in, conv_w, conv_b, b_gates, g_head, w_out, g_ffn, w_router, b_router,
                    w_gate_up, b_gate_up, w_down, b_down, g_final)
    return (yp.reshape(x_prompt.shape), ys.reshape(x_sample.shape))
```

```python
import functools

import jax
import jax.numpy as jnp
from jax import lax
from jax.experimental import pallas as pl
from jax.experimental.pallas import tpu as pltpu

F32 = jnp.float32
BF16 = jnp.bfloat16

D_MODEL = 1024
ATT_HEADS = 8
ATT_HEAD_DIM = 64
ATT_WIDTH = ATT_HEADS * ATT_HEAD_DIM
DILATIONS = (1, 4, 16)
ATT_HALF = 64
ML_HEADS = 4
ML_HEAD_DIM = 128
ML_WIDTH = ML_HEADS * ML_HEAD_DIM
ML_CHUNK = 128
CONV_W = 5
N_GATES = 2 * 2 * ML_HEADS
N_EXPERTS = 32
TOP_K = 4
D_FF = 1024
SWIGLU_LIMIT = 7.0
SWIGLU_ALPHA = 1.702
MOE_BLOCK = 512
EPS = 1e-6
NEG = -1e30

LANES = 128
SUBLANES = 8
IN_PAD = 7 * 512 + LANES
TOK_TILE = 512
ROW_TILE = D_MODEL // LANES
VMEM_LIMIT = 56 * 1024 * 1024


def _cparams(sem):
    return pltpu.CompilerParams(dimension_semantics=sem, vmem_limit_bytes=VMEM_LIMIT)


def _dual_rows(n_first, width, rows=TOK_TILE):
    first = pl.BlockSpec((rows, width), lambda i: (jnp.minimum(i, n_first - 1), 0))
    second = pl.BlockSpec((rows, width), lambda i: (jnp.maximum(i - n_first, 0), 0))
    return first, second


def _inproj_kernel(xp_ref, xs_ref, pp_ref, pn_ref, sp_ref, sn_ref, g_ref, w_ref, cw_ref, cb_ref,
                   q1_ref, k1_ref, v1_ref, q4_ref, k4_ref, v4_ref, q16_ref, k16_ref, v16_ref,
                   qm_ref, kt_ref, vm_ref, om_ref, gt_ref,
                   ext, stage_q, stage_k, stage_v, *, n_first, tiles_per_seq):
    i = pl.program_id(0)
    first = i < n_first
    ti = lax.rem(i, tiles_per_seq)
    tm = TOK_TILE

    def norm(v):
        vn = v * lax.rsqrt(jnp.mean(v * v, axis=-1, keepdims=True) + EPS)
        return (vn * g_ref[...]).astype(BF16)

    xn = norm(jnp.where(first, xp_ref[...], xs_ref[...]))
    prev = jnp.where(ti > 0, jnp.where(first, pp_ref[0], sp_ref[0]), 0.0)
    nxt = jnp.where(ti < tiles_per_seq - 1, jnp.where(first, pn_ref[0], sn_ref[0]), 0.0)
    xn_halo = norm(jnp.concatenate([prev, nxt], axis=0))

    def proj(lo, hi, lhs=xn):
        return jnp.dot(lhs, w_ref[:, lo:hi], preferred_element_type=F32)

    def emit_attn(col, scale, stage, r1, r4, r16):
        p = proj(col, col + ATT_WIDTH)
        if scale is not None:
            p = p * scale
        r1[...] = p.astype(BF16)
        for j in range(ATT_WIDTH // LANES):
            stage[j] = p[:, j * LANES:(j + 1) * LANES]
        for d, ref in ((4, r4), (16, r16)):
            for ph in range(d):
                for j in range(ATT_WIDTH // LANES):
                    ref[ph, :, j * LANES:(j + 1) * LANES] = (
                        stage.at[j][pl.ds(ph, tm // d, stride=d), :].astype(BF16))

    emit_attn(0, ATT_HEAD_DIM ** -0.5, stage_q, q1_ref, q4_ref, q16_ref)
    emit_attn(512, None, stage_k, k1_ref, k4_ref, k16_ref)
    emit_attn(1024, None, stage_v, v1_ref, v4_ref, v16_ref)

    halo_qk = proj(1536, 2560, xn_halo)
    ext[0:SUBLANES, :] = halo_qk[0:SUBLANES]
    ext[SUBLANES:SUBLANES + tm, :] = proj(1536, 2560)
    ext[SUBLANES + tm:, :] = halo_qk[SUBLANES:]
    base = SUBLANES - CONV_W // 2
    acc = cb_ref[...] + ext[base:base + tm, :] * cw_ref[0:1, :]
    for j in range(1, CONV_W):
        acc = acc + ext[base + j:base + j + tm, :] * cw_ref[j:j + 1, :]
    act = acc * jax.nn.sigmoid(acc)
    qm_ref[...] = (act[:, :ML_WIDTH] * (ML_HEAD_DIM ** -0.5)).astype(BF16)
    for c in range(tm // ML_CHUNK):
        kt_ref[c] = act[c * ML_CHUNK:(c + 1) * ML_CHUNK, ML_WIDTH:].T.astype(BF16)

    vm_ref[...] = proj(2560, 3072).astype(BF16)
    om_ref[...] = proj(3072, 3584)
    gt_ref[...] = proj(3584, IN_PAD)


def _inproj(xp, xs, S, g_mix, w_in_p, cw_p, cb, interpret):
    tm = TOK_TILE
    n_first = xp.shape[0] // tm
    T = xp.shape[0] + xs.shape[0]
    B = T // S
    tps = S // tm
    per = tm // SUBLANES
    row = lambda w: pl.BlockSpec((tm, w), lambda i: (i, 0))
    const = lambda shape: pl.BlockSpec(shape, lambda i: (0,) * len(shape))
    xp_spec, xs_spec = _dual_rows(n_first, D_MODEL)

    def halo(n_tiles, shift, offset):
        last = n_tiles * per - 1
        return pl.BlockSpec(
            (1, SUBLANES, D_MODEL),
            lambda i: (jnp.clip((jnp.clip(i - shift, 0, n_tiles - 1)) * per + offset, 0, last), 0, 0))

    n_second = xs.shape[0] // tm
    xp3 = xp.reshape(xp.shape[0] // SUBLANES, SUBLANES, D_MODEL)
    xs3 = xs.reshape(xs.shape[0] // SUBLANES, SUBLANES, D_MODEL)
    phase = lambda d, w: pl.BlockSpec((None, d, tm // d, w), lambda i: (i // tps, 0, i % tps, 0))
    bf = lambda shape: jax.ShapeDtypeStruct(shape, BF16)
    nat = bf((T, ATT_WIDTH))
    ph4 = bf((B, 4, S // 4, ATT_WIDTH))
    ph16 = bf((B, 16, S // 16, ATT_WIDTH))
    return pl.pallas_call(
        functools.partial(_inproj_kernel, n_first=n_first, tiles_per_seq=tps),
        grid=(T // tm,),
        in_specs=[xp_spec, xs_spec,
                  halo(n_first, 0, -1), halo(n_first, 0, per),
                  halo(n_second, n_first, -1), halo(n_second, n_first, per),
                  const((1, D_MODEL)), const((D_MODEL, IN_PAD)),
                  const((SUBLANES, 2 * ML_WIDTH)), const((1, 2 * ML_WIDTH))],
        out_specs=[row(512), row(512), row(512),
                   phase(4, 512), phase(4, 512), phase(4, 512),
                   phase(16, 512), phase(16, 512), phase(16, 512),
                   row(512),
                   pl.BlockSpec((None, tm // ML_CHUNK, ML_WIDTH, ML_CHUNK), lambda i: (i // tps, i % tps, 0, 0)),
                   row(512), row(512), row(LANES)],
        out_shape=[nat, nat, nat, ph4, ph4, ph4, ph16, ph16, ph16,
                   bf((T, ML_WIDTH)),
                   bf((B, S // ML_CHUNK, ML_WIDTH, ML_CHUNK)),
                   bf((T, ML_WIDTH)),
                   jax.ShapeDtypeStruct((T, ML_WIDTH), F32),
                   jax.ShapeDtypeStruct((T, LANES), F32)],
        scratch_shapes=[pltpu.VMEM((tm + 2 * SUBLANES, 2 * ML_WIDTH), F32),
                        pltpu.VMEM((ATT_WIDTH // LANES, tm, LANES), F32),
                        pltpu.VMEM((ATT_WIDTH // LANES, tm, LANES), F32),
                        pltpu.VMEM((ATT_WIDTH // LANES, tm, LANES), F32)],
        compiler_params=_cparams(("arbitrary",)),
        name="inproj",
        interpret=interpret,
    )(xp, xs, xp3, xp3, xs3, xs3, g_mix.reshape(1, D_MODEL), w_in_p, cw_p, cb)


def _attn_kernel(q_ref, k_ref, v_ref, o_ref, lse_ref, *, dil, qb, kb, m_len, n_sub):
    lane = lax.broadcasted_iota(jnp.int32, (1, LANES), 1)
    lo = lane < ATT_HEAD_DIM
    zero = jnp.zeros((), BF16)
    row = lax.broadcasted_iota(jnp.int32, (qb, kb), 0)
    col = lax.broadcasted_iota(jnp.int32, (qb, kb), 1)
    for sub in range(n_sub):
        blk = pl.program_id(2) * n_sub + sub
        rows = slice(sub * qb, (sub + 1) * qb)
        start = jnp.clip(blk * qb - ATT_HALF, 0, m_len - kb)
        start = pl.multiple_of(start, ATT_HALF)
        off = start - blk * qb
        q = q_ref[rows, :]
        k = k_ref[pl.ds(start, kb), :]
        v = v_ref[pl.ds(start, kb), :]
        absd = jnp.abs(col - row + off).astype(F32)
        valid = absd <= float(ATT_HALF)
        lse_tile = jnp.zeros((qb, LANES), F32)
        for pair in range(ATT_HEADS // 2):
            sl = slice(pair * LANES, (pair + 1) * LANES)
            qp, kp, vp = q[:, sl], k[:, sl], v[:, sl]
            outs = []
            for hh in range(2):
                h = 2 * pair + hh
                slope = 2.0 ** (-(8.0 / ATT_HEADS) * (h + 1))
                qh = jnp.where(lo if hh == 0 else jnp.logical_not(lo), qp, zero)
                s = lax.dot_general(qh, kp, (((1,), (1,)), ((), ())), preferred_element_type=F32)
                s = jnp.where(valid, s - absd * (slope * dil), NEG)
                mx = jnp.max(s, axis=-1, keepdims=True)
                p = jnp.exp(s - mx)
                l = jnp.sum(p, axis=-1, keepdims=True)
                o = jnp.dot(p.astype(BF16), vp, preferred_element_type=F32)
                outs.append(o / l)
                lse_tile = jnp.where(lane == h, mx + jnp.log(l), lse_tile)
            o_ref[rows, sl] = jnp.where(lo, outs[0], outs[1]).astype(BF16)
        lse_ref[rows, :] = lse_tile


def _attn_branch(q, k, v, dil, interpret):
    B, _, m_len, _ = q.shape
    qb = 128
    n_sub = 2
    kb = min(qb + 2 * ATT_HALF, m_len)
    qspec = pl.BlockSpec((None, None, n_sub * qb, ATT_WIDTH), lambda b, p, i: (b, p, i, 0))
    kvspec = pl.BlockSpec((None, None, m_len, ATT_WIDTH), lambda b, p, i: (b, p, 0, 0))
    return pl.pallas_call(
        functools.partial(_attn_kernel, dil=dil, qb=qb, kb=kb, m_len=m_len, n_sub=n_sub),
        grid=(B, dil, m_len // (n_sub * qb)),
        in_specs=[qspec, kvspec, kvspec],
        out_specs=[qspec, pl.BlockSpec((None, None, n_sub * qb, LANES), lambda b, p, i: (b, p, i, 0))],
        out_shape=[jax.ShapeDtypeStruct((B, dil, m_len, ATT_WIDTH), BF16),
                   jax.ShapeDtypeStruct((B, dil, m_len, LANES), F32)],
        compiler_params=_cparams(("arbitrary", "arbitrary", "arbitrary")),
        name=f"attn_d{dil}",
        interpret=interpret,
    )(q, k, v)


def _mlstm_kernel(qf_ref, ktf_ref, vf_ref, gf_ref, qb_ref, ktb_ref, vb_ref, gb_ref, bg_ref,
                  hf_ref, hb_ref, cn_st, m_st):
    L = ML_CHUNK
    hd = ML_HEAD_DIM

    @pl.when(pl.program_id(1) == 0)
    def _():
        cn_st[...] = jnp.zeros_like(cn_st)
        m_st[...] = jnp.full_like(m_st, NEG)

    t_idx = lax.broadcasted_iota(jnp.int32, (L, L), 0)
    s_idx = lax.broadcasted_iota(jnp.int32, (L, L), 1)
    ones_col = (lax.broadcasted_iota(jnp.int32, (L, LANES), 1) == 0).astype(BF16)

    for direction, (q_ref, kt_ref, v_ref, g_ref, h_ref) in enumerate(
            ((qf_ref, ktf_ref, vf_ref, gf_ref, hf_ref), (qb_ref, ktb_ref, vb_ref, gb_ref, hb_ref))):
        tri = (s_idx <= t_idx) if direction == 0 else (s_idx >= t_idx)
        last = L - 1 if direction == 0 else 0
        g = g_ref[...] + bg_ref[...]
        logf = jax.nn.log_sigmoid(g)
        b_all = jnp.dot(tri.astype(F32), logf, preferred_element_type=F32,
                        precision=lax.Precision.HIGHEST)
        g_t = g.T
        b_t = b_all.T
        outs = []
        for h in range(ML_HEADS):
            ic = direction * 8 + h
            fc = direction * 8 + 4 + h
            hs = slice(h * hd, (h + 1) * hd)
            st = direction * ML_HEADS + h
            q = q_ref[:, hs]
            kt = kt_ref[hs, :]
            v_aug = jnp.concatenate([v_ref[:, hs], ones_col], axis=-1)
            b_col = b_all[:, fc:fc + 1]
            b_row = b_t[fc:fc + 1, :]
            i_row = g_t[ic:ic + 1, :]
            b_end = b_row[:, last:last + 1]
            m_prev = m_st[st:st + 1, 0:1]
            cn_prev = cn_st[st]

            dmat = jnp.where(tri, b_col - (b_row - i_row), NEG)
            inter = b_col + m_prev
            m_t = jnp.maximum(jnp.max(dmat, axis=-1, keepdims=True), inter)
            dw = jnp.exp(dmat - m_t)
            iw = jnp.exp(inter - m_t)
            qk_s = jnp.dot(q, kt, preferred_element_type=F32) * dw
            intra = jnp.dot(qk_s.astype(BF16), v_aug, preferred_element_type=F32)
            carried = jnp.dot(q, cn_prev.astype(BF16), preferred_element_type=F32)
            num = intra[:, :hd] + iw * carried[:, :hd]
            den = intra[:, hd:hd + 1] + iw * carried[:, hd:hd + 1]
            outs.append(num / jnp.maximum(jnp.abs(den), jnp.exp(-m_t)))

            a_row = b_end - b_row + i_row
            m_loc = jnp.max(a_row, axis=-1, keepdims=True)
            kw = (kt.astype(F32) * jnp.exp(a_row - m_loc)).astype(BF16)
            cn_loc = jnp.dot(kw, v_aug, preferred_element_type=F32)
            m_new = jnp.maximum(b_end + m_prev, m_loc)
            sp = jnp.exp(b_end + m_prev - m_new)
            sl = jnp.exp(m_loc - m_new)
            cn_st[st] = sp * cn_prev + sl * cn_loc
            m_st[st:st + 1, :] = jnp.broadcast_to(m_new, (1, LANES))
        h_ref[...] = jnp.concatenate(outs, axis=-1)


def _mlstm(qm, kt, vm, gates, bg_p, B, S, interpret):
    L = ML_CHUNK
    nc = S // L
    fwd = lambda s: s
    bwd = lambda s: nc - 1 - s

    def specs(cidx):
        blk = lambda w: pl.BlockSpec((None, L, w), lambda b, s: (b, cidx(s), 0))
        ktspec = pl.BlockSpec((None, None, ML_WIDTH, L), lambda b, s: (b, cidx(s), 0, 0))
        return [blk(ML_WIDTH), ktspec, blk(ML_WIDTH), blk(LANES)]

    q3 = qm.reshape(B, S, ML_WIDTH)
    v3 = vm.reshape(B, S, ML_WIDTH)
    g3 = gates.reshape(B, S, LANES)
    hblk = lambda cidx: pl.BlockSpec((None, L, ML_WIDTH), lambda b, s: (b, cidx(s), 0))
    return pl.pallas_call(
        _mlstm_kernel,
        grid=(B, nc),
        in_specs=specs(fwd) + specs(bwd) + [pl.BlockSpec((1, LANES), lambda b, s: (0, 0))],
        out_specs=[hblk(fwd), hblk(bwd)],
        out_shape=[jax.ShapeDtypeStruct((B, S, ML_WIDTH), F32),
                   jax.ShapeDtypeStruct((B, S, ML_WIDTH), F32)],
        scratch_shapes=[pltpu.VMEM((2 * ML_HEADS, ML_HEAD_DIM, 2 * ML_HEAD_DIM), F32),
                        pltpu.VMEM((2 * ML_HEADS, LANES), F32)],
        compiler_params=_cparams(("arbitrary", "arbitrary")),
        name="mlstm",
        interpret=interpret,
    )(q3, kt, v3, g3, q3, kt, v3, g3, bg_p)


def _outproj_kernel(xp_ref, xs_ref, o1_ref, o4_ref, o16_ref, l1_ref, l4_ref, l16_ref,
                    hf_ref, hb_ref, om_ref, gh_ref, wo_ref, gf_ref, wrt_ref, br_ref,
                    h_ref, hn_ref, idx_ref, gate_ref, cnt_ref,
                    carry, nat4, nat16, lnat4, lnat16, *, n_first):
    tm = TOK_TILE
    step = pl.program_id(0)

    @pl.when(step == 0)
    def _():
        carry[...] = jnp.zeros_like(carry)

    for d, o_ref, l_ref, nat, lnat in ((4, o4_ref, l4_ref, nat4, lnat4), (16, o16_ref, l16_ref, nat16, lnat16)):
        for ph in range(d):
            rows = pl.ds(ph, tm // d, stride=d)
            lnat[rows, :] = l_ref[ph]
            for j in range(ATT_WIDTH // LANES):
                nat.at[j][rows, :] = o_ref[ph, :, j * LANES:(j + 1) * LANES].astype(F32)

    l1, l2, l3 = l1_ref[...], lnat4[...], lnat16[...]
    mx = jnp.maximum(jnp.maximum(l1, l2), l3)
    e1, e2, e3 = jnp.exp(l1 - mx), jnp.exp(l2 - mx), jnp.exp(l3 - mx)
    inv = 1.0 / (e1 + e2 + e3)
    er = lax.broadcasted_iota(jnp.int32, (LANES, ATT_WIDTH), 0)
    ec = lax.broadcasted_iota(jnp.int32, (LANES, ATT_WIDTH), 1)
    expand = (ec // ATT_HEAD_DIM == er).astype(BF16)

    def widen(w):
        hi = w.astype(BF16)
        lo = (w - hi.astype(F32)).astype(BF16)
        return (jnp.dot(hi, expand, preferred_element_type=F32)
                + jnp.dot(lo, expand, preferred_element_type=F32))

    slabs = lambda nat: jnp.concatenate([nat[j] for j in range(ATT_WIDTH // LANES)], axis=-1)
    attn = (widen(e1 * inv) * o1_ref[...].astype(F32)
            + widen(e2 * inv) * slabs(nat4)
            + widen(e3 * inv) * slabs(nat16))

    hsum = hf_ref[...] + hb_ref[...]
    normed = []
    for hh in range(ML_HEADS):
        hv = hsum[:, hh * ML_HEAD_DIM:(hh + 1) * ML_HEAD_DIM]
        normed.append(hv * lax.rsqrt(jnp.mean(hv * hv, axis=-1, keepdims=True) + EPS))
    ml = jax.nn.sigmoid(om_ref[...]) * (jnp.concatenate(normed, axis=-1) * gh_ref[...])

    mix = jnp.concatenate([attn.astype(BF16), ml.astype(BF16)], axis=-1)
    x = jnp.where(step < n_first, xp_ref[...], xs_ref[...])
    h = x + jnp.dot(mix, wo_ref[...], preferred_element_type=F32)
    h_ref[...] = h
    hn = h * lax.rsqrt(jnp.mean(h * h, axis=-1, keepdims=True) + EPS) * gf_ref[...]
    for j in range(D_MODEL // LANES):
        hn_ref[pl.ds(j, tm, stride=SUBLANES), :] = hn[:, j * LANES:(j + 1) * LANES]

    logits = lax.dot_general(wrt_ref[...], hn.astype(BF16), (((1,), (1,)), ((), ())),
                             preferred_element_type=F32) + br_ref[:, 0:1]
    eid = lax.broadcasted_iota(jnp.int32, (N_EXPERTS, tm), 0)
    work = logits
    vals, idxs, hots = [], [], []
    for _ in range(TOP_K):
        top = jnp.max(work, axis=0, keepdims=True)
        idx = jnp.min(jnp.where(work == top, eid, N_EXPERTS), axis=0, keepdims=True)
        hot = eid == idx
        work = jnp.where(hot, -jnp.inf, work)
        vals.append(top)
        idxs.append(idx)
        hots.append(hot)
    exps = [jnp.exp(v - vals[0]) for v in vals]
    inv_den = 1.0 / (exps[0] + exps[1] + exps[2] + exps[3])

    cnt = jnp.where(hots[0] | hots[1] | hots[2] | hots[3], 1.0, 0.0)
    r_idx = lax.broadcasted_iota(jnp.int32, (tm, tm), 0)
    c_idx = lax.broadcasted_iota(jnp.int32, (tm, tm), 1)
    before = (r_idx < c_idx).astype(BF16)
    rank_all = jnp.dot(cnt.astype(BF16), before, preferred_element_type=F32) + carry[:, 0:1]
    carry[...] = carry[...] + jnp.sum(cnt, axis=1, keepdims=True)
    cnt_ref[...] = carry[...]

    ranks = [jnp.sum(jnp.where(hots[kk], rank_all, 0.0), axis=0, keepdims=True).astype(jnp.int32)
             for kk in range(TOP_K)]
    idx_ref[...] = jnp.concatenate(idxs + ranks, axis=0)
    gate_ref[...] = jnp.concatenate([e * inv_den for e in exps] + [jnp.zeros((TOP_K, tm), F32)], axis=0)


def _outproj(xp, xs, S, o1, o4, o16, l1, l4, l16, hf, hb, om, g_head, w_out_b, g_ffn, w_rt, b_r, interpret):
    tm = TOK_TILE
    n_first = xp.shape[0] // tm
    T = xp.shape[0] + xs.shape[0]
    tps = S // tm
    row = lambda w: pl.BlockSpec((tm, w), lambda i: (i, 0))
    col = lambda r: pl.BlockSpec((r, tm), lambda i: (0, i))
    const = lambda shape: pl.BlockSpec(shape, lambda i: (0,) * len(shape))
    phase = lambda d, w: pl.BlockSpec((None, d, tm // d, w), lambda i: (i // tps, 0, i % tps, 0))
    xp_spec, xs_spec = _dual_rows(n_first, D_MODEL)
    return pl.pallas_call(
        functools.partial(_outproj_kernel, n_first=n_first),
        grid=(T // tm,),
        in_specs=[xp_spec, xs_spec, row(512), phase(4, 512), phase(16, 512),
                  row(LANES), phase(4, LANES), phase(16, LANES),
                  row(512), row(512), row(512), const((1, ML_WIDTH)),
                  const((D_MODEL, D_MODEL)), const((1, D_MODEL)),
                  const((N_EXPERTS, D_MODEL)), const((N_EXPERTS, LANES))],
        out_specs=[row(D_MODEL), pl.BlockSpec((tm * ROW_TILE, LANES), lambda i: (i, 0)),
                   col(2 * TOP_K), col(2 * TOP_K), const((N_EXPERTS, LANES))],
        out_shape=[jax.ShapeDtypeStruct((T, D_MODEL), F32),
                   jax.ShapeDtypeStruct((T * ROW_TILE, LANES), F32),
                   jax.ShapeDtypeStruct((2 * TOP_K, T), jnp.int32),
                   jax.ShapeDtypeStruct((2 * TOP_K, T), F32),
                   jax.ShapeDtypeStruct((N_EXPERTS, LANES), F32)],
        scratch_shapes=[pltpu.VMEM((N_EXPERTS, LANES), F32),
                        pltpu.VMEM((ATT_WIDTH // LANES, tm, LANES), F32),
                        pltpu.VMEM((ATT_WIDTH // LANES, tm, LANES), F32),
                        pltpu.VMEM((tm, LANES), F32), pltpu.VMEM((tm, LANES), F32)],
        compiler_params=_cparams(("arbitrary",)),
        name="outproj_router",
        interpret=interpret,
    )(xp, xs, o1, o4, o16, l1, l4, l16, hf, hb, om, g_head.reshape(1, ML_WIDTH), w_out_b,
      g_ffn.reshape(1, D_MODEL), w_rt, b_r)


def _push_kernel(lb_ref, nu_ref, dest_ref, hn_ref, xs_ref, zbuf, sem, zsem, *, tm, n_blocks):
    def zero_block(j):
        rows = MOE_BLOCK * ROW_TILE
        return pltpu.make_async_copy(zbuf, xs_ref.at[pl.ds(pl.multiple_of(j * rows, rows), rows)], zsem)

    @pl.when(pl.program_id(0) == 0)
    def _():
        zbuf[...] = jnp.zeros_like(zbuf)
        for phase in ("start", "wait"):
            for e in range(N_EXPERTS):
                @pl.when(lb_ref[e] >= 0)
                def _():
                    getattr(zero_block(lb_ref[e]), phase)()

            def tail(j, carry):
                getattr(zero_block(j), phase)()
                return carry

            lax.fori_loop(nu_ref[0], n_blocks, tail, 0)

    def issue(r, carry):
        for kk in range(TOP_K):
            d = pl.multiple_of(dest_ref[0, 0, kk * tm + r] * ROW_TILE, ROW_TILE)
            pltpu.make_async_copy(hn_ref.at[pl.ds(pl.multiple_of(r * ROW_TILE, ROW_TILE), ROW_TILE)],
                                  xs_ref.at[pl.ds(d, ROW_TILE)], sem).start()
        return carry

    lax.fori_loop(0, tm, issue, 0, unroll=4)
    for _ in range(TOP_K):
        pltpu.make_async_copy(hn_ref, xs_ref.at[pl.ds(0, tm * ROW_TILE)], sem).wait()


def _push_rows(hn, dest_tiles, last_block, n_used, n_blocks, tm, interpret):
    T = hn.shape[0] // ROW_TILE
    grid_spec = pltpu.PrefetchScalarGridSpec(
        num_scalar_prefetch=2,
        grid=(T // tm,),
        in_specs=[pl.BlockSpec((1, 1, tm * TOP_K), lambda i, lb, nu: (i, 0, 0), memory_space=pltpu.SMEM),
                  pl.BlockSpec((tm * ROW_TILE, LANES), lambda i, lb, nu: (i, 0))],
        out_specs=pl.BlockSpec(memory_space=pl.ANY),
        scratch_shapes=[pltpu.VMEM((MOE_BLOCK * ROW_TILE, LANES), F32),
                        pltpu.SemaphoreType.DMA, pltpu.SemaphoreType.DMA],
    )
    return pl.pallas_call(
        functools.partial(_push_kernel, tm=tm, n_blocks=n_blocks),
        grid_spec=grid_spec,
        out_shape=jax.ShapeDtypeStruct((n_blocks * MOE_BLOCK * ROW_TILE, LANES), F32),
        compiler_params=_cparams(("arbitrary",)),
        name="moe_push",
        interpret=interpret,
    )(last_block, n_used, dest_tiles, hn)


def _expert_kernel(be_ref, nb_ref, x_ref, wgu_ref, bgu_ref, wd_ref, bd_ref, y_ref, wgu_b, wd_b):
    j = pl.program_id(0)
    active = j < nb_ref[0]
    fresh = jnp.logical_or(j == 0, be_ref[j] != be_ref[jnp.maximum(j - 1, 0)])

    @pl.when(jnp.logical_and(active, fresh))
    def _():
        wgu_b[...] = wgu_ref[0].astype(BF16)
        wd_b[...] = wd_ref[0].astype(BF16)

    @pl.when(active)
    def _():
        x = jnp.concatenate([x_ref[pl.ds(j, MOE_BLOCK, stride=ROW_TILE), :] for j in range(ROW_TILE)],
                            axis=-1).astype(BF16)
        gu = jnp.dot(x, wgu_b[...], preferred_element_type=F32) + bgu_ref[0]
        g = jnp.minimum(gu[:, :D_FF], SWIGLU_LIMIT)
        u = jnp.clip(gu[:, D_FF:], -SWIGLU_LIMIT, SWIGLU_LIMIT)
        hdn = (u + 1.0) * (g * jax.nn.sigmoid(SWIGLU_ALPHA * g))
        y = jnp.dot(hdn.astype(BF16), wd_b[...], preferred_element_type=F32) + bd_ref[0]
        for j in range(ROW_TILE):
            y_ref[pl.ds(j, MOE_BLOCK, stride=ROW_TILE), :] = y[:, j * LANES:(j + 1) * LANES]

    @pl.when(jnp.logical_not(active))
    def _():
        y_ref[...] = jnp.zeros_like(y_ref)


def _experts(xs, block_e, n_used, w_gate_up, b_gate_up, w_down, b_down, interpret):
    nb = xs.shape[0] // (MOE_BLOCK * ROW_TILE)
    row = lambda j, be, nu: (j, 0)
    grid_spec = pltpu.PrefetchScalarGridSpec(
        num_scalar_prefetch=2,
        grid=(nb,),
        in_specs=[pl.BlockSpec((MOE_BLOCK * ROW_TILE, LANES), row),
                  pl.BlockSpec((1, D_MODEL, 2 * D_FF), lambda j, be, nu: (be[j], 0, 0)),
                  pl.BlockSpec((1, 1, 2 * D_FF), lambda j, be, nu: (be[j], 0, 0)),
                  pl.BlockSpec((1, D_FF, D_MODEL), lambda j, be, nu: (be[j], 0, 0)),
                  pl.BlockSpec((1, 1, D_MODEL), lambda j, be, nu: (be[j], 0, 0))],
        out_specs=pl.BlockSpec((MOE_BLOCK * ROW_TILE, LANES), row),
        scratch_shapes=[pltpu.VMEM((D_MODEL, 2 * D_FF), BF16),
                        pltpu.VMEM((D_FF, D_MODEL), BF16)],
    )
    return pl.pallas_call(
        _expert_kernel,
        grid_spec=grid_spec,
        out_shape=jax.ShapeDtypeStruct(xs.shape, F32),
        compiler_params=_cparams(("arbitrary",)),
        name="moe_experts",
        interpret=interpret,
    )(block_e, n_used, xs, w_gate_up, b_gate_up.reshape(N_EXPERTS, 1, 2 * D_FF),
      w_down, b_down.reshape(N_EXPERTS, 1, D_MODEL))


def _combine_kernel(dest_ref, y_ref, h_ref, gate_ref, gfin_ref, outp_ref, outs_ref, ybuf, sem, *, tm, n_first):
    def issue(r, carry):
        for kk in range(TOP_K):
            d = pl.multiple_of(dest_ref[0, 0, kk * tm + r] * ROW_TILE, ROW_TILE)
            pltpu.make_async_copy(y_ref.at[pl.ds(d, ROW_TILE)],
                                  ybuf.at[kk, pl.ds(pl.multiple_of(r * ROW_TILE, ROW_TILE), ROW_TILE)], sem).start()
        return carry

    lax.fori_loop(0, tm, issue, 0, unroll=4)
    for kk in range(TOP_K):
        pltpu.make_async_copy(y_ref.at[pl.ds(0, tm * ROW_TILE)], ybuf.at[kk], sem).wait()

    gate = gate_ref[...]
    parts = []
    for j in range(ROW_TILE):
        rows = pl.ds(j, tm, stride=ROW_TILE)
        part = gate[:, 0:1] * ybuf.at[0][rows, :]
        for kk in range(1, TOP_K):
            part = part + gate[:, kk:kk + 1] * ybuf.at[kk][rows, :]
        parts.append(part)
    h = h_ref[...] + jnp.concatenate(parts, axis=-1)
    res = h * lax.rsqrt(jnp.mean(h * h, axis=-1, keepdims=True) + EPS) * gfin_ref[...]

    @pl.when(pl.program_id(0) < n_first)
    def _():
        outp_ref[...] = res

    @pl.when(pl.program_id(0) >= n_first)
    def _():
        outs_ref[...] = res


def _combine(y, dest_tiles, h, gate_tm, g_final, t_first, tm, interpret):
    T = h.shape[0]
    n_first = t_first // tm
    outp_spec, outs_spec = _dual_rows(n_first, D_MODEL, tm)
    return pl.pallas_call(
        functools.partial(_combine_kernel, tm=tm, n_first=n_first),
        grid=(T // tm,),
        in_specs=[pl.BlockSpec((1, 1, tm * TOP_K), lambda i: (i, 0, 0), memory_space=pltpu.SMEM),
                  pl.BlockSpec(memory_space=pl.ANY),
                  pl.BlockSpec((tm, D_MODEL), lambda i: (i, 0)),
                  pl.BlockSpec((tm, 2 * TOP_K), lambda i: (i, 0)),
                  pl.BlockSpec((1, D_MODEL), lambda i: (0, 0))],
        out_specs=[outp_spec, outs_spec],
        out_shape=[jax.ShapeDtypeStruct((t_first, D_MODEL), F32),
                   jax.ShapeDtypeStruct((T - t_first, D_MODEL), F32)],
        scratch_shapes=[pltpu.VMEM((TOP_K, tm * ROW_TILE, LANES), F32), pltpu.SemaphoreType.DMA],
        compiler_params=_cparams(("arbitrary",)),
        name="moe_combine",
        interpret=interpret,
    )(dest_tiles, y, h, gate_tm, g_final.reshape(1, D_MODEL))


def _routing_tables(idx_rows, counts, T, tm):
    top_idx = idx_rows[:TOP_K]
    rank = idx_rows[TOP_K:]
    counts = counts.astype(jnp.int32)
    blocks_e = (counts + MOE_BLOCK - 1) // MOE_BLOCK
    cum_blocks = jnp.cumsum(blocks_e)
    start_row = (cum_blocks - blocks_e) * MOE_BLOCK
    experts = jnp.arange(N_EXPERTS, dtype=jnp.int32)
    start_of = jnp.sum(jnp.where(top_idx[..., None] == experts, start_row, 0), axis=-1)
    dest = (start_of + rank).astype(jnp.int32)
    dest_tiles = dest.reshape(TOP_K, T // tm, tm).transpose(1, 0, 2).reshape(T // tm, 1, TOP_K * tm)
    n_blocks = -(-(T * TOP_K) // MOE_BLOCK) + N_EXPERTS
    n_used = cum_blocks[-1:].astype(jnp.int32)
    blocks = jnp.arange(n_blocks, dtype=jnp.int32)
    block_e = jnp.sum((blocks[:, None] >= cum_blocks[None, :]).astype(jnp.int32), axis=-1)
    last_e = jnp.sum((n_used - 1 >= cum_blocks).astype(jnp.int32))
    block_e = jnp.minimum(block_e, last_e).astype(jnp.int32)
    last_block = jnp.where(blocks_e > 0, cum_blocks - 1, -1).astype(jnp.int32)
    return dest_tiles, block_e, n_used, last_block, n_blocks


def _layer(xp, xs, S, g_mix, w_in, conv_w, conv_b, b_gates, g_head, w_out, g_ffn, w_router, b_router,
           w_gate_up, b_gate_up, w_down, b_down, g_final, interpret=False):
    T = xp.shape[0] + xs.shape[0]
    B = T // S

    w_in_p = jnp.pad(w_in, ((0, 0), (0, IN_PAD - w_in.shape[1]))).astype(BF16)
    cw_p = jnp.pad(conv_w, ((0, SUBLANES - CONV_W), (0, 0)))
    cb = conv_b.reshape(1, 2 * ML_WIDTH)
    (q1, k1, v1, q4, k4, v4, q16, k16, v16, qm, kt, vm, om, gates) = _inproj(
        xp, xs, S, g_mix, w_in_p, cw_p, cb, interpret)

    as_phase = lambda a: a.reshape(B, 1, S, ATT_WIDTH)
    o1, l1 = _attn_branch(as_phase(q1), as_phase(k1), as_phase(v1), 1, interpret)
    o4, l4 = _attn_branch(q4, k4, v4, 4, interpret)
    o16, l16 = _attn_branch(q16, k16, v16, 16, interpret)

    bg_p = jnp.pad(b_gates, (0, LANES - N_GATES)).reshape(1, LANES)
    hf, hb = _mlstm(qm, kt, vm, gates, bg_p, B, S, interpret)

    w_rt = w_router.T.astype(BF16)
    b_r = jnp.broadcast_to(b_router[:, None], (N_EXPERTS, LANES))
    h, hn, idx_rows, gate_rows, counts = _outproj(
        xp, xs, S, o1.reshape(T, ATT_WIDTH), o4, o16, l1.reshape(T, LANES), l4, l16,
        hf.reshape(T, ML_WIDTH), hb.reshape(T, ML_WIDTH), om, g_head, w_out.astype(BF16), g_ffn,
        w_rt, b_r, interpret)

    tm = 256
    dest_tiles, block_e, n_used, last_block, n_blocks = _routing_tables(idx_rows, counts[:, 0], T, tm)
    xsort = _push_rows(hn, dest_tiles, last_block, n_used, n_blocks, tm, interpret)
    y = _experts(xsort, block_e, n_used, w_gate_up, b_gate_up, w_down, b_down, interpret)
    return _combine(y, dest_tiles, h, gate_rows.T, g_final, xp.shape[0], tm, interpret)


def kernel(x_prompt, x_sample, g_mix, w_in, conv_w, conv_b, b_gates, g_head, w_out, g_ffn, w_router,
           b_router, w_gate_up, b_gate_up, w_down, b_down, g_final):
    S = x_prompt.shape[1]
    yp, ys = _layer(x_prompt.reshape(-1, D_MODEL), x_sample.reshape(-1, D_MODEL), S,
                    g_mix, w_in, conv_w, conv_b, b_gates, g_head, w_out, g_ffn, w_router, b_router,
                    w_gate_up, b_gate_up, w_down, b_down, g_final)
    return (yp.reshape(x_prompt.shape), ys.reshape(x_sample.shape))
```

```python
import functools

import jax
import jax.numpy as jnp
from jax import lax
from jax.experimental import pallas as pl
from jax.experimental.pallas import tpu as pltpu

F32 = jnp.float32
BF16 = jnp.bfloat16

D_MODEL = 1024
ATT_HEADS = 8
ATT_HEAD_DIM = 64
ATT_WIDTH = ATT_HEADS * ATT_HEAD_DIM
DILATIONS = (1, 4, 16)
ATT_HALF = 64
ML_HEADS = 4
ML_HEAD_DIM = 128
ML_WIDTH = ML_HEADS * ML_HEAD_DIM
ML_CHUNK = 128
CONV_W = 5
N_GATES = 2 * 2 * ML_HEADS
N_EXPERTS = 32
TOP_K = 4
D_FF = 1024
SWIGLU_LIMIT = 7.0
SWIGLU_ALPHA = 1.702
MOE_BLOCK = 512
EPS = 1e-6
NEG = -1e30

LANES = 128
SUBLANES = 8
IN_PAD = 7 * 512 + LANES
TOK_TILE = 512
ROW_TILE = D_MODEL // LANES
VMEM_LIMIT = 56 * 1024 * 1024


def _cparams(sem):
    return pltpu.CompilerParams(dimension_semantics=sem, vmem_limit_bytes=VMEM_LIMIT)


def _dual_rows(n_first, width, rows=TOK_TILE):
    first = pl.BlockSpec((rows, width), lambda i: (jnp.minimum(i, n_first - 1), 0))
    second = pl.BlockSpec((rows, width), lambda i: (jnp.maximum(i - n_first, 0), 0))
    return first, second


def _inproj_kernel(xp_ref, xs_ref, pp_ref, pn_ref, sp_ref, sn_ref, g_ref, w_ref, cw_ref, cb_ref, bg_ref,
                   q1_ref, k1_ref, v1_ref, q4_ref, k4_ref, v4_ref, q16_ref, k16_ref, v16_ref,
                   qm_ref, kt_ref, vm_ref, om_ref, gt_ref,
                   ext, stage_q, stage_k, stage_v, *, n_first, tiles_per_seq):
    i = pl.program_id(0)
    first = i < n_first
    ti = lax.rem(i, tiles_per_seq)
    tm = TOK_TILE

    def norm(v):
        vn = v * lax.rsqrt(jnp.mean(v * v, axis=-1, keepdims=True) + EPS)
        return (vn * g_ref[...]).astype(BF16)

    xn = norm(jnp.where(first, xp_ref[...], xs_ref[...]))
    prev = jnp.where(ti > 0, jnp.where(first, pp_ref[0], sp_ref[0]), 0.0)
    nxt = jnp.where(ti < tiles_per_seq - 1, jnp.where(first, pn_ref[0], sn_ref[0]), 0.0)
    xn_halo = norm(jnp.concatenate([prev, nxt], axis=0))

    def proj(lo, hi, lhs=xn):
        return jnp.dot(lhs, w_ref[:, lo:hi], preferred_element_type=F32)

    def emit_attn(col, scale, stage, r1, r4, r16):
        p = proj(col, col + ATT_WIDTH)
        if scale is not None:
            p = p * scale
        r1[...] = p.astype(BF16)
        for j in range(ATT_WIDTH // LANES):
            stage[j] = p[:, j * LANES:(j + 1) * LANES]
        for d, ref in ((4, r4), (16, r16)):
            for ph in range(d):
                for j in range(ATT_WIDTH // LANES):
                    ref[ph, :, j * LANES:(j + 1) * LANES] = (
                        stage.at[j][pl.ds(ph, tm // d, stride=d), :].astype(BF16))

    emit_attn(0, ATT_HEAD_DIM ** -0.5, stage_q, q1_ref, q4_ref, q16_ref)
    emit_attn(512, None, stage_k, k1_ref, k4_ref, k16_ref)
    emit_attn(1024, None, stage_v, v1_ref, v4_ref, v16_ref)

    halo_qk = proj(1536, 2560, xn_halo)
    ext[0:SUBLANES, :] = halo_qk[0:SUBLANES]
    ext[SUBLANES:SUBLANES + tm, :] = proj(1536, 2560)
    ext[SUBLANES + tm:, :] = halo_qk[SUBLANES:]
    base = SUBLANES - CONV_W // 2
    acc = cb_ref[...] + ext[base:base + tm, :] * cw_ref[0:1, :]
    for j in range(1, CONV_W):
        acc = acc + ext[base + j:base + j + tm, :] * cw_ref[j:j + 1, :]
    act = acc * jax.nn.sigmoid(acc)
    qm_ref[...] = (act[:, :ML_WIDTH] * (ML_HEAD_DIM ** -0.5)).astype(BF16)
    for c in range(tm // ML_CHUNK):
        kt_ref[c] = act[c * ML_CHUNK:(c + 1) * ML_CHUNK, ML_WIDTH:].T.astype(BF16)

    vm_ref[...] = proj(2560, 3072).astype(BF16)
    om_ref[...] = proj(3072, 3584)
    gates = proj(3584, IN_PAD) + bg_ref[...]
    for c in range(tm // ML_CHUNK):
        gt_ref[c] = gates[c * ML_CHUNK:(c + 1) * ML_CHUNK, :].T[0:N_GATES, :]


def _inproj(xp, xs, S, g_mix, w_in_p, cw_p, cb, bg_p, interpret):
    tm = TOK_TILE
    n_first = xp.shape[0] // tm
    T = xp.shape[0] + xs.shape[0]
    B = T // S
    tps = S // tm
    per = tm // SUBLANES
    row = lambda w: pl.BlockSpec((tm, w), lambda i: (i, 0))
    const = lambda shape: pl.BlockSpec(shape, lambda i: (0,) * len(shape))
    xp_spec, xs_spec = _dual_rows(n_first, D_MODEL)

    def halo(n_tiles, shift, offset):
        last = n_tiles * per - 1
        return pl.BlockSpec(
            (1, SUBLANES, D_MODEL),
            lambda i: (jnp.clip((jnp.clip(i - shift, 0, n_tiles - 1)) * per + offset, 0, last), 0, 0))

    n_second = xs.shape[0] // tm
    xp3 = xp.reshape(xp.shape[0] // SUBLANES, SUBLANES, D_MODEL)
    xs3 = xs.reshape(xs.shape[0] // SUBLANES, SUBLANES, D_MODEL)
    phase = lambda d, w: pl.BlockSpec((None, d, tm // d, w), lambda i: (i // tps, 0, i % tps, 0))
    bf = lambda shape: jax.ShapeDtypeStruct(shape, BF16)
    nat = bf((T, ATT_WIDTH))
    ph4 = bf((B, 4, S // 4, ATT_WIDTH))
    ph16 = bf((B, 16, S // 16, ATT_WIDTH))
    return pl.pallas_call(
        functools.partial(_inproj_kernel, n_first=n_first, tiles_per_seq=tps),
        grid=(T // tm,),
        in_specs=[xp_spec, xs_spec,
                  halo(n_first, 0, -1), halo(n_first, 0, per),
                  halo(n_second, n_first, -1), halo(n_second, n_first, per),
                  const((1, D_MODEL)), const((D_MODEL, IN_PAD)),
                  const((SUBLANES, 2 * ML_WIDTH)), const((1, 2 * ML_WIDTH)), const((1, LANES))],
        out_specs=[row(512), row(512), row(512),
                   phase(4, 512), phase(4, 512), phase(4, 512),
                   phase(16, 512), phase(16, 512), phase(16, 512),
                   row(512),
                   pl.BlockSpec((None, tm // ML_CHUNK, ML_WIDTH, ML_CHUNK), lambda i: (i // tps, i % tps, 0, 0)),
                   row(512), row(512),
                   pl.BlockSpec((None, tm // ML_CHUNK, N_GATES, ML_CHUNK), lambda i: (i // tps, i % tps, 0, 0))],
        out_shape=[nat, nat, nat, ph4, ph4, ph4, ph16, ph16, ph16,
                   bf((T, ML_WIDTH)),
                   bf((B, S // ML_CHUNK, ML_WIDTH, ML_CHUNK)),
                   bf((T, ML_WIDTH)),
                   jax.ShapeDtypeStruct((T, ML_WIDTH), F32),
                   jax.ShapeDtypeStruct((B, S // ML_CHUNK, N_GATES, ML_CHUNK), F32)],
        scratch_shapes=[pltpu.VMEM((tm + 2 * SUBLANES, 2 * ML_WIDTH), F32),
                        pltpu.VMEM((ATT_WIDTH // LANES, tm, LANES), F32),
                        pltpu.VMEM((ATT_WIDTH // LANES, tm, LANES), F32),
                        pltpu.VMEM((ATT_WIDTH // LANES, tm, LANES), F32)],
        compiler_params=_cparams(("arbitrary",)),
        name="inproj",
        interpret=interpret,
    )(xp, xs, xp3, xp3, xs3, xs3, g_mix.reshape(1, D_MODEL), w_in_p, cw_p, cb, bg_p)


def _attn_kernel(q_ref, k_ref, v_ref, o_ref, lse_ref, *, dil, qb, kb, m_len, n_sub):
    lane = lax.broadcasted_iota(jnp.int32, (1, LANES), 1)
    lo = lane < ATT_HEAD_DIM
    zero = jnp.zeros((), BF16)
    row = lax.broadcasted_iota(jnp.int32, (qb, kb), 0)
    col = lax.broadcasted_iota(jnp.int32, (qb, kb), 1)
    for sub in range(n_sub):
        blk = pl.program_id(2) * n_sub + sub
        rows = slice(sub * qb, (sub + 1) * qb)
        start = jnp.clip(blk * qb - ATT_HALF, 0, m_len - kb)
        start = pl.multiple_of(start, ATT_HALF)
        off = start - blk * qb
        q = q_ref[rows, :]
        k = k_ref[pl.ds(start, kb), :]
        v = v_ref[pl.ds(start, kb), :]
        absd = jnp.abs(col - row + off).astype(F32)
        valid = absd <= float(ATT_HALF)
        lse_tile = jnp.zeros((qb, LANES), F32)
        for pair in range(ATT_HEADS // 2):
            sl = slice(pair * LANES, (pair + 1) * LANES)
            qp, kp, vp = q[:, sl], k[:, sl], v[:, sl]
            outs = []
            for hh in range(2):
                h = 2 * pair + hh
                slope = 2.0 ** (-(8.0 / ATT_HEADS) * (h + 1))
                qh = jnp.where(lo if hh == 0 else jnp.logical_not(lo), qp, zero)
                s = lax.dot_general(qh, kp, (((1,), (1,)), ((), ())), preferred_element_type=F32)
                s = jnp.where(valid, s - absd * (slope * dil), NEG)
                mx = jnp.max(s, axis=-1, keepdims=True)
                p = jnp.exp(s - mx)
                l = jnp.sum(p, axis=-1, keepdims=True)
                o = jnp.dot(p.astype(BF16), vp, preferred_element_type=F32)
                outs.append(o / l)
                lse_tile = jnp.where(lane == h, mx + jnp.log(l), lse_tile)
            o_ref[rows, sl] = jnp.where(lo, outs[0], outs[1]).astype(BF16)
        lse_ref[rows, :] = lse_tile


def _attn_branch(q, k, v, dil, interpret):
    B, _, m_len, _ = q.shape
    qb = 128
    n_sub = min(8, m_len // qb)
    kb = min(qb + 2 * ATT_HALF, m_len)
    qspec = pl.BlockSpec((None, None, n_sub * qb, ATT_WIDTH), lambda b, p, i: (b, p, i, 0))
    kvspec = pl.BlockSpec((None, None, m_len, ATT_WIDTH), lambda b, p, i: (b, p, 0, 0))
    return pl.pallas_call(
        functools.partial(_attn_kernel, dil=dil, qb=qb, kb=kb, m_len=m_len, n_sub=n_sub),
        grid=(B, dil, m_len // (n_sub * qb)),
        in_specs=[qspec, kvspec, kvspec],
        out_specs=[qspec, pl.BlockSpec((None, None, n_sub * qb, LANES), lambda b, p, i: (b, p, i, 0))],
        out_shape=[jax.ShapeDtypeStruct((B, dil, m_len, ATT_WIDTH), BF16),
                   jax.ShapeDtypeStruct((B, dil, m_len, LANES), F32)],
        compiler_params=_cparams(("arbitrary", "arbitrary", "arbitrary")),
        name=f"attn_d{dil}",
        interpret=interpret,
    )(q, k, v)


def _mlstm_kernel(qf_ref, ktf_ref, vf_ref, gf_ref, qb_ref, ktb_ref, vb_ref, gb_ref,
                  hf_ref, hb_ref, cn_st, m_st):
    L = ML_CHUNK
    hd = ML_HEAD_DIM

    @pl.when(pl.program_id(1) == 0)
    def _():
        cn_st[...] = jnp.zeros_like(cn_st)
        m_st[...] = jnp.full_like(m_st, NEG)

    t_idx = lax.broadcasted_iota(jnp.int32, (L, L), 0)
    s_idx = lax.broadcasted_iota(jnp.int32, (L, L), 1)
    ones_col = (lax.broadcasted_iota(jnp.int32, (L, LANES), 1) == 0).astype(BF16)
    streams = ((qf_ref, ktf_ref, vf_ref, gf_ref, hf_ref), (qb_ref, ktb_ref, vb_ref, gb_ref, hb_ref))

    prep = []
    for direction in range(2):
        tri = (s_idx <= t_idx) if direction == 0 else (s_idx >= t_idx)
        last = L - 1 if direction == 0 else 0
        rows8 = slice(direction * SUBLANES, (direction + 1) * SUBLANES)
        g8 = streams[direction][3][rows8, :]
        seen_by = (t_idx <= s_idx) if direction == 0 else (t_idx >= s_idx)
        b8 = jnp.dot(jax.nn.log_sigmoid(g8), seen_by.astype(F32), preferred_element_type=F32,
                     precision=lax.Precision.HIGHEST)
        b8 = pltpu.roll(b8, ML_HEADS, axis=0)
        nc8 = g8 - b8
        m_prev8 = m_st[rows8, :]
        b_end8 = b8[:, last:last + 1]
        a8 = b_end8 + nc8
        m_loc8 = jnp.max(a8, axis=-1, keepdims=True)
        w8 = jnp.exp(a8 - m_loc8)
        m_new8 = jnp.maximum(b_end8 + m_prev8[:, 0:1], m_loc8)
        sp8 = jnp.exp(b_end8 + m_prev8[:, 0:1] - m_new8)
        sl8 = jnp.exp(m_loc8 - m_new8)
        m_st[rows8, :] = jnp.broadcast_to(m_new8, (SUBLANES, LANES))
        b_cols = jnp.concatenate([b8, jnp.zeros((LANES - SUBLANES, L), F32)], axis=0).T
        prep.append((tri, nc8, m_prev8, w8, sp8, sl8, b_cols))

    outs = ([], [])
    for h in range(ML_HEADS):
        hs = slice(h * hd, (h + 1) * hd)
        for direction in range(2):
            q_ref, kt_ref, v_ref, _, _ = streams[direction]
            tri, nc8, m_prev8, w8, sp8, sl8, b_cols = prep[direction]
            st = direction * ML_HEADS + h
            q = q_ref[:, hs]
            kt = kt_ref[hs, :]
            v_aug = jnp.concatenate([v_ref[:, hs], ones_col], axis=-1)
            cn_prev = cn_st[st]
            m_prev = m_prev8[h:h + 1, 0:1]

            seen = jnp.where(tri, nc8[h:h + 1, :], NEG)
            mm = jnp.maximum(jnp.max(seen, axis=-1, keepdims=True), m_prev)
            dw = jnp.exp(seen - mm)
            iw = jnp.exp(m_prev - mm)
            qk_s = jnp.dot(q, kt, preferred_element_type=F32) * dw
            intra = jnp.dot(qk_s.astype(BF16), v_aug, preferred_element_type=F32)
            carried = jnp.dot(q, cn_prev.astype(BF16), preferred_element_type=F32)
            num = intra[:, :hd] + iw * carried[:, :hd]
            den = intra[:, hd:hd + 1] + iw * carried[:, hd:hd + 1]
            floor = jnp.exp(-(b_cols[:, h:h + 1] + mm))
            outs[direction].append(num / jnp.maximum(jnp.abs(den), floor))

            kw = (kt.astype(F32) * w8[h:h + 1, :]).astype(BF16)
            cn_loc = jnp.dot(kw, v_aug, preferred_element_type=F32)
            cn_st[st] = sp8[h:h + 1, :] * cn_prev + sl8[h:h + 1, :] * cn_loc
    hf_ref[...] = jnp.concatenate(outs[0], axis=-1)
    hb_ref[...] = jnp.concatenate(outs[1], axis=-1)


def _mlstm(qm, kt, vm, gates_t, B, S, interpret):
    L = ML_CHUNK
    nc = S // L
    fwd = lambda s: s
    bwd = lambda s: nc - 1 - s

    def specs(cidx):
        blk = lambda w: pl.BlockSpec((None, L, w), lambda b, s: (b, cidx(s), 0))
        chunk = lambda r: pl.BlockSpec((None, None, r, L), lambda b, s: (b, cidx(s), 0, 0))
        return [blk(ML_WIDTH), chunk(ML_WIDTH), blk(ML_WIDTH), chunk(N_GATES)]

    q3 = qm.reshape(B, S, ML_WIDTH)
    v3 = vm.reshape(B, S, ML_WIDTH)
    hblk = lambda cidx: pl.BlockSpec((None, L, ML_WIDTH), lambda b, s: (b, cidx(s), 0))
    return pl.pallas_call(
        _mlstm_kernel,
        grid=(B, nc),
        in_specs=specs(fwd) + specs(bwd),
        out_specs=[hblk(fwd), hblk(bwd)],
        out_shape=[jax.ShapeDtypeStruct((B, S, ML_WIDTH), F32),
                   jax.ShapeDtypeStruct((B, S, ML_WIDTH), F32)],
        scratch_shapes=[pltpu.VMEM((2 * ML_HEADS, ML_HEAD_DIM, 2 * ML_HEAD_DIM), F32),
                        pltpu.VMEM((2 * SUBLANES, LANES), F32)],
        compiler_params=_cparams(("arbitrary", "arbitrary")),
        name="mlstm",
        interpret=interpret,
    )(q3, kt, v3, gates_t, q3, kt, v3, gates_t)


def _outproj_kernel(xp_ref, xs_ref, o1_ref, o4_ref, o16_ref, l1_ref, l4_ref, l16_ref,
                    hf_ref, hb_ref, om_ref, gh_ref, wo_ref, gf_ref, wrt_ref, br_ref,
                    h_ref, hn_ref, idx_ref, gate_ref, cnt_ref,
                    carry, nat4, nat16, lnat4, lnat16, *, n_first):
    tm = TOK_TILE
    step = pl.program_id(0)

    @pl.when(step == 0)
    def _():
        carry[...] = jnp.zeros_like(carry)

    for d, o_ref, l_ref, nat, lnat in ((4, o4_ref, l4_ref, nat4, lnat4), (16, o16_ref, l16_ref, nat16, lnat16)):
        for ph in range(d):
            rows = pl.ds(ph, tm // d, stride=d)
            lnat[rows, :] = l_ref[ph]
            for j in range(ATT_WIDTH // LANES):
                nat.at[j][rows, :] = o_ref[ph, :, j * LANES:(j + 1) * LANES].astype(F32)

    l1, l2, l3 = l1_ref[...], lnat4[...], lnat16[...]
    mx = jnp.maximum(jnp.maximum(l1, l2), l3)
    e1, e2, e3 = jnp.exp(l1 - mx), jnp.exp(l2 - mx), jnp.exp(l3 - mx)
    inv = 1.0 / (e1 + e2 + e3)
    er = lax.broadcasted_iota(jnp.int32, (LANES, ATT_WIDTH), 0)
    ec = lax.broadcasted_iota(jnp.int32, (LANES, ATT_WIDTH), 1)
    expand = (ec // ATT_HEAD_DIM == er).astype(BF16)

    def widen(w):
        hi = w.astype(BF16)
        lo = (w - hi.astype(F32)).astype(BF16)
        return (jnp.dot(hi, expand, preferred_element_type=F32)
                + jnp.dot(lo, expand, preferred_element_type=F32))

    slabs = lambda nat: jnp.concatenate([nat[j] for j in range(ATT_WIDTH // LANES)], axis=-1)
    attn = (widen(e1 * inv) * o1_ref[...].astype(F32)
            + widen(e2 * inv) * slabs(nat4)
            + widen(e3 * inv) * slabs(nat16))

    hsum = hf_ref[...] + hb_ref[...]
    normed = []
    for hh in range(ML_HEADS):
        hv = hsum[:, hh * ML_HEAD_DIM:(hh + 1) * ML_HEAD_DIM]
        normed.append(hv * lax.rsqrt(jnp.mean(hv * hv, axis=-1, keepdims=True) + EPS))
    ml = jax.nn.sigmoid(om_ref[...]) * (jnp.concatenate(normed, axis=-1) * gh_ref[...])

    mix = jnp.concatenate([attn.astype(BF16), ml.astype(BF16)], axis=-1)
    x = jnp.where(step < n_first, xp_ref[...], xs_ref[...])
    h = x + jnp.dot(mix, wo_ref[...], preferred_element_type=F32)
    h_ref[...] = h
    hn = h * lax.rsqrt(jnp.mean(h * h, axis=-1, keepdims=True) + EPS) * gf_ref[...]
    for j in range(D_MODEL // LANES):
        hn_ref[pl.ds(j, tm, stride=SUBLANES), :] = hn[:, j * LANES:(j + 1) * LANES]

    logits = lax.dot_general(wrt_ref[...], hn.astype(BF16), (((1,), (1,)), ((), ())),
                             preferred_element_type=F32) + br_ref[:, 0:1]
    eid = lax.broadcasted_iota(jnp.int32, (N_EXPERTS, tm), 0)
    work = logits
    vals, idxs, hots = [], [], []
    for _ in range(TOP_K):
        top = jnp.max(work, axis=0, keepdims=True)
        idx = jnp.min(jnp.where(work == top, eid, N_EXPERTS), axis=0, keepdims=True)
        hot = eid == idx
        work = jnp.where(hot, -jnp.inf, work)
        vals.append(top)
        idxs.append(idx)
        hots.append(hot)
    exps = [jnp.exp(v - vals[0]) for v in vals]
    inv_den = 1.0 / (exps[0] + exps[1] + exps[2] + exps[3])

    cnt = jnp.where(hots[0] | hots[1] | hots[2] | hots[3], 1.0, 0.0)
    r_idx = lax.broadcasted_iota(jnp.int32, (tm, tm), 0)
    c_idx = lax.broadcasted_iota(jnp.int32, (tm, tm), 1)
    before = (r_idx < c_idx).astype(BF16)
    rank_all = jnp.dot(cnt.astype(BF16), before, preferred_element_type=F32) + carry[:, 0:1]
    carry[...] = carry[...] + jnp.sum(cnt, axis=1, keepdims=True)
    cnt_ref[...] = carry[...]

    ranks = [jnp.sum(jnp.where(hots[kk], rank_all, 0.0), axis=0, keepdims=True).astype(jnp.int32)
             for kk in range(TOP_K)]
    idx_ref[...] = jnp.concatenate(idxs + ranks, axis=0)
    gate_ref[...] = jnp.concatenate([e * inv_den for e in exps] + [jnp.zeros((TOP_K, tm), F32)], axis=0)


def _outproj(xp, xs, S, o1, o4, o16, l1, l4, l16, hf, hb, om, g_head, w_out_b, g_ffn, w_rt, b_r, interpret):
    tm = TOK_TILE
    n_first = xp.shape[0] // tm
    T = xp.shape[0] + xs.shape[0]
    tps = S // tm
    row = lambda w: pl.BlockSpec((tm, w), lambda i: (i, 0))
    col = lambda r: pl.BlockSpec((r, tm), lambda i: (0, i))
    const = lambda shape: pl.BlockSpec(shape, lambda i: (0,) * len(shape))
    phase = lambda d, w: pl.BlockSpec((None, d, tm // d, w), lambda i: (i // tps, 0, i % tps, 0))
    xp_spec, xs_spec = _dual_rows(n_first, D_MODEL)
    return pl.pallas_call(
        functools.partial(_outproj_kernel, n_first=n_first),
        grid=(T // tm,),
        in_specs=[xp_spec, xs_spec, row(512), phase(4, 512), phase(16, 512),
                  row(LANES), phase(4, LANES), phase(16, LANES),
                  row(512), row(512), row(512), const((1, ML_WIDTH)),
                  const((D_MODEL, D_MODEL)), const((1, D_MODEL)),
                  const((N_EXPERTS, D_MODEL)), const((N_EXPERTS, LANES))],
        out_specs=[row(D_MODEL), pl.BlockSpec((tm * ROW_TILE, LANES), lambda i: (i, 0)),
                   col(2 * TOP_K), col(2 * TOP_K), const((N_EXPERTS, LANES))],
        out_shape=[jax.ShapeDtypeStruct((T, D_MODEL), F32),
                   jax.ShapeDtypeStruct((T * ROW_TILE, LANES), F32),
                   jax.ShapeDtypeStruct((2 * TOP_K, T), jnp.int32),
                   jax.ShapeDtypeStruct((2 * TOP_K, T), F32),
                   jax.ShapeDtypeStruct((N_EXPERTS, LANES), F32)],
        scratch_shapes=[pltpu.VMEM((N_EXPERTS, LANES), F32),
                        pltpu.VMEM((ATT_WIDTH // LANES, tm, LANES), F32),
                        pltpu.VMEM((ATT_WIDTH // LANES, tm, LANES), F32),
                        pltpu.VMEM((tm, LANES), F32), pltpu.VMEM((tm, LANES), F32)],
        compiler_params=_cparams(("arbitrary",)),
        name="outproj_router",
        interpret=interpret,
    )(xp, xs, o1, o4, o16, l1, l4, l16, hf, hb, om, g_head.reshape(1, ML_WIDTH), w_out_b,
      g_ffn.reshape(1, D_MODEL), w_rt, b_r)


def _push_kernel(lb_ref, nu_ref, dest_ref, hn_ref, xs_ref, zbuf, sem, zsem, *, tm, n_blocks):
    def zero_block(j):
        rows = MOE_BLOCK * ROW_TILE
        return pltpu.make_async_copy(zbuf, xs_ref.at[pl.ds(pl.multiple_of(j * rows, rows), rows)], zsem)

    @pl.when(pl.program_id(0) == 0)
    def _():
        zbuf[...] = jnp.zeros_like(zbuf)
        for phase in ("start", "wait"):
            for e in range(N_EXPERTS):
                @pl.when(lb_ref[e] >= 0)
                def _():
                    getattr(zero_block(lb_ref[e]), phase)()

            def tail(j, carry):
                getattr(zero_block(j), phase)()
                return carry

            lax.fori_loop(nu_ref[0], n_blocks, tail, 0)

    def issue(r, carry):
        for kk in range(TOP_K):
            d = pl.multiple_of(dest_ref[0, 0, kk * tm + r] * ROW_TILE, ROW_TILE)
            pltpu.make_async_copy(hn_ref.at[pl.ds(pl.multiple_of(r * ROW_TILE, ROW_TILE), ROW_TILE)],
                                  xs_ref.at[pl.ds(d, ROW_TILE)], sem).start()
        return carry

    lax.fori_loop(0, tm, issue, 0, unroll=4)
    for _ in range(TOP_K):
        pltpu.make_async_copy(hn_ref, xs_ref.at[pl.ds(0, tm * ROW_TILE)], sem).wait()


def _push_rows(hn, dest_tiles, last_block, n_used, n_blocks, tm, interpret):
    T = hn.shape[0] // ROW_TILE
    grid_spec = pltpu.PrefetchScalarGridSpec(
        num_scalar_prefetch=2,
        grid=(T // tm,),
        in_specs=[pl.BlockSpec((1, 1, tm * TOP_K), lambda i, lb, nu: (i, 0, 0), memory_space=pltpu.SMEM),
                  pl.BlockSpec((tm * ROW_TILE, LANES), lambda i, lb, nu: (i, 0))],
        out_specs=pl.BlockSpec(memory_space=pl.ANY),
        scratch_shapes=[pltpu.VMEM((MOE_BLOCK * ROW_TILE, LANES), F32),
                        pltpu.SemaphoreType.DMA, pltpu.SemaphoreType.DMA],
    )
    return pl.pallas_call(
        functools.partial(_push_kernel, tm=tm, n_blocks=n_blocks),
        grid_spec=grid_spec,
        out_shape=jax.ShapeDtypeStruct((n_blocks * MOE_BLOCK * ROW_TILE, LANES), F32),
        compiler_params=_cparams(("arbitrary",)),
        name="moe_push",
        interpret=interpret,
    )(last_block, n_used, dest_tiles, hn)


def _expert_kernel(be_ref, nb_ref, x_ref, wgu_ref, bgu_ref, wd_ref, bd_ref, y_ref, wgu_b, wd_b):
    j = pl.program_id(0)
    active = j < nb_ref[0]
    fresh = jnp.logical_or(j == 0, be_ref[j] != be_ref[jnp.maximum(j - 1, 0)])

    @pl.when(jnp.logical_and(active, fresh))
    def _():
        wgu_b[...] = wgu_ref[0].astype(BF16)
        wd_b[...] = wd_ref[0].astype(BF16)

    @pl.when(active)
    def _():
        x = jnp.concatenate([x_ref[pl.ds(j, MOE_BLOCK, stride=ROW_TILE), :] for j in range(ROW_TILE)],
                            axis=-1).astype(BF16)
        gu = jnp.dot(x, wgu_b[...], preferred_element_type=F32) + bgu_ref[0]
        g = jnp.minimum(gu[:, :D_FF], SWIGLU_LIMIT)
        u = jnp.clip(gu[:, D_FF:], -SWIGLU_LIMIT, SWIGLU_LIMIT)
        hdn = (u + 1.0) * (g * jax.nn.sigmoid(SWIGLU_ALPHA * g))
        y = jnp.dot(hdn.astype(BF16), wd_b[...], preferred_element_type=F32) + bd_ref[0]
        for j in range(ROW_TILE):
            y_ref[pl.ds(j, MOE_BLOCK, stride=ROW_TILE), :] = y[:, j * LANES:(j + 1) * LANES]

    @pl.when(jnp.logical_not(active))
    def _():
        y_ref[...] = jnp.zeros_like(y_ref)


def _experts(xs, block_e, n_used, w_gate_up, b_gate_up, w_down, b_down, interpret):
    nb = xs.shape[0] // (MOE_BLOCK * ROW_TILE)
    row = lambda j, be, nu: (j, 0)
    grid_spec = pltpu.PrefetchScalarGridSpec(
        num_scalar_prefetch=2,
        grid=(nb,),
        in_specs=[pl.BlockSpec((MOE_BLOCK * ROW_TILE, LANES), row),
                  pl.BlockSpec((1, D_MODEL, 2 * D_FF), lambda j, be, nu: (be[j], 0, 0)),
                  pl.BlockSpec((1, 1, 2 * D_FF), lambda j, be, nu: (be[j], 0, 0)),
                  pl.BlockSpec((1, D_FF, D_MODEL), lambda j, be, nu: (be[j], 0, 0)),
                  pl.BlockSpec((1, 1, D_MODEL), lambda j, be, nu: (be[j], 0, 0))],
        out_specs=pl.BlockSpec((MOE_BLOCK * ROW_TILE, LANES), row),
        scratch_shapes=[pltpu.VMEM((D_MODEL, 2 * D_FF), BF16),
                        pltpu.VMEM((D_FF, D_MODEL), BF16)],
    )
    return pl.pallas_call(
        _expert_kernel,
        grid_spec=grid_spec,
        out_shape=jax.ShapeDtypeStruct(xs.shape, F32),
        compiler_params=_cparams(("arbitrary",)),
        name="moe_experts",
        interpret=interpret,
    )(block_e, n_used, xs, w_gate_up, b_gate_up.reshape(N_EXPERTS, 1, 2 * D_FF),
      w_down, b_down.reshape(N_EXPERTS, 1, D_MODEL))


def _combine_kernel(dest_ref, y_ref, h_ref, gate_ref, gfin_ref, outp_ref, outs_ref, ybuf, sem, *, tm, n_first):
    def issue(r, carry):
        for kk in range(TOP_K):
            d = pl.multiple_of(dest_ref[0, 0, kk * tm + r] * ROW_TILE, ROW_TILE)
            pltpu.make_async_copy(y_ref.at[pl.ds(d, ROW_TILE)],
                                  ybuf.at[kk, pl.ds(pl.multiple_of(r * ROW_TILE, ROW_TILE), ROW_TILE)], sem).start()
        return carry

    lax.fori_loop(0, tm, issue, 0, unroll=4)
    for kk in range(TOP_K):
        pltpu.make_async_copy(y_ref.at[pl.ds(0, tm * ROW_TILE)], ybuf.at[kk], sem).wait()

    gate = gate_ref[...]
    parts = []
    for j in range(ROW_TILE):
        rows = pl.ds(j, tm, stride=ROW_TILE)
        part = gate[:, 0:1] * ybuf.at[0][rows, :]
        for kk in range(1, TOP_K):
            part = part + gate[:, kk:kk + 1] * ybuf.at[kk][rows, :]
        parts.append(part)
    h = h_ref[...] + jnp.concatenate(parts, axis=-1)
    res = h * lax.rsqrt(jnp.mean(h * h, axis=-1, keepdims=True) + EPS) * gfin_ref[...]

    @pl.when(pl.program_id(0) < n_first)
    def _():
        outp_ref[...] = res

    @pl.when(pl.program_id(0) >= n_first)
    def _():
        outs_ref[...] = res


def _combine(y, dest_tiles, h, gate_tm, g_final, t_first, tm, interpret):
    T = h.shape[0]
    n_first = t_first // tm
    outp_spec, outs_spec = _dual_rows(n_first, D_MODEL, tm)
    return pl.pallas_call(
        functools.partial(_combine_kernel, tm=tm, n_first=n_first),
        grid=(T // tm,),
        in_specs=[pl.BlockSpec((1, 1, tm * TOP_K), lambda i: (i, 0, 0), memory_space=pltpu.SMEM),
                  pl.BlockSpec(memory_space=pl.ANY),
                  pl.BlockSpec((tm, D_MODEL), lambda i: (i, 0)),
                  pl.BlockSpec((tm, 2 * TOP_K), lambda i: (i, 0)),
                  pl.BlockSpec((1, D_MODEL), lambda i: (0, 0))],
        out_specs=[outp_spec, outs_spec],
        out_shape=[jax.ShapeDtypeStruct((t_first, D_MODEL), F32),
                   jax.ShapeDtypeStruct((T - t_first, D_MODEL), F32)],
        scratch_shapes=[pltpu.VMEM((TOP_K, tm * ROW_TILE, LANES), F32), pltpu.SemaphoreType.DMA],
        compiler_params=_cparams(("arbitrary",)),
        name="moe_combine",
        interpret=interpret,
    )(dest_tiles, y, h, gate_tm, g_final.reshape(1, D_MODEL))


def _routing_tables(idx_rows, counts, T, tm):
    top_idx = idx_rows[:TOP_K]
    rank = idx_rows[TOP_K:]
    counts = counts.astype(jnp.int32)
    blocks_e = (counts + MOE_BLOCK - 1) // MOE_BLOCK
    cum_blocks = jnp.cumsum(blocks_e)
    start_row = (cum_blocks - blocks_e) * MOE_BLOCK
    experts = jnp.arange(N_EXPERTS, dtype=jnp.int32)
    start_of = jnp.sum(jnp.where(top_idx[..., None] == experts, start_row, 0), axis=-1)
    dest = (start_of + rank).astype(jnp.int32)
    dest_tiles = dest.reshape(TOP_K, T // tm, tm).transpose(1, 0, 2).reshape(T // tm, 1, TOP_K * tm)
    n_blocks = -(-(T * TOP_K) // MOE_BLOCK) + N_EXPERTS
    n_used = cum_blocks[-1:].astype(jnp.int32)
    blocks = jnp.arange(n_blocks, dtype=jnp.int32)
    block_e = jnp.sum((blocks[:, None] >= cum_blocks[None, :]).astype(jnp.int32), axis=-1)
    last_e = jnp.sum((n_used - 1 >= cum_blocks).astype(jnp.int32))
    block_e = jnp.minimum(block_e, last_e).astype(jnp.int32)
    last_block = jnp.where(blocks_e > 0, cum_blocks - 1, -1).astype(jnp.int32)
    return dest_tiles, block_e, n_used, last_block, n_blocks


def _layer(xp, xs, S, g_mix, w_in, conv_w, conv_b, b_gates, g_head, w_out, g_ffn, w_router, b_router,
           w_gate_up, b_gate_up, w_down, b_down, g_final, interpret=False):
    T = xp.shape[0] + xs.shape[0]
    B = T // S

    w_in_p = jnp.pad(w_in, ((0, 0), (0, IN_PAD - w_in.shape[1]))).astype(BF16)
    cw_p = jnp.pad(conv_w, ((0, SUBLANES - CONV_W), (0, 0)))
    cb = conv_b.reshape(1, 2 * ML_WIDTH)
    bg_p = jnp.pad(b_gates, (0, LANES - N_GATES)).reshape(1, LANES)
    (q1, k1, v1, q4, k4, v4, q16, k16, v16, qm, kt, vm, om, gates_t) = _inproj(
        xp, xs, S, g_mix, w_in_p, cw_p, cb, bg_p, interpret)

    as_phase = lambda a: a.reshape(B, 1, S, ATT_WIDTH)
    o1, l1 = _attn_branch(as_phase(q1), as_phase(k1), as_phase(v1), 1, interpret)
    o4, l4 = _attn_branch(q4, k4, v4, 4, interpret)
    o16, l16 = _attn_branch(q16, k16, v16, 16, interpret)

    hf, hb = _mlstm(qm, kt, vm, gates_t, B, S, interpret)

    w_rt = w_router.T.astype(BF16)
    b_r = jnp.broadcast_to(b_router[:, None], (N_EXPERTS, LANES))
    h, hn, idx_rows, gate_rows, counts = _outproj(
        xp, xs, S, o1.reshape(T, ATT_WIDTH), o4, o16, l1.reshape(T, LANES), l4, l16,
        hf.reshape(T, ML_WIDTH), hb.reshape(T, ML_WIDTH), om, g_head, w_out.astype(BF16), g_ffn,
        w_rt, b_r, interpret)

    tm = 256
    dest_tiles, block_e, n_used, last_block, n_blocks = _routing_tables(idx_rows, counts[:, 0], T, tm)
    xsort = _push_rows(hn, dest_tiles, last_block, n_used, n_blocks, tm, interpret)
    y = _experts(xsort, block_e, n_used, w_gate_up, b_gate_up, w_down, b_down, interpret)
    return _combine(y, dest_tiles, h, gate_rows.T, g_final, xp.shape[0], tm, interpret)


def kernel(x_prompt, x_sample, g_mix, w_in, conv_w, conv_b, b_gates, g_head, w_out, g_ffn, w_router,
           b_router, w_gate_up, b_gate_up, w_down, b_down, g_final):
    S = x_prompt.shape[1]
    yp, ys = _layer(x_prompt.reshape(-1, D_MODEL), x_sample.reshape(-1, D_MODEL), S,
                    g_mix, w_in, conv_w, conv_b, b_gates, g_head, w_out, g_ffn, w_router, b_router,
                    w_gate_up, b_gate_up, w_down, b_down, g_final)
    return (yp.reshape(x_prompt.shape), ys.reshape(x_sample.shape))
```

```python
import functools

import jax
import jax.numpy as jnp
from jax import lax
from jax.experimental import pallas as pl
from jax.experimental.pallas import tpu as pltpu

F32 = jnp.float32
BF16 = jnp.bfloat16

D_MODEL = 1024
ATT_HEADS = 8
ATT_HEAD_DIM = 64
ATT_WIDTH = ATT_HEADS * ATT_HEAD_DIM
DILATIONS = (1, 4, 16)
ATT_HALF = 64
ML_HEADS = 4
ML_HEAD_DIM = 128
ML_WIDTH = ML_HEADS * ML_HEAD_DIM
ML_CHUNK = 128
CONV_W = 5
N_GATES = 2 * 2 * ML_HEADS
N_EXPERTS = 32
TOP_K = 4
D_FF = 1024
SWIGLU_LIMIT = 7.0
SWIGLU_ALPHA = 1.702
MOE_BLOCK = 512
EPS = 1e-6
NEG = -1e30

LANES = 128
SUBLANES = 8
IN_PAD = 7 * 512 + LANES
TOK_TILE = 512
ROW_TILE = D_MODEL // LANES
VMEM_LIMIT = 56 * 1024 * 1024


def _cparams(sem):
    return pltpu.CompilerParams(dimension_semantics=sem, vmem_limit_bytes=VMEM_LIMIT)


def _dual_rows(n_first, width, rows=TOK_TILE):
    first = pl.BlockSpec((rows, width), lambda i: (jnp.minimum(i, n_first - 1), 0))
    second = pl.BlockSpec((rows, width), lambda i: (jnp.maximum(i - n_first, 0), 0))
    return first, second


def _inproj_kernel(xp_ref, xs_ref, pp_ref, pn_ref, sp_ref, sn_ref, g_ref, w_ref, cw_ref, cb_ref, bg_ref,
                   q1_ref, k1_ref, v1_ref, q4_ref, k4_ref, v4_ref, q16_ref, k16_ref, v16_ref,
                   qm_ref, kt_ref, vm_ref, om_ref, gt_ref,
                   ext, stage_q, stage_k, stage_v, *, n_first, tiles_per_seq):
    i = pl.program_id(0)
    first = i < n_first
    ti = lax.rem(i, tiles_per_seq)
    tm = TOK_TILE

    def norm(v):
        vn = v * lax.rsqrt(jnp.mean(v * v, axis=-1, keepdims=True) + EPS)
        return (vn * g_ref[...]).astype(BF16)

    xn = norm(jnp.where(first, xp_ref[...], xs_ref[...]))
    prev = jnp.where(ti > 0, jnp.where(first, pp_ref[0], sp_ref[0]), 0.0)
    nxt = jnp.where(ti < tiles_per_seq - 1, jnp.where(first, pn_ref[0], sn_ref[0]), 0.0)
    xn_halo = norm(jnp.concatenate([prev, nxt], axis=0))

    def proj(lo, hi, lhs=xn):
        return jnp.dot(lhs, w_ref[:, lo:hi], preferred_element_type=F32)

    def emit_attn(col, scale, stage, r1, r4, r16):
        p = proj(col, col + ATT_WIDTH)
        if scale is not None:
            p = p * scale
        r1[...] = p.astype(BF16)
        for j in range(ATT_WIDTH // LANES):
            stage[j] = p[:, j * LANES:(j + 1) * LANES]
        for d, ref in ((4, r4), (16, r16)):
            for ph in range(d):
                for j in range(ATT_WIDTH // LANES):
                    ref[ph, :, j * LANES:(j + 1) * LANES] = (
                        stage.at[j][pl.ds(ph, tm // d, stride=d), :].astype(BF16))

    emit_attn(0, ATT_HEAD_DIM ** -0.5, stage_q, q1_ref, q4_ref, q16_ref)
    emit_attn(512, None, stage_k, k1_ref, k4_ref, k16_ref)
    emit_attn(1024, None, stage_v, v1_ref, v4_ref, v16_ref)

    halo_qk = proj(1536, 2560, xn_halo)
    ext[0:SUBLANES, :] = halo_qk[0:SUBLANES]
    ext[SUBLANES:SUBLANES + tm, :] = proj(1536, 2560)
    ext[SUBLANES + tm:, :] = halo_qk[SUBLANES:]
    base = SUBLANES - CONV_W // 2
    acc = cb_ref[...] + ext[base:base + tm, :] * cw_ref[0:1, :]
    for j in range(1, CONV_W):
        acc = acc + ext[base + j:base + j + tm, :] * cw_ref[j:j + 1, :]
    act = acc * jax.nn.sigmoid(acc)
    qm_ref[...] = (act[:, :ML_WIDTH] * (ML_HEAD_DIM ** -0.5)).astype(BF16)
    for c in range(tm // ML_CHUNK):
        kt_ref[c] = act[c * ML_CHUNK:(c + 1) * ML_CHUNK, ML_WIDTH:].T.astype(BF16)

    vm_ref[...] = proj(2560, 3072).astype(BF16)
    om_ref[...] = proj(3072, 3584)
    gates = proj(3584, IN_PAD) + bg_ref[...]
    for c in range(tm // ML_CHUNK):
        gt_ref[c] = gates[c * ML_CHUNK:(c + 1) * ML_CHUNK, :].T[0:N_GATES, :]


def _inproj(xp, xs, S, g_mix, w_in_p, cw_p, cb, bg_p, interpret):
    tm = TOK_TILE
    n_first = xp.shape[0] // tm
    T = xp.shape[0] + xs.shape[0]
    B = T // S
    tps = S // tm
    per = tm // SUBLANES
    row = lambda w: pl.BlockSpec((tm, w), lambda i: (i, 0))
    const = lambda shape: pl.BlockSpec(shape, lambda i: (0,) * len(shape))
    xp_spec, xs_spec = _dual_rows(n_first, D_MODEL)

    def halo(n_tiles, shift, offset):
        last = n_tiles * per - 1
        return pl.BlockSpec(
            (1, SUBLANES, D_MODEL),
            lambda i: (jnp.clip((jnp.clip(i - shift, 0, n_tiles - 1)) * per + offset, 0, last), 0, 0))

    n_second = xs.shape[0] // tm
    xp3 = xp.reshape(xp.shape[0] // SUBLANES, SUBLANES, D_MODEL)
    xs3 = xs.reshape(xs.shape[0] // SUBLANES, SUBLANES, D_MODEL)
    phase = lambda d, w: pl.BlockSpec((None, d, tm // d, w), lambda i: (i // tps, 0, i % tps, 0))
    bf = lambda shape: jax.ShapeDtypeStruct(shape, BF16)
    nat = bf((T, ATT_WIDTH))
    ph4 = bf((B, 4, S // 4, ATT_WIDTH))
    ph16 = bf((B, 16, S // 16, ATT_WIDTH))
    return pl.pallas_call(
        functools.partial(_inproj_kernel, n_first=n_first, tiles_per_seq=tps),
        grid=(T // tm,),
        in_specs=[xp_spec, xs_spec,
                  halo(n_first, 0, -1), halo(n_first, 0, per),
                  halo(n_second, n_first, -1), halo(n_second, n_first, per),
                  const((1, D_MODEL)), const((D_MODEL, IN_PAD)),
                  const((SUBLANES, 2 * ML_WIDTH)), const((1, 2 * ML_WIDTH)), const((1, LANES))],
        out_specs=[row(512), row(512), row(512),
                   phase(4, 512), phase(4, 512), phase(4, 512),
                   phase(16, 512), phase(16, 512), phase(16, 512),
                   row(512),
                   pl.BlockSpec((None, tm // ML_CHUNK, ML_WIDTH, ML_CHUNK), lambda i: (i // tps, i % tps, 0, 0)),
                   row(512), row(512),
                   pl.BlockSpec((None, tm // ML_CHUNK, N_GATES, ML_CHUNK), lambda i: (i // tps, i % tps, 0, 0))],
        out_shape=[nat, nat, nat, ph4, ph4, ph4, ph16, ph16, ph16,
                   bf((T, ML_WIDTH)),
                   bf((B, S // ML_CHUNK, ML_WIDTH, ML_CHUNK)),
                   bf((T, ML_WIDTH)),
                   jax.ShapeDtypeStruct((T, ML_WIDTH), F32),
                   jax.ShapeDtypeStruct((B, S // ML_CHUNK, N_GATES, ML_CHUNK), F32)],
        scratch_shapes=[pltpu.VMEM((tm + 2 * SUBLANES, 2 * ML_WIDTH), F32),
                        pltpu.VMEM((ATT_WIDTH // LANES, tm, LANES), F32),
                        pltpu.VMEM((ATT_WIDTH // LANES, tm, LANES), F32),
                        pltpu.VMEM((ATT_WIDTH // LANES, tm, LANES), F32)],
        compiler_params=_cparams(("arbitrary",)),
        name="inproj",
        interpret=interpret,
    )(xp, xs, xp3, xp3, xs3, xs3, g_mix.reshape(1, D_MODEL), w_in_p, cw_p, cb, bg_p)


def _attn_kernel(q_ref, k_ref, v_ref, o_ref, lse_ref, *, dil, qb, kb, m_len, n_sub):
    lane = lax.broadcasted_iota(jnp.int32, (1, LANES), 1)
    lo = lane < ATT_HEAD_DIM
    zero = jnp.zeros((), BF16)
    row = lax.broadcasted_iota(jnp.int32, (qb, kb), 0)
    col = lax.broadcasted_iota(jnp.int32, (qb, kb), 1)
    for sub in range(n_sub):
        blk = pl.program_id(2) * n_sub + sub
        rows = slice(sub * qb, (sub + 1) * qb)
        start = jnp.clip(blk * qb - ATT_HALF, 0, m_len - kb)
        start = pl.multiple_of(start, ATT_HALF)
        off = start - blk * qb
        q = q_ref[rows, :]
        k = k_ref[pl.ds(start, kb), :]
        v = v_ref[pl.ds(start, kb), :]
        absd = jnp.abs(col - row + off).astype(F32)
        valid = absd <= float(ATT_HALF)
        lse_tile = jnp.zeros((qb, LANES), F32)
        for pair in range(ATT_HEADS // 2):
            sl = slice(pair * LANES, (pair + 1) * LANES)
            qp, kp, vp = q[:, sl], k[:, sl], v[:, sl]
            outs = []
            for hh in range(2):
                h = 2 * pair + hh
                slope = 2.0 ** (-(8.0 / ATT_HEADS) * (h + 1))
                qh = jnp.where(lo if hh == 0 else jnp.logical_not(lo), qp, zero)
                s = lax.dot_general(qh, kp, (((1,), (1,)), ((), ())), preferred_element_type=F32)
                s = jnp.where(valid, s - absd * (slope * dil), NEG)
                mx = jnp.max(s, axis=-1, keepdims=True)
                p = jnp.exp(s - mx)
                l = jnp.sum(p, axis=-1, keepdims=True)
                o = jnp.dot(p.astype(BF16), vp, preferred_element_type=F32)
                outs.append(o / l)
                lse_tile = jnp.where(lane == h, mx + jnp.log(l), lse_tile)
            o_ref[rows, sl] = jnp.where(lo, outs[0], outs[1]).astype(BF16)
        lse_ref[rows, :] = lse_tile


def _attn_branch(q, k, v, dil, interpret):
    B, _, m_len, _ = q.shape
    qb = 128
    n_sub = min(8, m_len // qb)
    kb = min(qb + 2 * ATT_HALF, m_len)
    qspec = pl.BlockSpec((None, None, n_sub * qb, ATT_WIDTH), lambda b, p, i: (b, p, i, 0))
    kvspec = pl.BlockSpec((None, None, m_len, ATT_WIDTH), lambda b, p, i: (b, p, 0, 0))
    return pl.pallas_call(
        functools.partial(_attn_kernel, dil=dil, qb=qb, kb=kb, m_len=m_len, n_sub=n_sub),
        grid=(B, dil, m_len // (n_sub * qb)),
        in_specs=[qspec, kvspec, kvspec],
        out_specs=[qspec, pl.BlockSpec((None, None, n_sub * qb, LANES), lambda b, p, i: (b, p, i, 0))],
        out_shape=[jax.ShapeDtypeStruct((B, dil, m_len, ATT_WIDTH), BF16),
                   jax.ShapeDtypeStruct((B, dil, m_len, LANES), F32)],
        compiler_params=_cparams(("arbitrary", "arbitrary", "arbitrary")),
        name=f"attn_d{dil}",
        interpret=interpret,
    )(q, k, v)


def _mlstm_kernel(qf_ref, ktf_ref, vf_ref, gf_ref, qb_ref, ktb_ref, vb_ref, gb_ref,
                  hf_ref, hb_ref, cn_st, m_st):
    L = ML_CHUNK
    hd = ML_HEAD_DIM

    @pl.when(pl.program_id(1) == 0)
    def _():
        cn_st[...] = jnp.zeros_like(cn_st)
        m_st[...] = jnp.full_like(m_st, NEG)

    t_idx = lax.broadcasted_iota(jnp.int32, (L, L), 0)
    s_idx = lax.broadcasted_iota(jnp.int32, (L, L), 1)
    ones_col = (lax.broadcasted_iota(jnp.int32, (L, LANES), 1) == 0).astype(BF16)
    streams = ((qf_ref, ktf_ref, vf_ref, gf_ref, hf_ref), (qb_ref, ktb_ref, vb_ref, gb_ref, hb_ref))

    prep = []
    for direction in range(2):
        tri = (s_idx <= t_idx) if direction == 0 else (s_idx >= t_idx)
        last = L - 1 if direction == 0 else 0
        rows8 = slice(direction * SUBLANES, (direction + 1) * SUBLANES)
        g8 = streams[direction][3][rows8, :]
        seen_by = (t_idx <= s_idx) if direction == 0 else (t_idx >= s_idx)
        b8 = jnp.dot(jax.nn.log_sigmoid(g8), seen_by.astype(F32), preferred_element_type=F32,
                     precision=lax.Precision.HIGHEST)
        b8 = pltpu.roll(b8, ML_HEADS, axis=0)
        nc8 = g8 - b8
        m_prev8 = m_st[rows8, :]
        b_end8 = b8[:, last:last + 1]
        a8 = b_end8 + nc8
        m_loc8 = jnp.max(a8, axis=-1, keepdims=True)
        w8 = jnp.exp(a8 - m_loc8)
        m_new8 = jnp.maximum(b_end8 + m_prev8[:, 0:1], m_loc8)
        sp8 = jnp.exp(b_end8 + m_prev8[:, 0:1] - m_new8)
        sl8 = jnp.exp(m_loc8 - m_new8)
        m_st[rows8, :] = jnp.broadcast_to(m_new8, (SUBLANES, LANES))
        b_cols = jnp.concatenate([b8, jnp.zeros((LANES - SUBLANES, L), F32)], axis=0).T
        prep.append((tri, nc8, m_prev8, w8, sp8, sl8, b_cols))

    outs = ([], [])
    for h in range(ML_HEADS):
        hs = slice(h * hd, (h + 1) * hd)
        for direction in range(2):
            q_ref, kt_ref, v_ref, _, _ = streams[direction]
            tri, nc8, m_prev8, w8, sp8, sl8, b_cols = prep[direction]
            st = direction * ML_HEADS + h
            q = q_ref[:, hs]
            kt = kt_ref[hs, :]
            v_aug = jnp.concatenate([v_ref[:, hs], ones_col], axis=-1)
            cn_prev = cn_st[st]
            m_prev = m_prev8[h:h + 1, 0:1]

            seen = jnp.where(tri, nc8[h:h + 1, :], NEG)
            mm = jnp.maximum(jnp.max(seen, axis=-1, keepdims=True), m_prev)
            dw = jnp.exp(seen - mm)
            iw = jnp.exp(m_prev - mm)
            qk_s = jnp.dot(q, kt, preferred_element_type=F32) * dw
            intra = jnp.dot(qk_s.astype(BF16), v_aug, preferred_element_type=F32)
            carried = jnp.dot(q, cn_prev.astype(BF16), preferred_element_type=F32)
            num = intra[:, :hd] + iw * carried[:, :hd]
            den = intra[:, hd:hd + 1] + iw * carried[:, hd:hd + 1]
            floor = jnp.exp(-(b_cols[:, h:h + 1] + mm))
            outs[direction].append(num / jnp.maximum(jnp.abs(den), floor))

            kw = (kt.astype(F32) * w8[h:h + 1, :]).astype(BF16)
            cn_loc = jnp.dot(kw, v_aug, preferred_element_type=F32)
            cn_st[st] = sp8[h:h + 1, :] * cn_prev + sl8[h:h + 1, :] * cn_loc
    hf_ref[...] = jnp.concatenate(outs[0], axis=-1)
    hb_ref[...] = jnp.concatenate(outs[1], axis=-1)


def _mlstm(qm, kt, vm, gates_t, B, S, interpret):
    L = ML_CHUNK
    nc = S // L
    fwd = lambda s: s
    bwd = lambda s: nc - 1 - s

    def specs(cidx):
        blk = lambda w: pl.BlockSpec((None, L, w), lambda b, s: (b, cidx(s), 0))
        chunk = lambda r: pl.BlockSpec((None, None, r, L), lambda b, s: (b, cidx(s), 0, 0))
        return [blk(ML_WIDTH), chunk(ML_WIDTH), blk(ML_WIDTH), chunk(N_GATES)]

    q3 = qm.reshape(B, S, ML_WIDTH)
    v3 = vm.reshape(B, S, ML_WIDTH)
    hblk = lambda cidx: pl.BlockSpec((None, L, ML_WIDTH), lambda b, s: (b, cidx(s), 0))
    return pl.pallas_call(
        _mlstm_kernel,
        grid=(B, nc),
        in_specs=specs(fwd) + specs(bwd),
        out_specs=[hblk(fwd), hblk(bwd)],
        out_shape=[jax.ShapeDtypeStruct((B, S, ML_WIDTH), F32),
                   jax.ShapeDtypeStruct((B, S, ML_WIDTH), F32)],
        scratch_shapes=[pltpu.VMEM((2 * ML_HEADS, ML_HEAD_DIM, 2 * ML_HEAD_DIM), F32),
                        pltpu.VMEM((2 * SUBLANES, LANES), F32)],
        compiler_params=_cparams(("arbitrary", "arbitrary")),
        name="mlstm",
        interpret=interpret,
    )(q3, kt, v3, gates_t, q3, kt, v3, gates_t)


def _outproj_kernel(xp_ref, xs_ref, o1_ref, o4_ref, o16_ref, l1_ref, l4_ref, l16_ref,
                    hf_ref, hb_ref, om_ref, gh_ref, wo_ref, gf_ref, wrt_ref, br_ref,
                    h_ref, hn_ref, idx_ref, gate_ref, cnt_ref,
                    carry, nat4, nat16, lnat4, lnat16, *, n_first):
    tm = TOK_TILE
    step = pl.program_id(0)

    @pl.when(step == 0)
    def _():
        carry[...] = jnp.zeros_like(carry)

    for d, o_ref, l_ref, nat, lnat in ((4, o4_ref, l4_ref, nat4, lnat4), (16, o16_ref, l16_ref, nat16, lnat16)):
        for ph in range(d):
            rows = pl.ds(ph, tm // d, stride=d)
            lnat[rows, :] = l_ref[ph]
            for j in range(ATT_WIDTH // LANES):
                nat.at[j][rows, :] = o_ref[ph, :, j * LANES:(j + 1) * LANES].astype(F32)

    l1, l2, l3 = l1_ref[...], lnat4[...], lnat16[...]
    mx = jnp.maximum(jnp.maximum(l1, l2), l3)
    e1, e2, e3 = jnp.exp(l1 - mx), jnp.exp(l2 - mx), jnp.exp(l3 - mx)
    inv = 1.0 / (e1 + e2 + e3)
    er = lax.broadcasted_iota(jnp.int32, (LANES, ATT_WIDTH), 0)
    ec = lax.broadcasted_iota(jnp.int32, (LANES, ATT_WIDTH), 1)
    expand = (ec // ATT_HEAD_DIM == er).astype(BF16)

    def widen(w):
        hi = w.astype(BF16)
        lo = (w - hi.astype(F32)).astype(BF16)
        return (jnp.dot(hi, expand, preferred_element_type=F32)
                + jnp.dot(lo, expand, preferred_element_type=F32))

    slabs = lambda nat: jnp.concatenate([nat[j] for j in range(ATT_WIDTH // LANES)], axis=-1)
    attn = (widen(e1 * inv) * o1_ref[...].astype(F32)
            + widen(e2 * inv) * slabs(nat4)
            + widen(e3 * inv) * slabs(nat16))

    hsum = hf_ref[...] + hb_ref[...]
    normed = []
    for hh in range(ML_HEADS):
        hv = hsum[:, hh * ML_HEAD_DIM:(hh + 1) * ML_HEAD_DIM]
        normed.append(hv * lax.rsqrt(jnp.mean(hv * hv, axis=-1, keepdims=True) + EPS))
    ml = jax.nn.sigmoid(om_ref[...]) * (jnp.concatenate(normed, axis=-1) * gh_ref[...])

    mix = jnp.concatenate([attn.astype(BF16), ml.astype(BF16)], axis=-1)
    x = jnp.where(step < n_first, xp_ref[...], xs_ref[...])
    h = x + jnp.dot(mix, wo_ref[...], preferred_element_type=F32)
    h_ref[...] = h
    hn = h * lax.rsqrt(jnp.mean(h * h, axis=-1, keepdims=True) + EPS) * gf_ref[...]
    for j in range(D_MODEL // LANES):
        hn_ref[pl.ds(j, tm, stride=SUBLANES), :] = hn[:, j * LANES:(j + 1) * LANES]

    logits = lax.dot_general(wrt_ref[...], hn.astype(BF16), (((1,), (1,)), ((), ())),
                             preferred_element_type=F32) + br_ref[:, 0:1]
    eid = lax.broadcasted_iota(jnp.int32, (N_EXPERTS, tm), 0)
    work = logits
    vals, idxs, hots = [], [], []
    for _ in range(TOP_K):
        top = jnp.max(work, axis=0, keepdims=True)
        idx = jnp.min(jnp.where(work == top, eid, N_EXPERTS), axis=0, keepdims=True)
        hot = eid == idx
        work = jnp.where(hot, -jnp.inf, work)
        vals.append(top)
        idxs.append(idx)
        hots.append(hot)
    exps = [jnp.exp(v - vals[0]) for v in vals]
    inv_den = 1.0 / (exps[0] + exps[1] + exps[2] + exps[3])

    cnt = jnp.where(hots[0] | hots[1] | hots[2] | hots[3], 1.0, 0.0)
    r_idx = lax.broadcasted_iota(jnp.int32, (tm, tm), 0)
    c_idx = lax.broadcasted_iota(jnp.int32, (tm, tm), 1)
    before = (r_idx < c_idx).astype(BF16)
    rank_all = jnp.dot(cnt.astype(BF16), before, preferred_element_type=F32) + carry[:, 0:1]
    carry[...] = carry[...] + jnp.sum(cnt, axis=1, keepdims=True)
    cnt_ref[...] = carry[...]

    ranks = [jnp.sum(jnp.where(hots[kk], rank_all, 0.0), axis=0, keepdims=True).astype(jnp.int32)
             for kk in range(TOP_K)]
    idx_ref[...] = jnp.concatenate(idxs + ranks, axis=0)
    gate_ref[...] = jnp.concatenate([e * inv_den for e in exps] + [jnp.zeros((TOP_K, tm), F32)], axis=0)


def _outproj(xp, xs, S, o1, o4, o16, l1, l4, l16, hf, hb, om, g_head, w_out_b, g_ffn, w_rt, b_r, interpret):
    tm = TOK_TILE
    n_first = xp.shape[0] // tm
    T = xp.shape[0] + xs.shape[0]
    tps = S // tm
    row = lambda w: pl.BlockSpec((tm, w), lambda i: (i, 0))
    col = lambda r: pl.BlockSpec((r, tm), lambda i: (0, i))
    const = lambda shape: pl.BlockSpec(shape, lambda i: (0,) * len(shape))
    phase = lambda d, w: pl.BlockSpec((None, d, tm // d, w), lambda i: (i // tps, 0, i % tps, 0))
    xp_spec, xs_spec = _dual_rows(n_first, D_MODEL)
    return pl.pallas_call(
        functools.partial(_outproj_kernel, n_first=n_first),
        grid=(T // tm,),
        in_specs=[xp_spec, xs_spec, row(512), phase(4, 512), phase(16, 512),
                  row(LANES), phase(4, LANES), phase(16, LANES),
                  row(512), row(512), row(512), const((1, ML_WIDTH)),
                  const((D_MODEL, D_MODEL)), const((1, D_MODEL)),
                  const((N_EXPERTS, D_MODEL)), const((N_EXPERTS, LANES))],
        out_specs=[row(D_MODEL), pl.BlockSpec((tm * ROW_TILE, LANES), lambda i: (i, 0)),
                   col(2 * TOP_K), col(2 * TOP_K), const((N_EXPERTS, LANES))],
        out_shape=[jax.ShapeDtypeStruct((T, D_MODEL), F32),
                   jax.ShapeDtypeStruct((T * ROW_TILE, LANES), F32),
                   jax.ShapeDtypeStruct((2 * TOP_K, T), jnp.int32),
                   jax.ShapeDtypeStruct((2 * TOP_K, T), F32),
                   jax.ShapeDtypeStruct((N_EXPERTS, LANES), F32)],
        scratch_shapes=[pltpu.VMEM((N_EXPERTS, LANES), F32),
                        pltpu.VMEM((ATT_WIDTH // LANES, tm, LANES), F32),
                        pltpu.VMEM((ATT_WIDTH // LANES, tm, LANES), F32),
                        pltpu.VMEM((tm, LANES), F32), pltpu.VMEM((tm, LANES), F32)],
        compiler_params=_cparams(("arbitrary",)),
        name="outproj_router",
        interpret=interpret,
    )(xp, xs, o1, o4, o16, l1, l4, l16, hf, hb, om, g_head.reshape(1, ML_WIDTH), w_out_b,
      g_ffn.reshape(1, D_MODEL), w_rt, b_r)


def _expert_kernel(be_ref, src_cur, src_next, dst_prev, dst_cur, hn_ref, wgu_ref, bgu_ref, wd_ref, bd_ref,
                   y_ref, wgu_b, wd_b, xbuf, ybuf, gsem, ssem, *, n_blocks, dump_row):
    j = pl.program_id(0)
    slot = lax.rem(j, 2)
    other = 1 - slot
    rows = MOE_BLOCK * ROW_TILE

    def gather_wait(s):
        pltpu.make_async_copy(hn_ref.at[pl.ds(0, rows)], xbuf.at[s], gsem.at[s]).wait()

    def scatter_wait(s):
        pltpu.make_async_copy(ybuf.at[s], y_ref.at[pl.ds(0, rows)], ssem.at[s]).wait()

    def gather_row(tab, r, s):
        tok = pl.multiple_of(tab[0, 0, r] * ROW_TILE, ROW_TILE)
        pltpu.make_async_copy(hn_ref.at[pl.ds(tok, ROW_TILE)],
                              xbuf.at[s, pl.ds(r * ROW_TILE, ROW_TILE)], gsem.at[s]).start()

    def scatter_row(tab, r, s):
        d = pl.multiple_of(tab[0, 0, r] * ROW_TILE, ROW_TILE)
        pltpu.make_async_copy(ybuf.at[s, pl.ds(r * ROW_TILE, ROW_TILE)],
                              y_ref.at[pl.ds(d, ROW_TILE)], ssem.at[s]).start()

    @pl.when(j == 0)
    def _():
        ybuf[...] = jnp.zeros_like(ybuf)
        dump = pltpu.make_async_copy(ybuf.at[0], y_ref.at[pl.ds(dump_row * ROW_TILE, rows)], ssem.at[0])
        dump.start()
        dump.wait()

        def first(r, carry):
            gather_row(src_cur, r, 0)
            return carry

        lax.fori_loop(0, MOE_BLOCK, first, 0, unroll=8)

    @pl.when(jnp.logical_or(j == 0, be_ref[j] != be_ref[jnp.maximum(j - 1, 0)]))
    def _():
        wgu_b[...] = wgu_ref[0].astype(BF16)
        wd_b[...] = wd_ref[0].astype(BF16)

    gather_wait(slot)

    @pl.when(j >= 1)
    def _():
        scatter_wait(slot)

    for r in range(MOE_BLOCK):
        gather_row(src_next, r, other)
    for r in range(MOE_BLOCK):
        scatter_row(dst_prev, r, other)

    xv = xbuf.at[slot]
    x = jnp.concatenate([xv[pl.ds(c, MOE_BLOCK, stride=ROW_TILE), :] for c in range(ROW_TILE)],
                        axis=-1).astype(BF16)
    gu = jnp.dot(x, wgu_b[...], preferred_element_type=F32) + bgu_ref[0]
    g = jnp.minimum(gu[:, :D_FF], SWIGLU_LIMIT)
    u = jnp.clip(gu[:, D_FF:], -SWIGLU_LIMIT, SWIGLU_LIMIT)
    hdn = (u + 1.0) * (g * jax.nn.sigmoid(SWIGLU_ALPHA * g))
    y = jnp.dot(hdn.astype(BF16), wd_b[...], preferred_element_type=F32) + bd_ref[0]
    yv = ybuf.at[slot]
    for c in range(ROW_TILE):
        yv[pl.ds(c, MOE_BLOCK, stride=ROW_TILE), :] = y[:, c * LANES:(c + 1) * LANES]

    @pl.when(j == n_blocks - 1)
    def _():
        def last(r, carry):
            scatter_row(dst_cur, r, slot)
            return carry

        scatter_wait(other)
        lax.fori_loop(0, MOE_BLOCK, last, 0, unroll=8)
        scatter_wait(slot)
        gather_wait(other)


def _experts(hn_rt, src_tab, dst_tab, block_e, w_gate_up, b_gate_up, w_down, b_down, n_rows, interpret):
    nb = src_tab.shape[0]
    tab = lambda f: pl.BlockSpec((1, 1, MOE_BLOCK), lambda j, be: (f(j), 0, 0), memory_space=pltpu.SMEM)
    grid_spec = pltpu.PrefetchScalarGridSpec(
        num_scalar_prefetch=1,
        grid=(nb,),
        in_specs=[tab(lambda j: j), tab(lambda j: jnp.minimum(j + 1, nb - 1)),
                  tab(lambda j: j), tab(lambda j: j + 1),
                  pl.BlockSpec(memory_space=pl.ANY),
                  pl.BlockSpec((1, D_MODEL, 2 * D_FF), lambda j, be: (be[j], 0, 0)),
                  pl.BlockSpec((1, 1, 2 * D_FF), lambda j, be: (be[j], 0, 0)),
                  pl.BlockSpec((1, D_FF, D_MODEL), lambda j, be: (be[j], 0, 0)),
                  pl.BlockSpec((1, 1, D_MODEL), lambda j, be: (be[j], 0, 0))],
        out_specs=pl.BlockSpec(memory_space=pl.ANY),
        scratch_shapes=[pltpu.VMEM((D_MODEL, 2 * D_FF), BF16),
                        pltpu.VMEM((D_FF, D_MODEL), BF16),
                        pltpu.VMEM((2, MOE_BLOCK * ROW_TILE, LANES), F32),
                        pltpu.VMEM((2, MOE_BLOCK * ROW_TILE, LANES), F32),
                        pltpu.SemaphoreType.DMA((2,)), pltpu.SemaphoreType.DMA((2,))],
    )
    return pl.pallas_call(
        functools.partial(_expert_kernel, n_blocks=nb, dump_row=n_rows),
        grid_spec=grid_spec,
        out_shape=jax.ShapeDtypeStruct(((n_rows + MOE_BLOCK) * ROW_TILE, LANES), F32),
        compiler_params=_cparams(("arbitrary",)),
        name="moe_experts",
        interpret=interpret,
    )(block_e, src_tab, src_tab, dst_tab, dst_tab, hn_rt, w_gate_up,
      b_gate_up.reshape(N_EXPERTS, 1, 2 * D_FF), w_down, b_down.reshape(N_EXPERTS, 1, D_MODEL))


def _combine_kernel(y_ref, h_ref, gate_ref, gfin_ref, outp_ref, outs_ref, *, tm, n_first):
    gate = gate_ref[...]
    parts = []
    for c in range(ROW_TILE):
        part = gate[:, 0:1] * y_ref[pl.ds(c, tm, stride=TOP_K * ROW_TILE), :]
        for kk in range(1, TOP_K):
            part = part + gate[:, kk:kk + 1] * y_ref[pl.ds(kk * ROW_TILE + c, tm, stride=TOP_K * ROW_TILE), :]
        parts.append(part)
    h = h_ref[...] + jnp.concatenate(parts, axis=-1)
    res = h * lax.rsqrt(jnp.mean(h * h, axis=-1, keepdims=True) + EPS) * gfin_ref[...]

    @pl.when(pl.program_id(0) < n_first)
    def _():
        outp_ref[...] = res

    @pl.when(pl.program_id(0) >= n_first)
    def _():
        outs_ref[...] = res


def _combine(y, h, gate_tm, g_final, t_first, tm, interpret):
    T = h.shape[0]
    n_first = t_first // tm
    outp_spec, outs_spec = _dual_rows(n_first, D_MODEL, tm)
    return pl.pallas_call(
        functools.partial(_combine_kernel, tm=tm, n_first=n_first),
        grid=(T // tm,),
        in_specs=[pl.BlockSpec((tm * TOP_K * ROW_TILE, LANES), lambda i: (i, 0)),
                  pl.BlockSpec((tm, D_MODEL), lambda i: (i, 0)),
                  pl.BlockSpec((tm, 2 * TOP_K), lambda i: (i, 0)),
                  pl.BlockSpec((1, D_MODEL), lambda i: (0, 0))],
        out_specs=[outp_spec, outs_spec],
        out_shape=[jax.ShapeDtypeStruct((t_first, D_MODEL), F32),
                   jax.ShapeDtypeStruct((T - t_first, D_MODEL), F32)],
        compiler_params=_cparams(("arbitrary",)),
        name="moe_combine",
        interpret=interpret,
    )(y, h, gate_tm, g_final.reshape(1, D_MODEL))


def _routing_tables(idx_rows, counts, T):
    tk = T * TOP_K
    top_idx = idx_rows[:TOP_K].T.reshape(-1)
    rank = idx_rows[TOP_K:].T.reshape(-1)
    counts = counts.astype(jnp.int32)
    blocks_e = (counts + MOE_BLOCK - 1) // MOE_BLOCK
    cum_blocks = jnp.cumsum(blocks_e)
    start_row = (cum_blocks - blocks_e) * MOE_BLOCK
    experts = jnp.arange(N_EXPERTS, dtype=jnp.int32)
    slot_of = jnp.sum(jnp.where(top_idx[:, None] == experts, start_row, 0), axis=-1) + rank
    n_blocks = -(-tk // MOE_BLOCK) + N_EXPERTS
    n_slots = n_blocks * MOE_BLOCK
    pad_len = jnp.concatenate([blocks_e * MOE_BLOCK - counts, n_slots - cum_blocks[-1:] * MOE_BLOCK])
    pad_base = jnp.concatenate([start_row + counts, cum_blocks[-1:] * MOE_BLOCK])
    pad_end = jnp.cumsum(pad_len)
    i = jnp.arange(n_slots - tk, dtype=jnp.int32)
    seg = i[:, None] >= pad_end[None, :]
    shift = jnp.concatenate([pad_base[:1], pad_base[1:] - pad_end[:-1]])
    incr = jnp.concatenate([shift[1:] - shift[:-1], jnp.zeros((1,), jnp.int32)])
    pad_slot = i + shift[0] + jnp.sum(jnp.where(seg, incr, 0), axis=-1)
    source = jnp.argsort(jnp.concatenate([slot_of, pad_slot])).astype(jnp.int32)
    real = source < tk
    slots = jnp.arange(n_slots, dtype=jnp.int32)
    src_tok = jnp.where(real, source // TOP_K, 0).reshape(n_blocks, 1, MOE_BLOCK)
    dump = tk + slots % MOE_BLOCK
    dst_row = jnp.where(real, source, dump).reshape(n_blocks, 1, MOE_BLOCK)
    dst_tab = jnp.concatenate([dump[:MOE_BLOCK].reshape(1, 1, MOE_BLOCK), dst_row], axis=0)
    blocks = jnp.arange(n_blocks, dtype=jnp.int32)
    block_e = jnp.sum((blocks[:, None] >= cum_blocks[None, :]).astype(jnp.int32), axis=-1)
    last_e = jnp.sum((cum_blocks[-1] - 1 >= cum_blocks).astype(jnp.int32))
    block_e = jnp.minimum(block_e, last_e).astype(jnp.int32)
    return src_tok, dst_tab, block_e


def _layer(xp, xs, S, g_mix, w_in, conv_w, conv_b, b_gates, g_head, w_out, g_ffn, w_router, b_router,
           w_gate_up, b_gate_up, w_down, b_down, g_final, interpret=False):
    T = xp.shape[0] + xs.shape[0]
    B = T // S

    w_in_p = jnp.pad(w_in, ((0, 0), (0, IN_PAD - w_in.shape[1]))).astype(BF16)
    cw_p = jnp.pad(conv_w, ((0, SUBLANES - CONV_W), (0, 0)))
    cb = conv_b.reshape(1, 2 * ML_WIDTH)
    bg_p = jnp.pad(b_gates, (0, LANES - N_GATES)).reshape(1, LANES)
    (q1, k1, v1, q4, k4, v4, q16, k16, v16, qm, kt, vm, om, gates_t) = _inproj(
        xp, xs, S, g_mix, w_in_p, cw_p, cb, bg_p, interpret)

    as_phase = lambda a: a.reshape(B, 1, S, ATT_WIDTH)
    o1, l1 = _attn_branch(as_phase(q1), as_phase(k1), as_phase(v1), 1, interpret)
    o4, l4 = _attn_branch(q4, k4, v4, 4, interpret)
    o16, l16 = _attn_branch(q16, k16, v16, 16, interpret)

    hf, hb = _mlstm(qm, kt, vm, gates_t, B, S, interpret)

    w_rt = w_router.T.astype(BF16)
    b_r = jnp.broadcast_to(b_router[:, None], (N_EXPERTS, LANES))
    h, hn, idx_rows, gate_rows, counts = _outproj(
        xp, xs, S, o1.reshape(T, ATT_WIDTH), o4, o16, l1.reshape(T, LANES), l4, l16,
        hf.reshape(T, ML_WIDTH), hb.reshape(T, ML_WIDTH), om, g_head, w_out.astype(BF16), g_ffn,
        w_rt, b_r, interpret)

    src_tok, dst_tab, block_e = _routing_tables(idx_rows, counts[:, 0], T)
    y = _experts(hn, src_tok, dst_tab, block_e, w_gate_up, b_gate_up, w_down, b_down, T * TOP_K, interpret)
    return _combine(y, h, gate_rows.T, g_final, xp.shape[0], 256, interpret)


def kernel(x_prompt, x_sample, g_mix, w_in, conv_w, conv_b, b_gates, g_head, w_out, g_ffn, w_router,
           b_router, w_gate_up, b_gate_up, w_down, b_down, g_final):
    S = x_prompt.shape[1]
    yp, ys = _layer(x_prompt.reshape(-1, D_MODEL), x_sample.reshape(-1, D_MODEL), S,
                    g_mix, w_in, conv_w, conv_b, b_gates, g_head, w_out, g_ffn, w_router, b_router,
                    w_gate_up, b_gate_up, w_down, b_down, g_final)
    return (yp.reshape(x_prompt.shape), ys.reshape(x_sample.shape))
```

```python
import functools

import jax
import jax.numpy as jnp
from jax import lax
from jax.experimental import pallas as pl
from jax.experimental.pallas import tpu as pltpu

F32 = jnp.float32
BF16 = jnp.bfloat16

D_MODEL = 1024
ATT_HEADS = 8
ATT_HEAD_DIM = 64
ATT_WIDTH = ATT_HEADS * ATT_HEAD_DIM
DILATIONS = (1, 4, 16)
ATT_HALF = 64
ML_HEADS = 4
ML_HEAD_DIM = 128
ML_WIDTH = ML_HEADS * ML_HEAD_DIM
ML_CHUNK = 128
CONV_W = 5
N_GATES = 2 * 2 * ML_HEADS
N_EXPERTS = 32
TOP_K = 4
D_FF = 1024
SWIGLU_LIMIT = 7.0
SWIGLU_ALPHA = 1.702
MOE_BLOCK = 512
EPS = 1e-6
NEG = -1e30

LANES = 128
SUBLANES = 8
IN_PAD = 7 * 512 + LANES
TOK_TILE = 512
ROW_TILE = D_MODEL // LANES
VMEM_LIMIT = 56 * 1024 * 1024


def _cparams(sem):
    return pltpu.CompilerParams(dimension_semantics=sem, vmem_limit_bytes=VMEM_LIMIT)


def _dual_rows(n_first, width, rows=TOK_TILE):
    first = pl.BlockSpec((rows, width), lambda i: (jnp.minimum(i, n_first - 1), 0))
    second = pl.BlockSpec((rows, width), lambda i: (jnp.maximum(i - n_first, 0), 0))
    return first, second


def _inproj_kernel(xp_ref, xs_ref, pp_ref, pn_ref, sp_ref, sn_ref, g_ref, w_ref, cw_ref, cb_ref, bg_ref,
                   q1_ref, k1_ref, v1_ref, q4_ref, k4_ref, v4_ref, q16_ref, k16_ref, v16_ref,
                   qm_ref, kt_ref, vm_ref, om_ref, gt_ref,
                   ext, stage_q, stage_k, stage_v, *, n_first, tiles_per_seq):
    i = pl.program_id(0)
    first = i < n_first
    ti = lax.rem(i, tiles_per_seq)
    tm = TOK_TILE

    def norm(v):
        vn = v * lax.rsqrt(jnp.mean(v * v, axis=-1, keepdims=True) + EPS)
        return (vn * g_ref[...]).astype(BF16)

    xn = norm(jnp.where(first, xp_ref[...], xs_ref[...]))
    prev = jnp.where(ti > 0, jnp.where(first, pp_ref[0], sp_ref[0]), 0.0)
    nxt = jnp.where(ti < tiles_per_seq - 1, jnp.where(first, pn_ref[0], sn_ref[0]), 0.0)
    xn_halo = norm(jnp.concatenate([prev, nxt], axis=0))

    def proj(lo, hi, lhs=xn):
        return jnp.dot(lhs, w_ref[:, lo:hi], preferred_element_type=F32)

    def emit_attn(col, scale, stage, r1, r4, r16):
        p = proj(col, col + ATT_WIDTH)
        if scale is not None:
            p = p * scale
        r1[...] = p.astype(BF16)
        for j in range(ATT_WIDTH // LANES):
            stage[j] = p[:, j * LANES:(j + 1) * LANES]
        for d, ref in ((4, r4), (16, r16)):
            for ph in range(d):
                for j in range(ATT_WIDTH // LANES):
                    ref[ph, :, j * LANES:(j + 1) * LANES] = (
                        stage.at[j][pl.ds(ph, tm // d, stride=d), :].astype(BF16))

    emit_attn(0, ATT_HEAD_DIM ** -0.5, stage_q, q1_ref, q4_ref, q16_ref)
    emit_attn(512, None, stage_k, k1_ref, k4_ref, k16_ref)
    emit_attn(1024, None, stage_v, v1_ref, v4_ref, v16_ref)

    halo_qk = proj(1536, 2560, xn_halo)
    ext[0:SUBLANES, :] = halo_qk[0:SUBLANES]
    ext[SUBLANES:SUBLANES + tm, :] = proj(1536, 2560)
    ext[SUBLANES + tm:, :] = halo_qk[SUBLANES:]
    base = SUBLANES - CONV_W // 2
    acc = cb_ref[...] + ext[base:base + tm, :] * cw_ref[0:1, :]
    for j in range(1, CONV_W):
        acc = acc + ext[base + j:base + j + tm, :] * cw_ref[j:j + 1, :]
    act = acc * jax.nn.sigmoid(acc)
    qm_ref[...] = (act[:, :ML_WIDTH] * (ML_HEAD_DIM ** -0.5)).astype(BF16)
    for c in range(tm // ML_CHUNK):
        kt_ref[c] = act[c * ML_CHUNK:(c + 1) * ML_CHUNK, ML_WIDTH:].T.astype(BF16)

    vm_ref[...] = proj(2560, 3072).astype(BF16)
    om_ref[...] = proj(3072, 3584)
    gates = proj(3584, IN_PAD) + bg_ref[...]
    for c in range(tm // ML_CHUNK):
        gt_ref[c] = gates[c * ML_CHUNK:(c + 1) * ML_CHUNK, :].T[0:N_GATES, :]


def _inproj(xp, xs, S, g_mix, w_in_p, cw_p, cb, bg_p, interpret):
    tm = TOK_TILE
    n_first = xp.shape[0] // tm
    T = xp.shape[0] + xs.shape[0]
    B = T // S
    tps = S // tm
    per = tm // SUBLANES
    row = lambda w: pl.BlockSpec((tm, w), lambda i: (i, 0))
    const = lambda shape: pl.BlockSpec(shape, lambda i: (0,) * len(shape))
    xp_spec, xs_spec = _dual_rows(n_first, D_MODEL)

    def halo(n_tiles, shift, offset):
        last = n_tiles * per - 1
        return pl.BlockSpec(
            (1, SUBLANES, D_MODEL),
            lambda i: (jnp.clip((jnp.clip(i - shift, 0, n_tiles - 1)) * per + offset, 0, last), 0, 0))

    n_second = xs.shape[0] // tm
    xp3 = xp.reshape(xp.shape[0] // SUBLANES, SUBLANES, D_MODEL)
    xs3 = xs.reshape(xs.shape[0] // SUBLANES, SUBLANES, D_MODEL)
    phase = lambda d, w: pl.BlockSpec((None, d, tm // d, w), lambda i: (i // tps, 0, i % tps, 0))
    bf = lambda shape: jax.ShapeDtypeStruct(shape, BF16)
    nat = bf((T, ATT_WIDTH))
    ph4 = bf((B, 4, S // 4, ATT_WIDTH))
    ph16 = bf((B, 16, S // 16, ATT_WIDTH))
    return pl.pallas_call(
        functools.partial(_inproj_kernel, n_first=n_first, tiles_per_seq=tps),
        grid=(T // tm,),
        in_specs=[xp_spec, xs_spec,
                  halo(n_first, 0, -1), halo(n_first, 0, per),
                  halo(n_second, n_first, -1), halo(n_second, n_first, per),
                  const((1, D_MODEL)), const((D_MODEL, IN_PAD)),
                  const((SUBLANES, 2 * ML_WIDTH)), const((1, 2 * ML_WIDTH)), const((1, LANES))],
        out_specs=[row(512), row(512), row(512),
                   phase(4, 512), phase(4, 512), phase(4, 512),
                   phase(16, 512), phase(16, 512), phase(16, 512),
                   row(512),
                   pl.BlockSpec((None, tm // ML_CHUNK, ML_WIDTH, ML_CHUNK), lambda i: (i // tps, i % tps, 0, 0)),
                   row(512), row(512),
                   pl.BlockSpec((None, tm // ML_CHUNK, N_GATES, ML_CHUNK), lambda i: (i // tps, i % tps, 0, 0))],
        out_shape=[nat, nat, nat, ph4, ph4, ph4, ph16, ph16, ph16,
                   bf((T, ML_WIDTH)),
                   bf((B, S // ML_CHUNK, ML_WIDTH, ML_CHUNK)),
                   bf((T, ML_WIDTH)),
                   jax.ShapeDtypeStruct((T, ML_WIDTH), F32),
                   jax.ShapeDtypeStruct((B, S // ML_CHUNK, N_GATES, ML_CHUNK), F32)],
        scratch_shapes=[pltpu.VMEM((tm + 2 * SUBLANES, 2 * ML_WIDTH), F32),
                        pltpu.VMEM((ATT_WIDTH // LANES, tm, LANES), F32),
                        pltpu.VMEM((ATT_WIDTH // LANES, tm, LANES), F32),
                        pltpu.VMEM((ATT_WIDTH // LANES, tm, LANES), F32)],
        compiler_params=_cparams(("arbitrary",)),
        name="inproj",
        interpret=interpret,
    )(xp, xs, xp3, xp3, xs3, xs3, g_mix.reshape(1, D_MODEL), w_in_p, cw_p, cb, bg_p)


def _attn_kernel(q_ref, k_ref, v_ref, o_ref, lse_ref, *, dil, qb, kb, m_len, n_sub):
    lane = lax.broadcasted_iota(jnp.int32, (1, LANES), 1)
    lo = lane < ATT_HEAD_DIM
    zero = jnp.zeros((), BF16)
    row = lax.broadcasted_iota(jnp.int32, (qb, kb), 0)
    col = lax.broadcasted_iota(jnp.int32, (qb, kb), 1)
    for sub in range(n_sub):
        blk = pl.program_id(2) * n_sub + sub
        rows = slice(sub * qb, (sub + 1) * qb)
        start = jnp.clip(blk * qb - ATT_HALF, 0, m_len - kb)
        start = pl.multiple_of(start, ATT_HALF)
        off = start - blk * qb
        q = q_ref[rows, :]
        k = k_ref[pl.ds(start, kb), :]
        v = v_ref[pl.ds(start, kb), :]
        absd = jnp.abs(col - row + off).astype(F32)
        valid = absd <= float(ATT_HALF)
        lse_tile = jnp.zeros((qb, LANES), F32)
        for pair in range(ATT_HEADS // 2):
            sl = slice(pair * LANES, (pair + 1) * LANES)
            qp, kp, vp = q[:, sl], k[:, sl], v[:, sl]
            outs = []
            for hh in range(2):
                h = 2 * pair + hh
                slope = 2.0 ** (-(8.0 / ATT_HEADS) * (h + 1))
                qh = jnp.where(lo if hh == 0 else jnp.logical_not(lo), qp, zero)
                s = lax.dot_general(qh, kp, (((1,), (1,)), ((), ())), preferred_element_type=F32)
                s = jnp.where(valid, s - absd * (slope * dil), NEG)
                mx = jnp.max(s, axis=-1, keepdims=True)
                p = jnp.exp(s - mx)
                l = jnp.sum(p, axis=-1, keepdims=True)
                o = jnp.dot(p.astype(BF16), vp, preferred_element_type=F32)
                outs.append(o / l)
                lse_tile = jnp.where(lane == h, mx + jnp.log(l), lse_tile)
            o_ref[rows, sl] = jnp.where(lo, outs[0], outs[1]).astype(BF16)
        lse_ref[rows, :] = lse_tile


def _attn_branch(q, k, v, dil, interpret):
    B, _, m_len, _ = q.shape
    qb = 128
    n_sub = min(8, m_len // qb)
    kb = min(qb + 2 * ATT_HALF, m_len)
    qspec = pl.BlockSpec((None, None, n_sub * qb, ATT_WIDTH), lambda b, p, i: (b, p, i, 0))
    kvspec = pl.BlockSpec((None, None, m_len, ATT_WIDTH), lambda b, p, i: (b, p, 0, 0))
    return pl.pallas_call(
        functools.partial(_attn_kernel, dil=dil, qb=qb, kb=kb, m_len=m_len, n_sub=n_sub),
        grid=(B, dil, m_len // (n_sub * qb)),
        in_specs=[qspec, kvspec, kvspec],
        out_specs=[qspec, pl.BlockSpec((None, None, n_sub * qb, LANES), lambda b, p, i: (b, p, i, 0))],
        out_shape=[jax.ShapeDtypeStruct((B, dil, m_len, ATT_WIDTH), BF16),
                   jax.ShapeDtypeStruct((B, dil, m_len, LANES), F32)],
        compiler_params=_cparams(("arbitrary", "arbitrary", "arbitrary")),
        name=f"attn_d{dil}",
        interpret=interpret,
    )(q, k, v)


def _mlstm_kernel(qf_ref, ktf_ref, vf_ref, gf_ref, qb_ref, ktb_ref, vb_ref, gb_ref,
                  hf_ref, hb_ref, cn_st, m_st):
    L = ML_CHUNK
    hd = ML_HEAD_DIM

    @pl.when(pl.program_id(1) == 0)
    def _():
        cn_st[...] = jnp.zeros_like(cn_st)
        m_st[...] = jnp.full_like(m_st, NEG)

    t_idx = lax.broadcasted_iota(jnp.int32, (L, L), 0)
    s_idx = lax.broadcasted_iota(jnp.int32, (L, L), 1)
    ones_col = (lax.broadcasted_iota(jnp.int32, (L, LANES), 1) == 0).astype(BF16)
    streams = ((qf_ref, ktf_ref, vf_ref, gf_ref, hf_ref), (qb_ref, ktb_ref, vb_ref, gb_ref, hb_ref))

    prep = []
    for direction in range(2):
        tri = (s_idx <= t_idx) if direction == 0 else (s_idx >= t_idx)
        last = L - 1 if direction == 0 else 0
        rows8 = slice(direction * SUBLANES, (direction + 1) * SUBLANES)
        g8 = streams[direction][3][rows8, :]
        seen_by = (t_idx <= s_idx) if direction == 0 else (t_idx >= s_idx)
        b8 = jnp.dot(jax.nn.log_sigmoid(g8), seen_by.astype(F32), preferred_element_type=F32,
                     precision=lax.Precision.HIGHEST)
        b8 = pltpu.roll(b8, ML_HEADS, axis=0)
        nc8 = g8 - b8
        m_prev8 = m_st[rows8, :]
        b_end8 = b8[:, last:last + 1]
        a8 = b_end8 + nc8
        m_loc8 = jnp.max(a8, axis=-1, keepdims=True)
        w8 = jnp.exp(a8 - m_loc8)
        m_new8 = jnp.maximum(b_end8 + m_prev8[:, 0:1], m_loc8)
        sp8 = jnp.exp(b_end8 + m_prev8[:, 0:1] - m_new8)
        sl8 = jnp.exp(m_loc8 - m_new8)
        m_st[rows8, :] = jnp.broadcast_to(m_new8, (SUBLANES, LANES))
        b_cols = jnp.concatenate([b8, jnp.zeros((LANES - SUBLANES, L), F32)], axis=0).T
        prep.append((tri, nc8, m_prev8, w8, sp8, sl8, b_cols))

    outs = ([], [])
    for h in range(ML_HEADS):
        hs = slice(h * hd, (h + 1) * hd)
        for direction in range(2):
            q_ref, kt_ref, v_ref, _, _ = streams[direction]
            tri, nc8, m_prev8, w8, sp8, sl8, b_cols = prep[direction]
            st = direction * ML_HEADS + h
            q = q_ref[:, hs]
            kt = kt_ref[hs, :]
            v_aug = jnp.concatenate([v_ref[:, hs], ones_col], axis=-1)
            cn_prev = cn_st[st]
            m_prev = m_prev8[h:h + 1, 0:1]

            seen = jnp.where(tri, nc8[h:h + 1, :], NEG)
            mm = jnp.maximum(jnp.max(seen, axis=-1, keepdims=True), m_prev)
            dw = jnp.exp(seen - mm)
            iw = jnp.exp(m_prev - mm)
            qk_s = jnp.dot(q, kt, preferred_element_type=F32) * dw
            intra = jnp.dot(qk_s.astype(BF16), v_aug, preferred_element_type=F32)
            carried = jnp.dot(q, cn_prev.astype(BF16), preferred_element_type=F32)
            num = intra[:, :hd] + iw * carried[:, :hd]
            den = intra[:, hd:hd + 1] + iw * carried[:, hd:hd + 1]
            floor = jnp.exp(-(b_cols[:, h:h + 1] + mm))
            outs[direction].append(num / jnp.maximum(jnp.abs(den), floor))

            kw = (kt.astype(F32) * w8[h:h + 1, :]).astype(BF16)
            cn_loc = jnp.dot(kw, v_aug, preferred_element_type=F32)
            cn_st[st] = sp8[h:h + 1, :] * cn_prev + sl8[h:h + 1, :] * cn_loc
    hf_ref[...] = jnp.concatenate(outs[0], axis=-1)
    hb_ref[...] = jnp.concatenate(outs[1], axis=-1)


def _mlstm(qm, kt, vm, gates_t, B, S, interpret):
    L = ML_CHUNK
    nc = S // L
    fwd = lambda s: s
    bwd = lambda s: nc - 1 - s

    def specs(cidx):
        blk = lambda w: pl.BlockSpec((None, L, w), lambda b, s: (b, cidx(s), 0))
        chunk = lambda r: pl.BlockSpec((None, None, r, L), lambda b, s: (b, cidx(s), 0, 0))
        return [blk(ML_WIDTH), chunk(ML_WIDTH), blk(ML_WIDTH), chunk(N_GATES)]

    q3 = qm.reshape(B, S, ML_WIDTH)
    v3 = vm.reshape(B, S, ML_WIDTH)
    hblk = lambda cidx: pl.BlockSpec((None, L, ML_WIDTH), lambda b, s: (b, cidx(s), 0))
    return pl.pallas_call(
        _mlstm_kernel,
        grid=(B, nc),
        in_specs=specs(fwd) + specs(bwd),
        out_specs=[hblk(fwd), hblk(bwd)],
        out_shape=[jax.ShapeDtypeStruct((B, S, ML_WIDTH), F32),
                   jax.ShapeDtypeStruct((B, S, ML_WIDTH), F32)],
        scratch_shapes=[pltpu.VMEM((2 * ML_HEADS, ML_HEAD_DIM, 2 * ML_HEAD_DIM), F32),
                        pltpu.VMEM((2 * SUBLANES, LANES), F32)],
        compiler_params=_cparams(("arbitrary", "arbitrary")),
        name="mlstm",
        interpret=interpret,
    )(q3, kt, v3, gates_t, q3, kt, v3, gates_t)


def _outproj_kernel(xp_ref, xs_ref, o1_ref, o4_ref, o16_ref, l1_ref, l4_ref, l16_ref,
                    hf_ref, hb_ref, om_ref, gh_ref, wo_ref, gf_ref, wrt_ref, br_ref,
                    h_ref, hn_ref, idx_ref, gate_ref, cnt_ref,
                    carry, nat4, nat16, lnat4, lnat16, *, n_first):
    tm = TOK_TILE
    step = pl.program_id(0)

    @pl.when(step == 0)
    def _():
        carry[...] = jnp.zeros_like(carry)

    for d, o_ref, l_ref, nat, lnat in ((4, o4_ref, l4_ref, nat4, lnat4), (16, o16_ref, l16_ref, nat16, lnat16)):
        for ph in range(d):
            rows = pl.ds(ph, tm // d, stride=d)
            lnat[rows, :] = l_ref[ph]
            for j in range(ATT_WIDTH // LANES):
                nat.at[j][rows, :] = o_ref[ph, :, j * LANES:(j + 1) * LANES].astype(F32)

    l1, l2, l3 = l1_ref[...], lnat4[...], lnat16[...]
    mx = jnp.maximum(jnp.maximum(l1, l2), l3)
    e1, e2, e3 = jnp.exp(l1 - mx), jnp.exp(l2 - mx), jnp.exp(l3 - mx)
    inv = 1.0 / (e1 + e2 + e3)
    er = lax.broadcasted_iota(jnp.int32, (LANES, ATT_WIDTH), 0)
    ec = lax.broadcasted_iota(jnp.int32, (LANES, ATT_WIDTH), 1)
    expand = (ec // ATT_HEAD_DIM == er).astype(BF16)

    def widen(w):
        hi = w.astype(BF16)
        lo = (w - hi.astype(F32)).astype(BF16)
        return (jnp.dot(hi, expand, preferred_element_type=F32)
                + jnp.dot(lo, expand, preferred_element_type=F32))

    slabs = lambda nat: jnp.concatenate([nat[j] for j in range(ATT_WIDTH // LANES)], axis=-1)
    attn = (widen(e1 * inv) * o1_ref[...].astype(F32)
            + widen(e2 * inv) * slabs(nat4)
            + widen(e3 * inv) * slabs(nat16))

    hsum = hf_ref[...] + hb_ref[...]
    normed = []
    for hh in range(ML_HEADS):
        hv = hsum[:, hh * ML_HEAD_DIM:(hh + 1) * ML_HEAD_DIM]
        normed.append(hv * lax.rsqrt(jnp.mean(hv * hv, axis=-1, keepdims=True) + EPS))
    ml = jax.nn.sigmoid(om_ref[...]) * (jnp.concatenate(normed, axis=-1) * gh_ref[...])

    mix = jnp.concatenate([attn.astype(BF16), ml.astype(BF16)], axis=-1)
    x = jnp.where(step < n_first, xp_ref[...], xs_ref[...])
    h = x + jnp.dot(mix, wo_ref[...], preferred_element_type=F32)
    h_ref[...] = h
    hn = h * lax.rsqrt(jnp.mean(h * h, axis=-1, keepdims=True) + EPS) * gf_ref[...]
    for j in range(D_MODEL // LANES):
        hn_ref[pl.ds(j, tm, stride=SUBLANES), :] = hn[:, j * LANES:(j + 1) * LANES]

    logits = lax.dot_general(wrt_ref[...], hn.astype(BF16), (((1,), (1,)), ((), ())),
                             preferred_element_type=F32) + br_ref[:, 0:1]
    eid = lax.broadcasted_iota(jnp.int32, (N_EXPERTS, tm), 0)
    work = logits
    vals, idxs, hots = [], [], []
    for _ in range(TOP_K):
        top = jnp.max(work, axis=0, keepdims=True)
        idx = jnp.min(jnp.where(work == top, eid, N_EXPERTS), axis=0, keepdims=True)
        hot = eid == idx
        work = jnp.where(hot, -jnp.inf, work)
        vals.append(top)
        idxs.append(idx)
        hots.append(hot)
    exps = [jnp.exp(v - vals[0]) for v in vals]
    inv_den = 1.0 / (exps[0] + exps[1] + exps[2] + exps[3])

    cnt = jnp.where(hots[0] | hots[1] | hots[2] | hots[3], 1.0, 0.0)
    r_idx = lax.broadcasted_iota(jnp.int32, (tm, tm), 0)
    c_idx = lax.broadcasted_iota(jnp.int32, (tm, tm), 1)
    before = (r_idx < c_idx).astype(BF16)
    rank_all = jnp.dot(cnt.astype(BF16), before, preferred_element_type=F32) + carry[:, 0:1]
    carry[...] = carry[...] + jnp.sum(cnt, axis=1, keepdims=True)
    cnt_ref[...] = carry[...]

    ranks = [jnp.sum(jnp.where(hots[kk], rank_all, 0.0), axis=0, keepdims=True).astype(jnp.int32)
             for kk in range(TOP_K)]
    idx_ref[...] = jnp.concatenate(idxs + ranks, axis=0)
    gate_ref[...] = jnp.concatenate([e * inv_den for e in exps] + [jnp.zeros((TOP_K, tm), F32)], axis=0)


def _outproj(xp, xs, S, o1, o4, o16, l1, l4, l16, hf, hb, om, g_head, w_out_b, g_ffn, w_rt, b_r, interpret):
    tm = TOK_TILE
    n_first = xp.shape[0] // tm
    T = xp.shape[0] + xs.shape[0]
    tps = S // tm
    row = lambda w: pl.BlockSpec((tm, w), lambda i: (i, 0))
    col = lambda r: pl.BlockSpec((r, tm), lambda i: (0, i))
    const = lambda shape: pl.BlockSpec(shape, lambda i: (0,) * len(shape))
    phase = lambda d, w: pl.BlockSpec((None, d, tm // d, w), lambda i: (i // tps, 0, i % tps, 0))
    xp_spec, xs_spec = _dual_rows(n_first, D_MODEL)
    return pl.pallas_call(
        functools.partial(_outproj_kernel, n_first=n_first),
        grid=(T // tm,),
        in_specs=[xp_spec, xs_spec, row(512), phase(4, 512), phase(16, 512),
                  row(LANES), phase(4, LANES), phase(16, LANES),
                  row(512), row(512), row(512), const((1, ML_WIDTH)),
                  const((D_MODEL, D_MODEL)), const((1, D_MODEL)),
                  const((N_EXPERTS, D_MODEL)), const((N_EXPERTS, LANES))],
        out_specs=[row(D_MODEL), pl.BlockSpec((tm * ROW_TILE, LANES), lambda i: (i, 0)),
                   col(2 * TOP_K), col(2 * TOP_K), const((N_EXPERTS, LANES))],
        out_shape=[jax.ShapeDtypeStruct((T, D_MODEL), F32),
                   jax.ShapeDtypeStruct((T * ROW_TILE, LANES), F32),
                   jax.ShapeDtypeStruct((2 * TOP_K, T), jnp.int32),
                   jax.ShapeDtypeStruct((2 * TOP_K, T), F32),
                   jax.ShapeDtypeStruct((N_EXPERTS, LANES), F32)],
        scratch_shapes=[pltpu.VMEM((N_EXPERTS, LANES), F32),
                        pltpu.VMEM((ATT_WIDTH // LANES, tm, LANES), F32),
                        pltpu.VMEM((ATT_WIDTH // LANES, tm, LANES), F32),
                        pltpu.VMEM((tm, LANES), F32), pltpu.VMEM((tm, LANES), F32)],
        compiler_params=_cparams(("arbitrary",)),
        name="outproj_router",
        interpret=interpret,
    )(xp, xs, o1, o4, o16, l1, l4, l16, hf, hb, om, g_head.reshape(1, ML_WIDTH), w_out_b,
      g_ffn.reshape(1, D_MODEL), w_rt, b_r)


def _expert_kernel(be_ref, nu_ref, src_cur, src_next, hn_ref, wgu_ref, bgu_ref, wd_ref, bd_ref, y_ref,
                   wgu_b, wd_b, xbuf, gsem):
    j = pl.program_id(0)
    slot = lax.rem(j, 2)
    other = 1 - slot
    rows = MOE_BLOCK * ROW_TILE
    active = j < nu_ref[0]

    def gather_wait(s):
        pltpu.make_async_copy(hn_ref.at[pl.ds(0, rows)], xbuf.at[s], gsem.at[s]).wait()

    def gather_row(tab, r, s):
        tok = pl.multiple_of(tab[0, 0, r] * ROW_TILE, ROW_TILE)
        pltpu.make_async_copy(hn_ref.at[pl.ds(tok, ROW_TILE)],
                              xbuf.at[s, pl.ds(r * ROW_TILE, ROW_TILE)], gsem.at[s]).start()

    @pl.when(j == 0)
    def _():
        def first(r, carry):
            gather_row(src_cur, r, 0)
            return carry

        lax.fori_loop(0, MOE_BLOCK, first, 0, unroll=8)

    @pl.when(jnp.logical_and(active, jnp.logical_or(j == 0, be_ref[j] != be_ref[jnp.maximum(j - 1, 0)])))
    def _():
        wgu_b[...] = wgu_ref[0].astype(BF16)
        wd_b[...] = wd_ref[0].astype(BF16)

    @pl.when(active)
    def _():
        gather_wait(slot)
        for r in range(MOE_BLOCK):
            gather_row(src_next, r, other)

        xv = xbuf.at[slot]
        x = jnp.concatenate([xv[pl.ds(c, MOE_BLOCK, stride=ROW_TILE), :] for c in range(ROW_TILE)],
                            axis=-1).astype(BF16)
        gu = jnp.dot(x, wgu_b[...], preferred_element_type=F32) + bgu_ref[0]
        g = jnp.minimum(gu[:, :D_FF], SWIGLU_LIMIT)
        u = jnp.clip(gu[:, D_FF:], -SWIGLU_LIMIT, SWIGLU_LIMIT)
        hdn = (u + 1.0) * (g * jax.nn.sigmoid(SWIGLU_ALPHA * g))
        y = jnp.dot(hdn.astype(BF16), wd_b[...], preferred_element_type=F32) + bd_ref[0]
        for c in range(ROW_TILE):
            y_ref[pl.ds(c, MOE_BLOCK, stride=ROW_TILE), :] = y[:, c * LANES:(c + 1) * LANES]

    @pl.when(j == nu_ref[0] - 1)
    def _():
        gather_wait(other)

    @pl.when(jnp.logical_not(active))
    def _():
        y_ref[...] = jnp.zeros_like(y_ref)


def _experts(hn_rt, src_tab, block_e, n_used, w_gate_up, b_gate_up, w_down, b_down, interpret):
    nb = src_tab.shape[0]
    tab = lambda f: pl.BlockSpec((1, 1, MOE_BLOCK), lambda j, be, nu: (f(j, nu), 0, 0), memory_space=pltpu.SMEM)
    grid_spec = pltpu.PrefetchScalarGridSpec(
        num_scalar_prefetch=2,
        grid=(nb,),
        in_specs=[tab(lambda j, nu: j), tab(lambda j, nu: jnp.minimum(j + 1, nu[0] - 1)),
                  pl.BlockSpec(memory_space=pl.ANY),
                  pl.BlockSpec((1, D_MODEL, 2 * D_FF), lambda j, be, nu: (be[j], 0, 0)),
                  pl.BlockSpec((1, 1, 2 * D_FF), lambda j, be, nu: (be[j], 0, 0)),
                  pl.BlockSpec((1, D_FF, D_MODEL), lambda j, be, nu: (be[j], 0, 0)),
                  pl.BlockSpec((1, 1, D_MODEL), lambda j, be, nu: (be[j], 0, 0))],
        out_specs=pl.BlockSpec((MOE_BLOCK * ROW_TILE, LANES), lambda j, be, nu: (j, 0)),
        scratch_shapes=[pltpu.VMEM((D_MODEL, 2 * D_FF), BF16),
                        pltpu.VMEM((D_FF, D_MODEL), BF16),
                        pltpu.VMEM((2, MOE_BLOCK * ROW_TILE, LANES), F32),
                        pltpu.SemaphoreType.DMA((2,))],
    )
    return pl.pallas_call(
        _expert_kernel,
        grid_spec=grid_spec,
        out_shape=jax.ShapeDtypeStruct((nb * MOE_BLOCK * ROW_TILE, LANES), F32),
        compiler_params=_cparams(("arbitrary",)),
        name="moe_experts",
        interpret=interpret,
    )(block_e, n_used, src_tab, src_tab, hn_rt, w_gate_up,
      b_gate_up.reshape(N_EXPERTS, 1, 2 * D_FF), w_down, b_down.reshape(N_EXPERTS, 1, D_MODEL))


def _combine_kernel(dest_ref, y_ref, h_ref, gate_ref, gfin_ref, outp_ref, outs_ref, ybuf, sem, *, tm, n_first):
    def issue(r, carry):
        for kk in range(TOP_K):
            d = pl.multiple_of(dest_ref[0, 0, kk * tm + r] * ROW_TILE, ROW_TILE)
            pltpu.make_async_copy(y_ref.at[pl.ds(d, ROW_TILE)],
                                  ybuf.at[kk, pl.ds(pl.multiple_of(r * ROW_TILE, ROW_TILE), ROW_TILE)], sem).start()
        return carry

    lax.fori_loop(0, tm, issue, 0, unroll=4)
    for kk in range(TOP_K):
        pltpu.make_async_copy(y_ref.at[pl.ds(0, tm * ROW_TILE)], ybuf.at[kk], sem).wait()

    gate = gate_ref[...]
    parts = []
    for j in range(ROW_TILE):
        rows = pl.ds(j, tm, stride=ROW_TILE)
        part = gate[:, 0:1] * ybuf.at[0][rows, :]
        for kk in range(1, TOP_K):
            part = part + gate[:, kk:kk + 1] * ybuf.at[kk][rows, :]
        parts.append(part)
    h = h_ref[...] + jnp.concatenate(parts, axis=-1)
    res = h * lax.rsqrt(jnp.mean(h * h, axis=-1, keepdims=True) + EPS) * gfin_ref[...]

    @pl.when(pl.program_id(0) < n_first)
    def _():
        outp_ref[...] = res

    @pl.when(pl.program_id(0) >= n_first)
    def _():
        outs_ref[...] = res


def _combine(y, dest_tiles, h, gate_tm, g_final, t_first, tm, interpret):
    T = h.shape[0]
    n_first = t_first // tm
    outp_spec, outs_spec = _dual_rows(n_first, D_MODEL, tm)
    return pl.pallas_call(
        functools.partial(_combine_kernel, tm=tm, n_first=n_first),
        grid=(T // tm,),
        in_specs=[pl.BlockSpec((1, 1, tm * TOP_K), lambda i: (i, 0, 0), memory_space=pltpu.SMEM),
                  pl.BlockSpec(memory_space=pl.ANY),
                  pl.BlockSpec((tm, D_MODEL), lambda i: (i, 0)),
                  pl.BlockSpec((tm, 2 * TOP_K), lambda i: (i, 0)),
                  pl.BlockSpec((1, D_MODEL), lambda i: (0, 0))],
        out_specs=[outp_spec, outs_spec],
        out_shape=[jax.ShapeDtypeStruct((t_first, D_MODEL), F32),
                   jax.ShapeDtypeStruct((T - t_first, D_MODEL), F32)],
        scratch_shapes=[pltpu.VMEM((TOP_K, tm * ROW_TILE, LANES), F32), pltpu.SemaphoreType.DMA],
        compiler_params=_cparams(("arbitrary",)),
        name="moe_combine",
        interpret=interpret,
    )(dest_tiles, y, h, gate_tm, g_final.reshape(1, D_MODEL))


def _routing_tables(idx_rows, counts, T, tm):
    tk = T * TOP_K
    top_idx = idx_rows[:TOP_K]
    rank = idx_rows[TOP_K:]
    counts = counts.astype(jnp.int32)
    blocks_e = (counts + MOE_BLOCK - 1) // MOE_BLOCK
    cum_blocks = jnp.cumsum(blocks_e)
    start_row = (cum_blocks - blocks_e) * MOE_BLOCK
    experts = jnp.arange(N_EXPERTS, dtype=jnp.int32)
    slot_of = jnp.sum(jnp.where(top_idx[..., None] == experts, start_row, 0), axis=-1) + rank
    dest_tiles = slot_of.reshape(TOP_K, T // tm, tm).transpose(1, 0, 2).reshape(T // tm, 1, TOP_K * tm)
    n_blocks = -(-tk // MOE_BLOCK) + N_EXPERTS
    n_slots = n_blocks * MOE_BLOCK
    pad_len = jnp.concatenate([blocks_e * MOE_BLOCK - counts, n_slots - cum_blocks[-1:] * MOE_BLOCK])
    pad_base = jnp.concatenate([start_row + counts, cum_blocks[-1:] * MOE_BLOCK])
    pad_end = jnp.cumsum(pad_len)
    i = jnp.arange(n_slots - tk, dtype=jnp.int32)
    seg = i[:, None] >= pad_end[None, :]
    shift = jnp.concatenate([pad_base[:1], pad_base[1:] - pad_end[:-1]])
    incr = jnp.concatenate([shift[1:] - shift[:-1], jnp.zeros((1,), jnp.int32)])
    pad_slot = i + shift[0] + jnp.sum(jnp.where(seg, incr, 0), axis=-1)
    source = jnp.argsort(jnp.concatenate([slot_of.reshape(-1), pad_slot])).astype(jnp.int32)
    src_tok = jnp.where(source < tk, source % T, 0).reshape(n_blocks, 1, MOE_BLOCK)
    blocks = jnp.arange(n_blocks, dtype=jnp.int32)
    block_e = jnp.sum((blocks[:, None] >= cum_blocks[None, :]).astype(jnp.int32), axis=-1)
    last_e = jnp.sum((cum_blocks[-1] - 1 >= cum_blocks).astype(jnp.int32))
    block_e = jnp.minimum(block_e, last_e).astype(jnp.int32)
    return dest_tiles, src_tok, block_e, cum_blocks[-1:].astype(jnp.int32)


def _layer(xp, xs, S, g_mix, w_in, conv_w, conv_b, b_gates, g_head, w_out, g_ffn, w_router, b_router,
           w_gate_up, b_gate_up, w_down, b_down, g_final, interpret=False):
    T = xp.shape[0] + xs.shape[0]
    B = T // S

    w_in_p = jnp.pad(w_in, ((0, 0), (0, IN_PAD - w_in.shape[1]))).astype(BF16)
    cw_p = jnp.pad(conv_w, ((0, SUBLANES - CONV_W), (0, 0)))
    cb = conv_b.reshape(1, 2 * ML_WIDTH)
    bg_p = jnp.pad(b_gates, (0, LANES - N_GATES)).reshape(1, LANES)
    (q1, k1, v1, q4, k4, v4, q16, k16, v16, qm, kt, vm, om, gates_t) = _inproj(
        xp, xs, S, g_mix, w_in_p, cw_p, cb, bg_p, interpret)

    as_phase = lambda a: a.reshape(B, 1, S, ATT_WIDTH)
    o1, l1 = _attn_branch(as_phase(q1), as_phase(k1), as_phase(v1), 1, interpret)
    o4, l4 = _attn_branch(q4, k4, v4, 4, interpret)
    o16, l16 = _attn_branch(q16, k16, v16, 16, interpret)

    hf, hb = _mlstm(qm, kt, vm, gates_t, B, S, interpret)

    w_rt = w_router.T.astype(BF16)
    b_r = jnp.broadcast_to(b_router[:, None], (N_EXPERTS, LANES))
    h, hn, idx_rows, gate_rows, counts = _outproj(
        xp, xs, S, o1.reshape(T, ATT_WIDTH), o4, o16, l1.reshape(T, LANES), l4, l16,
        hf.reshape(T, ML_WIDTH), hb.reshape(T, ML_WIDTH), om, g_head, w_out.astype(BF16), g_ffn,
        w_rt, b_r, interpret)

    tm = 256
    dest_tiles, src_tok, block_e, n_used = _routing_tables(idx_rows, counts[:, 0], T, tm)
    y = _experts(hn, src_tok, block_e, n_used, w_gate_up, b_gate_up, w_down, b_down, interpret)
    return _combine(y, dest_tiles, h, gate_rows.T, g_final, xp.shape[0], tm, interpret)


def kernel(x_prompt, x_sample, g_mix, w_in, conv_w, conv_b, b_gates, g_head, w_out, g_ffn, w_router,
           b_router, w_gate_up, b_gate_up, w_down, b_down, g_final):
    S = x_prompt.shape[1]
    yp, ys = _layer(x_prompt.reshape(-1, D_MODEL), x_sample.reshape(-1, D_MODEL), S,
                    g_mix, w_in, conv_w, conv_b, b_gates, g_head, w_out, g_ffn, w_router, b_router,
                    w_gate_up, b_gate_up, w_down, b_down, g_final)
    return (yp.reshape(x_prompt.shape), ys.reshape(x_sample.shape))
```

```python
import functools

import jax
import jax.numpy as jnp
from jax import lax
from jax.experimental import pallas as pl
from jax.experimental.pallas import tpu as pltpu

F32 = jnp.float32
BF16 = jnp.bfloat16

D_MODEL = 1024
ATT_HEADS = 8
ATT_HEAD_DIM = 64
ATT_WIDTH = ATT_HEADS * ATT_HEAD_DIM
DILATIONS = (1, 4, 16)
ATT_HALF = 64
ML_HEADS = 4
ML_HEAD_DIM = 128
ML_WIDTH = ML_HEADS * ML_HEAD_DIM
ML_CHUNK = 128
CONV_W = 5
N_GATES = 2 * 2 * ML_HEADS
N_EXPERTS = 32
TOP_K = 4
D_FF = 1024
SWIGLU_LIMIT = 7.0
SWIGLU_ALPHA = 1.702
MOE_BLOCK = 512
EPS = 1e-6
NEG = -1e30

LANES = 128
SUBLANES = 8
IN_PAD = 7 * 512 + LANES
TOK_TILE = 512
ROW_TILE = D_MODEL // LANES
VMEM_LIMIT = 56 * 1024 * 1024


def _cparams(sem):
    return pltpu.CompilerParams(dimension_semantics=sem, vmem_limit_bytes=VMEM_LIMIT)


def _dual_rows(n_first, width, rows=TOK_TILE):
    first = pl.BlockSpec((rows, width), lambda i: (jnp.minimum(i, n_first - 1), 0))
    second = pl.BlockSpec((rows, width), lambda i: (jnp.maximum(i - n_first, 0), 0))
    return first, second


def _inproj_kernel(xp_ref, xs_ref, pp_ref, pn_ref, sp_ref, sn_ref, g_ref, w_ref, cw_ref, cb_ref, bg_ref,
                   q1_ref, k1_ref, v1_ref, q4_ref, k4_ref, v4_ref, q16_ref, k16_ref, v16_ref,
                   qm_ref, kt_ref, vm_ref, om_ref, gt_ref,
                   ext, stage_q, stage_k, stage_v, *, n_first, tiles_per_seq):
    i = pl.program_id(0)
    first = i < n_first
    ti = lax.rem(i, tiles_per_seq)
    tm = TOK_TILE

    def norm(v):
        vn = v * lax.rsqrt(jnp.mean(v * v, axis=-1, keepdims=True) + EPS)
        return (vn * g_ref[...]).astype(BF16)

    xn = norm(jnp.where(first, xp_ref[...], xs_ref[...]))
    prev = jnp.where(ti > 0, jnp.where(first, pp_ref[0], sp_ref[0]), 0.0)
    nxt = jnp.where(ti < tiles_per_seq - 1, jnp.where(first, pn_ref[0], sn_ref[0]), 0.0)
    xn_halo = norm(jnp.concatenate([prev, nxt], axis=0))

    def proj(lo, hi, lhs=xn):
        return jnp.dot(lhs, w_ref[:, lo:hi], preferred_element_type=F32)

    def emit_attn(col, scale, stage, r1, r4, r16):
        p = proj(col, col + ATT_WIDTH)
        if scale is not None:
            p = p * scale
        r1[...] = p.astype(BF16)
        for j in range(ATT_WIDTH // LANES):
            stage[j] = p[:, j * LANES:(j + 1) * LANES]
        for d, ref in ((4, r4), (16, r16)):
            for ph in range(d):
                for j in range(ATT_WIDTH // LANES):
                    ref[ph, :, j * LANES:(j + 1) * LANES] = (
                        stage.at[j][pl.ds(ph, tm // d, stride=d), :].astype(BF16))

    emit_attn(0, ATT_HEAD_DIM ** -0.5, stage_q, q1_ref, q4_ref, q16_ref)
    emit_attn(512, None, stage_k, k1_ref, k4_ref, k16_ref)
    emit_attn(1024, None, stage_v, v1_ref, v4_ref, v16_ref)

    halo_qk = proj(1536, 2560, xn_halo)
    ext[0:SUBLANES, :] = halo_qk[0:SUBLANES]
    ext[SUBLANES:SUBLANES + tm, :] = proj(1536, 2560)
    ext[SUBLANES + tm:, :] = halo_qk[SUBLANES:]
    base = SUBLANES - CONV_W // 2
    acc = cb_ref[...] + ext[base:base + tm, :] * cw_ref[0:1, :]
    for j in range(1, CONV_W):
        acc = acc + ext[base + j:base + j + tm, :] * cw_ref[j:j + 1, :]
    act = acc * jax.nn.sigmoid(acc)
    qm_ref[...] = (act[:, :ML_WIDTH] * (ML_HEAD_DIM ** -0.5)).astype(BF16)
    for c in range(tm // ML_CHUNK):
        kt_ref[c] = act[c * ML_CHUNK:(c + 1) * ML_CHUNK, ML_WIDTH:].T.astype(BF16)

    vm_ref[...] = proj(2560, 3072).astype(BF16)
    om_ref[...] = proj(3072, 3584)
    gates = proj(3584, IN_PAD) + bg_ref[...]
    for c in range(tm // ML_CHUNK):
        gt_ref[c] = gates[c * ML_CHUNK:(c + 1) * ML_CHUNK, :].T[0:N_GATES, :]


def _inproj(xp, xs, S, g_mix, w_in_p, cw_p, cb, bg_p, interpret):
    tm = TOK_TILE
    n_first = xp.shape[0] // tm
    T = xp.shape[0] + xs.shape[0]
    B = T // S
    tps = S // tm
    per = tm // SUBLANES
    row = lambda w: pl.BlockSpec((tm, w), lambda i: (i, 0))
    const = lambda shape: pl.BlockSpec(shape, lambda i: (0,) * len(shape))
    xp_spec, xs_spec = _dual_rows(n_first, D_MODEL)

    def halo(n_tiles, shift, offset):
        last = n_tiles * per - 1
        return pl.BlockSpec(
            (1, SUBLANES, D_MODEL),
            lambda i: (jnp.clip((jnp.clip(i - shift, 0, n_tiles - 1)) * per + offset, 0, last), 0, 0))

    n_second = xs.shape[0] // tm
    xp3 = xp.reshape(xp.shape[0] // SUBLANES, SUBLANES, D_MODEL)
    xs3 = xs.reshape(xs.shape[0] // SUBLANES, SUBLANES, D_MODEL)
    phase = lambda d, w: pl.BlockSpec((None, d, tm // d, w), lambda i: (i // tps, 0, i % tps, 0))
    bf = lambda shape: jax.ShapeDtypeStruct(shape, BF16)
    nat = bf((T, ATT_WIDTH))
    ph4 = bf((B, 4, S // 4, ATT_WIDTH))
    ph16 = bf((B, 16, S // 16, ATT_WIDTH))
    return pl.pallas_call(
        functools.partial(_inproj_kernel, n_first=n_first, tiles_per_seq=tps),
        grid=(T // tm,),
        in_specs=[xp_spec, xs_spec,
                  halo(n_first, 0, -1), halo(n_first, 0, per),
                  halo(n_second, n_first, -1), halo(n_second, n_first, per),
                  const((1, D_MODEL)), const((D_MODEL, IN_PAD)),
                  const((SUBLANES, 2 * ML_WIDTH)), const((1, 2 * ML_WIDTH)), const((1, LANES))],
        out_specs=[row(512), row(512), row(512),
                   phase(4, 512), phase(4, 512), phase(4, 512),
                   phase(16, 512), phase(16, 512), phase(16, 512),
                   row(512),
                   pl.BlockSpec((None, tm // ML_CHUNK, ML_WIDTH, ML_CHUNK), lambda i: (i // tps, i % tps, 0, 0)),
                   row(512), row(512),
                   pl.BlockSpec((None, tm // ML_CHUNK, N_GATES, ML_CHUNK), lambda i: (i // tps, i % tps, 0, 0))],
        out_shape=[nat, nat, nat, ph4, ph4, ph4, ph16, ph16, ph16,
                   bf((T, ML_WIDTH)),
                   bf((B, S // ML_CHUNK, ML_WIDTH, ML_CHUNK)),
                   bf((T, ML_WIDTH)),
                   jax.ShapeDtypeStruct((T, ML_WIDTH), F32),
                   jax.ShapeDtypeStruct((B, S // ML_CHUNK, N_GATES, ML_CHUNK), F32)],
        scratch_shapes=[pltpu.VMEM((tm + 2 * SUBLANES, 2 * ML_WIDTH), F32),
                        pltpu.VMEM((ATT_WIDTH // LANES, tm, LANES), F32),
                        pltpu.VMEM((ATT_WIDTH // LANES, tm, LANES), F32),
                        pltpu.VMEM((ATT_WIDTH // LANES, tm, LANES), F32)],
        compiler_params=_cparams(("arbitrary",)),
        name="inproj",
        interpret=interpret,
    )(xp, xs, xp3, xp3, xs3, xs3, g_mix.reshape(1, D_MODEL), w_in_p, cw_p, cb, bg_p)


def _attn_kernel(q_ref, k_ref, v_ref, o_ref, lse_ref, *, dil, qb, kb, m_len, n_sub, n_ph):
    lane = lax.broadcasted_iota(jnp.int32, (1, LANES), 1)
    lo = lane < ATT_HEAD_DIM
    zero = jnp.zeros((), BF16)
    row = lax.broadcasted_iota(jnp.int32, (qb, kb), 0)
    col = lax.broadcasted_iota(jnp.int32, (qb, kb), 1)
    for ph in range(n_ph):
        for sub in range(n_sub):
            blk = pl.program_id(2) * n_sub + sub
            rows = slice(sub * qb, (sub + 1) * qb)
            start = jnp.clip(blk * qb - ATT_HALF, 0, m_len - kb)
            start = pl.multiple_of(start, ATT_HALF)
            off = start - blk * qb
            q = q_ref[ph, rows, :]
            k = k_ref[ph, pl.ds(start, kb), :]
            v = v_ref[ph, pl.ds(start, kb), :]
            absd = jnp.abs(col - row + off).astype(F32)
            valid = absd <= float(ATT_HALF)
            lse_tile = jnp.zeros((qb, LANES), F32)
            for pair in range(ATT_HEADS // 2):
                sl = slice(pair * LANES, (pair + 1) * LANES)
                qp, kp, vp = q[:, sl], k[:, sl], v[:, sl]
                outs = []
                for hh in range(2):
                    h = 2 * pair + hh
                    slope = 2.0 ** (-(8.0 / ATT_HEADS) * (h + 1))
                    qh = jnp.where(lo if hh == 0 else jnp.logical_not(lo), qp, zero)
                    s = lax.dot_general(qh, kp, (((1,), (1,)), ((), ())), preferred_element_type=F32)
                    s = jnp.where(valid, s - absd * (slope * dil), NEG)
                    mx = jnp.max(s, axis=-1, keepdims=True)
                    p = jnp.exp(s - mx)
                    l = jnp.sum(p, axis=-1, keepdims=True)
                    o = jnp.dot(p.astype(BF16), vp, preferred_element_type=F32)
                    outs.append(o / l)
                    lse_tile = jnp.where(lane == h, mx + jnp.log(l), lse_tile)
                o_ref[ph, rows, sl] = jnp.where(lo, outs[0], outs[1]).astype(BF16)
            lse_ref[ph, rows, :] = lse_tile


def _attn_branch(q, k, v, dil, interpret):
    B, _, m_len, _ = q.shape
    qb = 128
    per_step = 8
    n_sub = min(per_step, m_len // qb)
    n_ph = min(per_step // n_sub, dil)
    kb = min(qb + 2 * ATT_HALF, m_len)
    qspec = pl.BlockSpec((None, n_ph, n_sub * qb, ATT_WIDTH), lambda b, p, i: (b, p, i, 0))
    kvspec = pl.BlockSpec((None, n_ph, m_len, ATT_WIDTH), lambda b, p, i: (b, p, 0, 0))
    return pl.pallas_call(
        functools.partial(_attn_kernel, dil=dil, qb=qb, kb=kb, m_len=m_len, n_sub=n_sub, n_ph=n_ph),
        grid=(B, dil // n_ph, m_len // (n_sub * qb)),
        in_specs=[qspec, kvspec, kvspec],
        out_specs=[qspec, pl.BlockSpec((None, n_ph, n_sub * qb, LANES), lambda b, p, i: (b, p, i, 0))],
        out_shape=[jax.ShapeDtypeStruct((B, dil, m_len, ATT_WIDTH), BF16),
                   jax.ShapeDtypeStruct((B, dil, m_len, LANES), F32)],
        compiler_params=_cparams(("arbitrary", "arbitrary", "arbitrary")),
        name=f"attn_d{dil}",
        interpret=interpret,
    )(q, k, v)


def _mlstm_kernel(qf_ref, ktf_ref, vf_ref, gf_ref, qb_ref, ktb_ref, vb_ref, gb_ref,
                  hf_ref, hb_ref, cn_st, m_st):
    L = ML_CHUNK
    hd = ML_HEAD_DIM

    @pl.when(pl.program_id(1) == 0)
    def _():
        cn_st[...] = jnp.zeros_like(cn_st)
        m_st[...] = jnp.full_like(m_st, NEG)

    t_idx = lax.broadcasted_iota(jnp.int32, (L, L), 0)
    s_idx = lax.broadcasted_iota(jnp.int32, (L, L), 1)
    ones_col = (lax.broadcasted_iota(jnp.int32, (L, LANES), 1) == 0).astype(BF16)
    streams = ((qf_ref, ktf_ref, vf_ref, gf_ref, hf_ref), (qb_ref, ktb_ref, vb_ref, gb_ref, hb_ref))

    prep = []
    for direction in range(2):
        tri = (s_idx <= t_idx) if direction == 0 else (s_idx >= t_idx)
        last = L - 1 if direction == 0 else 0
        rows8 = slice(direction * SUBLANES, (direction + 1) * SUBLANES)
        g8 = streams[direction][3][rows8, :]
        seen_by = (t_idx <= s_idx) if direction == 0 else (t_idx >= s_idx)
        b8 = jnp.dot(jax.nn.log_sigmoid(g8), seen_by.astype(F32), preferred_element_type=F32,
                     precision=lax.Precision.HIGHEST)
        b8 = pltpu.roll(b8, ML_HEADS, axis=0)
        nc8 = g8 - b8
        m_prev8 = m_st[rows8, :]
        b_end8 = b8[:, last:last + 1]
        a8 = b_end8 + nc8
        m_loc8 = jnp.max(a8, axis=-1, keepdims=True)
        w8 = jnp.exp(a8 - m_loc8)
        m_new8 = jnp.maximum(b_end8 + m_prev8[:, 0:1], m_loc8)
        sp8 = jnp.exp(b_end8 + m_prev8[:, 0:1] - m_new8)
        sl8 = jnp.exp(m_loc8 - m_new8)
        m_st[rows8, :] = jnp.broadcast_to(m_new8, (SUBLANES, LANES))
        b_cols = jnp.concatenate([b8, jnp.zeros((LANES - SUBLANES, L), F32)], axis=0).T
        prep.append((tri, nc8, m_prev8, w8, sp8, sl8, b_cols))

    outs = ([], [])
    for h in range(ML_HEADS):
        hs = slice(h * hd, (h + 1) * hd)
        for direction in range(2):
            q_ref, kt_ref, v_ref, _, _ = streams[direction]
            tri, nc8, m_prev8, w8, sp8, sl8, b_cols = prep[direction]
            st = direction * ML_HEADS + h
            q = q_ref[:, hs]
            kt = kt_ref[hs, :]
            v_aug = jnp.concatenate([v_ref[:, hs], ones_col], axis=-1)
            cn_prev = cn_st[st]
            m_prev = m_prev8[h:h + 1, 0:1]

            seen = jnp.where(tri, nc8[h:h + 1, :], NEG)
            mm = jnp.maximum(jnp.max(seen, axis=-1, keepdims=True), m_prev)
            dw = jnp.exp(seen - mm)
            iw = jnp.exp(m_prev - mm)
            qk_s = jnp.dot(q, kt, preferred_element_type=F32) * dw
            intra = jnp.dot(qk_s.astype(BF16), v_aug, preferred_element_type=F32)
            carried = jnp.dot(q, cn_prev.astype(BF16), preferred_element_type=F32)
            num = intra[:, :hd] + iw * carried[:, :hd]
            den = intra[:, hd:hd + 1] + iw * carried[:, hd:hd + 1]
            floor = jnp.exp(-(b_cols[:, h:h + 1] + mm))
            outs[direction].append(num / jnp.maximum(jnp.abs(den), floor))

            kw = (kt.astype(F32) * w8[h:h + 1, :]).astype(BF16)
            cn_loc = jnp.dot(kw, v_aug, preferred_element_type=F32)
            cn_st[st] = sp8[h:h + 1, :] * cn_prev + sl8[h:h + 1, :] * cn_loc
    hf_ref[...] = jnp.concatenate(outs[0], axis=-1)
    hb_ref[...] = jnp.concatenate(outs[1], axis=-1)


def _mlstm(qm, kt, vm, gates_t, B, S, interpret):
    L = ML_CHUNK
    nc = S // L
    fwd = lambda s: s
    bwd = lambda s: nc - 1 - s

    def specs(cidx):
        blk = lambda w: pl.BlockSpec((None, L, w), lambda b, s: (b, cidx(s), 0))
        chunk = lambda r: pl.BlockSpec((None, None, r, L), lambda b, s: (b, cidx(s), 0, 0))
        return [blk(ML_WIDTH), chunk(ML_WIDTH), blk(ML_WIDTH), chunk(N_GATES)]

    q3 = qm.reshape(B, S, ML_WIDTH)
    v3 = vm.reshape(B, S, ML_WIDTH)
    hblk = lambda cidx: pl.BlockSpec((None, L, ML_WIDTH), lambda b, s: (b, cidx(s), 0))
    return pl.pallas_call(
        _mlstm_kernel,
        grid=(B, nc),
        in_specs=specs(fwd) + specs(bwd),
        out_specs=[hblk(fwd), hblk(bwd)],
        out_shape=[jax.ShapeDtypeStruct((B, S, ML_WIDTH), F32),
                   jax.ShapeDtypeStruct((B, S, ML_WIDTH), F32)],
        scratch_shapes=[pltpu.VMEM((2 * ML_HEADS, ML_HEAD_DIM, 2 * ML_HEAD_DIM), F32),
                        pltpu.VMEM((2 * SUBLANES, LANES), F32)],
        compiler_params=_cparams(("arbitrary", "arbitrary")),
        name="mlstm",
        interpret=interpret,
    )(q3, kt, v3, gates_t, q3, kt, v3, gates_t)


def _outproj_kernel(xp_ref, xs_ref, o1_ref, o4_ref, o16_ref, l1_ref, l4_ref, l16_ref,
                    hf_ref, hb_ref, om_ref, gh_ref, wo_ref, gf_ref, wrt_ref, br_ref,
                    h_ref, hn_ref, idx_ref, gate_ref, cnt_ref,
                    carry, nat4, nat16, lnat4, lnat16, *, n_first):
    tm = TOK_TILE
    step = pl.program_id(0)

    @pl.when(step == 0)
    def _():
        carry[...] = jnp.zeros_like(carry)

    for d, o_ref, l_ref, nat, lnat in ((4, o4_ref, l4_ref, nat4, lnat4), (16, o16_ref, l16_ref, nat16, lnat16)):
        for ph in range(d):
            rows = pl.ds(ph, tm // d, stride=d)
            lnat[rows, :] = l_ref[ph]
            for j in range(ATT_WIDTH // LANES):
                nat.at[j][rows, :] = o_ref[ph, :, j * LANES:(j + 1) * LANES].astype(F32)

    l1, l2, l3 = l1_ref[...], lnat4[...], lnat16[...]
    mx = jnp.maximum(jnp.maximum(l1, l2), l3)
    e1, e2, e3 = jnp.exp(l1 - mx), jnp.exp(l2 - mx), jnp.exp(l3 - mx)
    inv = 1.0 / (e1 + e2 + e3)
    er = lax.broadcasted_iota(jnp.int32, (LANES, ATT_WIDTH), 0)
    ec = lax.broadcasted_iota(jnp.int32, (LANES, ATT_WIDTH), 1)
    expand = (ec // ATT_HEAD_DIM == er).astype(BF16)

    def widen(w):
        hi = w.astype(BF16)
        lo = (w - hi.astype(F32)).astype(BF16)
        return (jnp.dot(hi, expand, preferred_element_type=F32)
                + jnp.dot(lo, expand, preferred_element_type=F32))

    slabs = lambda nat: jnp.concatenate([nat[j] for j in range(ATT_WIDTH // LANES)], axis=-1)
    attn = (widen(e1 * inv) * o1_ref[...].astype(F32)
            + widen(e2 * inv) * slabs(nat4)
            + widen(e3 * inv) * slabs(nat16))

    hsum = hf_ref[...] + hb_ref[...]
    normed = []
    for hh in range(ML_HEADS):
        hv = hsum[:, hh * ML_HEAD_DIM:(hh + 1) * ML_HEAD_DIM]
        normed.append(hv * lax.rsqrt(jnp.mean(hv * hv, axis=-1, keepdims=True) + EPS))
    ml = jax.nn.sigmoid(om_ref[...]) * (jnp.concatenate(normed, axis=-1) * gh_ref[...])

    mix = jnp.concatenate([attn.astype(BF16), ml.astype(BF16)], axis=-1)
    x = jnp.where(step < n_first, xp_ref[...], xs_ref[...])
    h = x + jnp.dot(mix, wo_ref[...], preferred_element_type=F32)
    h_ref[...] = h
    hn = h * lax.rsqrt(jnp.mean(h * h, axis=-1, keepdims=True) + EPS) * gf_ref[...]
    for j in range(D_MODEL // LANES):
        hn_ref[pl.ds(j, tm, stride=SUBLANES), :] = hn[:, j * LANES:(j + 1) * LANES]

    logits = lax.dot_general(wrt_ref[...], hn.astype(BF16), (((1,), (1,)), ((), ())),
                             preferred_element_type=F32) + br_ref[:, 0:1]
    eid = lax.broadcasted_iota(jnp.int32, (N_EXPERTS, tm), 0)
    work = logits
    vals, idxs, hots = [], [], []
    for _ in range(TOP_K):
        top = jnp.max(work, axis=0, keepdims=True)
        idx = jnp.min(jnp.where(work == top, eid, N_EXPERTS), axis=0, keepdims=True)
        hot = eid == idx
        work = jnp.where(hot, -jnp.inf, work)
        vals.append(top)
        idxs.append(idx)
        hots.append(hot)
    exps = [jnp.exp(v - vals[0]) for v in vals]
    inv_den = 1.0 / (exps[0] + exps[1] + exps[2] + exps[3])

    cnt = jnp.where(hots[0] | hots[1] | hots[2] | hots[3], 1.0, 0.0)
    r_idx = lax.broadcasted_iota(jnp.int32, (tm, tm), 0)
    c_idx = lax.broadcasted_iota(jnp.int32, (tm, tm), 1)
    before = (r_idx < c_idx).astype(BF16)
    rank_all = jnp.dot(cnt.astype(BF16), before, preferred_element_type=F32) + carry[:, 0:1]
    carry[...] = carry[...] + jnp.sum(cnt, axis=1, keepdims=True)
    cnt_ref[...] = carry[...]

    ranks = [jnp.sum(jnp.where(hots[kk], rank_all, 0.0), axis=0, keepdims=True).astype(jnp.int32)
             for kk in range(TOP_K)]
    idx_ref[...] = jnp.concatenate(idxs + ranks, axis=0)
    gate_ref[...] = jnp.concatenate([e * inv_den for e in exps] + [jnp.zeros((TOP_K, tm), F32)], axis=0)


def _outproj(xp, xs, S, o1, o4, o16, l1, l4, l16, hf, hb, om, g_head, w_out_b, g_ffn, w_rt, b_r, interpret):
    tm = TOK_TILE
    n_first = xp.shape[0] // tm
    T = xp.shape[0] + xs.shape[0]
    tps = S // tm
    row = lambda w: pl.BlockSpec((tm, w), lambda i: (i, 0))
    col = lambda r: pl.BlockSpec((r, tm), lambda i: (0, i))
    const = lambda shape: pl.BlockSpec(shape, lambda i: (0,) * len(shape))
    phase = lambda d, w: pl.BlockSpec((None, d, tm // d, w), lambda i: (i // tps, 0, i % tps, 0))
    xp_spec, xs_spec = _dual_rows(n_first, D_MODEL)
    return pl.pallas_call(
        functools.partial(_outproj_kernel, n_first=n_first),
        grid=(T // tm,),
        in_specs=[xp_spec, xs_spec, row(512), phase(4, 512), phase(16, 512),
                  row(LANES), phase(4, LANES), phase(16, LANES),
                  row(512), row(512), row(512), const((1, ML_WIDTH)),
                  const((D_MODEL, D_MODEL)), const((1, D_MODEL)),
                  const((N_EXPERTS, D_MODEL)), const((N_EXPERTS, LANES))],
        out_specs=[row(D_MODEL), pl.BlockSpec((tm * ROW_TILE, LANES), lambda i: (i, 0)),
                   col(2 * TOP_K), col(2 * TOP_K), const((N_EXPERTS, LANES))],
        out_shape=[jax.ShapeDtypeStruct((T, D_MODEL), F32),
                   jax.ShapeDtypeStruct((T * ROW_TILE, LANES), F32),
                   jax.ShapeDtypeStruct((2 * TOP_K, T), jnp.int32),
                   jax.ShapeDtypeStruct((2 * TOP_K, T), F32),
                   jax.ShapeDtypeStruct((N_EXPERTS, LANES), F32)],
        scratch_shapes=[pltpu.VMEM((N_EXPERTS, LANES), F32),
                        pltpu.VMEM((ATT_WIDTH // LANES, tm, LANES), F32),
                        pltpu.VMEM((ATT_WIDTH // LANES, tm, LANES), F32),
                        pltpu.VMEM((tm, LANES), F32), pltpu.VMEM((tm, LANES), F32)],
        compiler_params=_cparams(("arbitrary",)),
        name="outproj_router",
        interpret=interpret,
    )(xp, xs, o1, o4, o16, l1, l4, l16, hf, hb, om, g_head.reshape(1, ML_WIDTH), w_out_b,
      g_ffn.reshape(1, D_MODEL), w_rt, b_r)


def _expert_kernel(be_ref, nu_ref, src_cur, src_next, hn_ref, wgu_ref, bgu_ref, wd_ref, bd_ref, y_ref,
                   wgu_b, wd_b, xbuf, gsem):
    j = pl.program_id(0)
    slot = lax.rem(j, 2)
    other = 1 - slot
    rows = MOE_BLOCK * ROW_TILE
    active = j < nu_ref[0]

    def gather_wait(s):
        pltpu.make_async_copy(hn_ref.at[pl.ds(0, rows)], xbuf.at[s], gsem.at[s]).wait()

    def gather_row(tab, r, s):
        pltpu.make_async_copy(hn_ref.at[pl.ds(pl.multiple_of(tab[0, 0, r], ROW_TILE), ROW_TILE)],
                              xbuf.at[s, pl.ds(r * ROW_TILE, ROW_TILE)], gsem.at[s]).start()

    @pl.when(j == 0)
    def _():
        def first(r, carry):
            gather_row(src_cur, r, 0)
            return carry

        lax.fori_loop(0, MOE_BLOCK, first, 0, unroll=8)

    @pl.when(jnp.logical_and(active, jnp.logical_or(j == 0, be_ref[j] != be_ref[jnp.maximum(j - 1, 0)])))
    def _():
        wgu_b[...] = wgu_ref[0].astype(BF16)
        wd_b[...] = wd_ref[0].astype(BF16)

    def step(cur, nxt):
        gather_wait(cur)
        for r in range(MOE_BLOCK):
            gather_row(src_next, r, nxt)

        xv = xbuf.at[cur]
        x = jnp.concatenate([xv[pl.ds(c, MOE_BLOCK, stride=ROW_TILE), :] for c in range(ROW_TILE)],
                            axis=-1).astype(BF16)
        gu = jnp.dot(x, wgu_b[...], preferred_element_type=F32) + bgu_ref[0]
        g = jnp.minimum(gu[:, :D_FF], SWIGLU_LIMIT)
        u = jnp.clip(gu[:, D_FF:], -SWIGLU_LIMIT, SWIGLU_LIMIT)
        hdn = (u + 1.0) * (g * jax.nn.sigmoid(SWIGLU_ALPHA * g))
        y = jnp.dot(hdn.astype(BF16), wd_b[...], preferred_element_type=F32) + bd_ref[0]
        for c in range(ROW_TILE):
            y_ref[pl.ds(c, MOE_BLOCK, stride=ROW_TILE), :] = y[:, c * LANES:(c + 1) * LANES]

    for parity in range(2):
        @pl.when(jnp.logical_and(active, slot == parity))
        def _():
            step(parity, 1 - parity)

    @pl.when(j == nu_ref[0] - 1)
    def _():
        gather_wait(other)

    @pl.when(jnp.logical_not(active))
    def _():
        y_ref[...] = jnp.zeros_like(y_ref)


def _experts(hn_rt, src_tab, block_e, n_used, w_gate_up, b_gate_up, w_down, b_down, interpret):
    nb = src_tab.shape[0]
    tab = lambda f: pl.BlockSpec((1, 1, MOE_BLOCK), lambda j, be, nu: (f(j, nu), 0, 0), memory_space=pltpu.SMEM)
    grid_spec = pltpu.PrefetchScalarGridSpec(
        num_scalar_prefetch=2,
        grid=(nb,),
        in_specs=[tab(lambda j, nu: j), tab(lambda j, nu: jnp.minimum(j + 1, nu[0] - 1)),
                  pl.BlockSpec(memory_space=pl.ANY),
                  pl.BlockSpec((1, D_MODEL, 2 * D_FF), lambda j, be, nu: (be[j], 0, 0)),
                  pl.BlockSpec((1, 1, 2 * D_FF), lambda j, be, nu: (be[j], 0, 0)),
                  pl.BlockSpec((1, D_FF, D_MODEL), lambda j, be, nu: (be[j], 0, 0)),
                  pl.BlockSpec((1, 1, D_MODEL), lambda j, be, nu: (be[j], 0, 0))],
        out_specs=pl.BlockSpec((MOE_BLOCK * ROW_TILE, LANES), lambda j, be, nu: (j, 0)),
        scratch_shapes=[pltpu.VMEM((D_MODEL, 2 * D_FF), BF16),
                        pltpu.VMEM((D_FF, D_MODEL), BF16),
                        pltpu.VMEM((2, MOE_BLOCK * ROW_TILE, LANES), F32),
                        pltpu.SemaphoreType.DMA((2,))],
    )
    return pl.pallas_call(
        _expert_kernel,
        grid_spec=grid_spec,
        out_shape=jax.ShapeDtypeStruct((nb * MOE_BLOCK * ROW_TILE, LANES), F32),
        compiler_params=_cparams(("arbitrary",)),
        name="moe_experts",
        interpret=interpret,
    )(block_e, n_used, src_tab, src_tab, hn_rt, w_gate_up,
      b_gate_up.reshape(N_EXPERTS, 1, 2 * D_FF), w_down, b_down.reshape(N_EXPERTS, 1, D_MODEL))


def _combine_kernel(dest_ref, y_ref, h_ref, gate_ref, gfin_ref, outp_ref, outs_ref, ybuf, sem, *, tm, n_first):
    def issue(r, carry):
        for kk in range(TOP_K):
            d = pl.multiple_of(dest_ref[0, 0, kk * tm + r] * ROW_TILE, ROW_TILE)
            pltpu.make_async_copy(y_ref.at[pl.ds(d, ROW_TILE)],
                                  ybuf.at[kk, pl.ds(pl.multiple_of(r * ROW_TILE, ROW_TILE), ROW_TILE)], sem).start()
        return carry

    lax.fori_loop(0, tm, issue, 0, unroll=4)
    for kk in range(TOP_K):
        pltpu.make_async_copy(y_ref.at[pl.ds(0, tm * ROW_TILE)], ybuf.at[kk], sem).wait()

    gate = gate_ref[...]
    parts = []
    for j in range(ROW_TILE):
        rows = pl.ds(j, tm, stride=ROW_TILE)
        part = gate[:, 0:1] * ybuf.at[0][rows, :]
        for kk in range(1, TOP_K):
            part = part + gate[:, kk:kk + 1] * ybuf.at[kk][rows, :]
        parts.append(part)
    h = h_ref[...] + jnp.concatenate(parts, axis=-1)
    res = h * lax.rsqrt(jnp.mean(h * h, axis=-1, keepdims=True) + EPS) * gfin_ref[...]

    @pl.when(pl.program_id(0) < n_first)
    def _():
        outp_ref[...] = res

    @pl.when(pl.program_id(0) >= n_first)
    def _():
        outs_ref[...] = res


def _combine(y, dest_tiles, h, gate_tm, g_final, t_first, tm, interpret):
    T = h.shape[0]
    n_first = t_first // tm
    outp_spec, outs_spec = _dual_rows(n_first, D_MODEL, tm)
    return pl.pallas_call(
        functools.partial(_combine_kernel, tm=tm, n_first=n_first),
        grid=(T // tm,),
        in_specs=[pl.BlockSpec((1, 1, tm * TOP_K), lambda i: (i, 0, 0), memory_space=pltpu.SMEM),
                  pl.BlockSpec(memory_space=pl.ANY),
                  pl.BlockSpec((tm, D_MODEL), lambda i: (i, 0)),
                  pl.BlockSpec((tm, 2 * TOP_K), lambda i: (i, 0)),
                  pl.BlockSpec((1, D_MODEL), lambda i: (0, 0))],
        out_specs=[outp_spec, outs_spec],
        out_shape=[jax.ShapeDtypeStruct((t_first, D_MODEL), F32),
                   jax.ShapeDtypeStruct((T - t_first, D_MODEL), F32)],
        scratch_shapes=[pltpu.VMEM((TOP_K, tm * ROW_TILE, LANES), F32), pltpu.SemaphoreType.DMA],
        compiler_params=_cparams(("arbitrary",)),
        name="moe_combine",
        interpret=interpret,
    )(dest_tiles, y, h, gate_tm, g_final.reshape(1, D_MODEL))


def _routing_tables(idx_rows, counts, T, tm):
    tk = T * TOP_K
    top_idx = idx_rows[:TOP_K]
    rank = idx_rows[TOP_K:]
    counts = counts.astype(jnp.int32)
    blocks_e = (counts + MOE_BLOCK - 1) // MOE_BLOCK
    cum_blocks = jnp.cumsum(blocks_e)
    start_row = (cum_blocks - blocks_e) * MOE_BLOCK
    experts = jnp.arange(N_EXPERTS, dtype=jnp.int32)
    slot_of = jnp.sum(jnp.where(top_idx[..., None] == experts, start_row, 0), axis=-1) + rank
    dest_tiles = slot_of.reshape(TOP_K, T // tm, tm).transpose(1, 0, 2).reshape(T // tm, 1, TOP_K * tm)
    n_blocks = -(-tk // MOE_BLOCK) + N_EXPERTS
    n_slots = n_blocks * MOE_BLOCK
    pad_len = jnp.concatenate([blocks_e * MOE_BLOCK - counts, n_slots - cum_blocks[-1:] * MOE_BLOCK])
    pad_base = jnp.concatenate([start_row + counts, cum_blocks[-1:] * MOE_BLOCK])
    pad_end = jnp.cumsum(pad_len)
    i = jnp.arange(n_slots - tk, dtype=jnp.int32)
    seg = i[:, None] >= pad_end[None, :]
    shift = jnp.concatenate([pad_base[:1], pad_base[1:] - pad_end[:-1]])
    incr = jnp.concatenate([shift[1:] - shift[:-1], jnp.zeros((1,), jnp.int32)])
    pad_slot = i + shift[0] + jnp.sum(jnp.where(seg, incr, 0), axis=-1)
    source = jnp.argsort(jnp.concatenate([slot_of.reshape(-1), pad_slot])).astype(jnp.int32)
    src_row = (jnp.where(source < tk, source % T, 0) * ROW_TILE).reshape(n_blocks, 1, MOE_BLOCK)
    blocks = jnp.arange(n_blocks, dtype=jnp.int32)
    block_e = jnp.sum((blocks[:, None] >= cum_blocks[None, :]).astype(jnp.int32), axis=-1)
    last_e = jnp.sum((cum_blocks[-1] - 1 >= cum_blocks).astype(jnp.int32))
    block_e = jnp.minimum(block_e, last_e).astype(jnp.int32)
    return dest_tiles, src_row, block_e, cum_blocks[-1:].astype(jnp.int32)


def _layer(xp, xs, S, g_mix, w_in, conv_w, conv_b, b_gates, g_head, w_out, g_ffn, w_router, b_router,
           w_gate_up, b_gate_up, w_down, b_down, g_final, interpret=False):
    T = xp.shape[0] + xs.shape[0]
    B = T // S

    w_in_p = jnp.pad(w_in, ((0, 0), (0, IN_PAD - w_in.shape[1]))).astype(BF16)
    cw_p = jnp.pad(conv_w, ((0, SUBLANES - CONV_W), (0, 0)))
    cb = conv_b.reshape(1, 2 * ML_WIDTH)
    bg_p = jnp.pad(b_gates, (0, LANES - N_GATES)).reshape(1, LANES)
    (q1, k1, v1, q4, k4, v4, q16, k16, v16, qm, kt, vm, om, gates_t) = _inproj(
        xp, xs, S, g_mix, w_in_p, cw_p, cb, bg_p, interpret)

    as_phase = lambda a: a.reshape(B, 1, S, ATT_WIDTH)
    o1, l1 = _attn_branch(as_phase(q1), as_phase(k1), as_phase(v1), 1, interpret)
    o4, l4 = _attn_branch(q4, k4, v4, 4, interpret)
    o16, l16 = _attn_branch(q16, k16, v16, 16, interpret)

    hf, hb = _mlstm(qm, kt, vm, gates_t, B, S, interpret)

    w_rt = w_router.T.astype(BF16)
    b_r = jnp.broadcast_to(b_router[:, None], (N_EXPERTS, LANES))
    h, hn, idx_rows, gate_rows, counts = _outproj(
        xp, xs, S, o1.reshape(T, ATT_WIDTH), o4, o16, l1.reshape(T, LANES), l4, l16,
        hf.reshape(T, ML_WIDTH), hb.reshape(T, ML_WIDTH), om, g_head, w_out.astype(BF16), g_ffn,
        w_rt, b_r, interpret)

    tm = 256
    dest_tiles, src_row, block_e, n_used = _routing_tables(idx_rows, counts[:, 0], T, tm)
    y = _experts(hn, src_row, block_e, n_used, w_gate_up, b_gate_up, w_down, b_down, interpret)
    return _combine(y, dest_tiles, h, gate_rows.T, g_final, xp.shape[0], tm, interpret)


def kernel(x_prompt, x_sample, g_mix, w_in, conv_w, conv_b, b_gates, g_head, w_out, g_ffn, w_router,
           b_router, w_gate_up, b_gate_up, w_down, b_down, g_final):
    S = x_prompt.shape[1]
    yp, ys = _layer(x_prompt.reshape(-1, D_MODEL), x_sample.reshape(-1, D_MODEL), S,
                    g_mix, w_in, conv_w, conv_b, b_gates, g_head, w_out, g_ffn, w_router, b_router,
                    w_gate_up, b_gate_up, w_down, b_down, g_final)
    return (yp.reshape(x_prompt.shape), ys.reshape(x_sample.shape))
```

```python
import functools

import jax
import jax.numpy as jnp
from jax import lax
from jax.experimental import pallas as pl
from jax.experimental.pallas import tpu as pltpu

F32 = jnp.float32
BF16 = jnp.bfloat16

D_MODEL = 1024
ATT_HEADS = 8
ATT_HEAD_DIM = 64
ATT_WIDTH = ATT_HEADS * ATT_HEAD_DIM
DILATIONS = (1, 4, 16)
ATT_HALF = 64
ML_HEADS = 4
ML_HEAD_DIM = 128
ML_WIDTH = ML_HEADS * ML_HEAD_DIM
ML_CHUNK = 128
CONV_W = 5
N_GATES = 2 * 2 * ML_HEADS
N_EXPERTS = 32
TOP_K = 4
D_FF = 1024
SWIGLU_LIMIT = 7.0
SWIGLU_ALPHA = 1.702
MOE_BLOCK = 512
EPS = 1e-6
NEG = -1e30

LANES = 128
SUBLANES = 8
IN_PAD = 7 * 512 + LANES
TOK_TILE = 512
ROW_TILE = D_MODEL // LANES
VMEM_LIMIT = 56 * 1024 * 1024


def _cparams(sem):
    return pltpu.CompilerParams(dimension_semantics=sem, vmem_limit_bytes=VMEM_LIMIT)


def _dual_rows(n_first, width, rows=TOK_TILE):
    first = pl.BlockSpec((rows, width), lambda i: (jnp.minimum(i, n_first - 1), 0))
    second = pl.BlockSpec((rows, width), lambda i: (jnp.maximum(i - n_first, 0), 0))
    return first, second


def _inproj_kernel(xp_ref, xs_ref, pp_ref, pn_ref, sp_ref, sn_ref, g_ref, w_ref, cw_ref, cb_ref, bg_ref,
                   q1_ref, k1_ref, v1_ref, q4_ref, k4_ref, v4_ref, q16_ref, k16_ref, v16_ref,
                   qm_ref, kt_ref, vm_ref, om_ref, gt_ref,
                   ext, stage_q, stage_k, stage_v, *, n_first, tiles_per_seq):
    i = pl.program_id(0)
    first = i < n_first
    ti = lax.rem(i, tiles_per_seq)
    tm = TOK_TILE

    def norm(v):
        vn = v * lax.rsqrt(jnp.mean(v * v, axis=-1, keepdims=True) + EPS)
        return (vn * g_ref[...]).astype(BF16)

    xn = norm(jnp.where(first, xp_ref[...], xs_ref[...]))
    prev = jnp.where(ti > 0, jnp.where(first, pp_ref[0], sp_ref[0]), 0.0)
    nxt = jnp.where(ti < tiles_per_seq - 1, jnp.where(first, pn_ref[0], sn_ref[0]), 0.0)
    xn_halo = norm(jnp.concatenate([prev, nxt], axis=0))

    def proj(lo, hi, lhs=xn):
        return jnp.dot(lhs, w_ref[:, lo:hi], preferred_element_type=F32)

    def emit_attn(col, scale, stage, r1, r4, r16):
        p = proj(col, col + ATT_WIDTH)
        if scale is not None:
            p = p * scale
        r1[...] = p.astype(BF16)
        for j in range(ATT_WIDTH // LANES):
            stage[j] = p[:, j * LANES:(j + 1) * LANES]
        for d, ref in ((4, r4), (16, r16)):
            for ph in range(d):
                for j in range(ATT_WIDTH // LANES):
                    ref[ph, :, j * LANES:(j + 1) * LANES] = (
                        stage.at[j][pl.ds(ph, tm // d, stride=d), :].astype(BF16))

    emit_attn(0, ATT_HEAD_DIM ** -0.5, stage_q, q1_ref, q4_ref, q16_ref)
    emit_attn(512, None, stage_k, k1_ref, k4_ref, k16_ref)
    emit_attn(1024, None, stage_v, v1_ref, v4_ref, v16_ref)

    halo_qk = proj(1536, 2560, xn_halo)
    ext[0:SUBLANES, :] = halo_qk[0:SUBLANES]
    ext[SUBLANES:SUBLANES + tm, :] = proj(1536, 2560)
    ext[SUBLANES + tm:, :] = halo_qk[SUBLANES:]
    base = SUBLANES - CONV_W // 2
    acc = cb_ref[...] + ext[base:base + tm, :] * cw_ref[0:1, :]
    for j in range(1, CONV_W):
        acc = acc + ext[base + j:base + j + tm, :] * cw_ref[j:j + 1, :]
    act = acc * jax.nn.sigmoid(acc)
    qm_ref[...] = (act[:, :ML_WIDTH] * (ML_HEAD_DIM ** -0.5)).astype(BF16)
    for c in range(tm // ML_CHUNK):
        kt_ref[c] = act[c * ML_CHUNK:(c + 1) * ML_CHUNK, ML_WIDTH:].T.astype(BF16)

    vm_ref[...] = proj(2560, 3072).astype(BF16)
    om_ref[...] = proj(3072, 3584)
    gates = proj(3584, IN_PAD) + bg_ref[...]
    for c in range(tm // ML_CHUNK):
        gt_ref[c] = gates[c * ML_CHUNK:(c + 1) * ML_CHUNK, :].T[0:N_GATES, :]


def _inproj(xp, xs, S, g_mix, w_in_p, cw_p, cb, bg_p, interpret):
    tm = TOK_TILE
    n_first = xp.shape[0] // tm
    T = xp.shape[0] + xs.shape[0]
    B = T // S
    tps = S // tm
    per = tm // SUBLANES
    row = lambda w: pl.BlockSpec((tm, w), lambda i: (i, 0))
    const = lambda shape: pl.BlockSpec(shape, lambda i: (0,) * len(shape))
    xp_spec, xs_spec = _dual_rows(n_first, D_MODEL)

    def halo(n_tiles, shift, offset):
        last = n_tiles * per - 1
        return pl.BlockSpec(
            (1, SUBLANES, D_MODEL),
            lambda i: (jnp.clip((jnp.clip(i - shift, 0, n_tiles - 1)) * per + offset, 0, last), 0, 0))

    n_second = xs.shape[0] // tm
    xp3 = xp.reshape(xp.shape[0] // SUBLANES, SUBLANES, D_MODEL)
    xs3 = xs.reshape(xs.shape[0] // SUBLANES, SUBLANES, D_MODEL)
    phase = lambda d, w: pl.BlockSpec((None, d, tm // d, w), lambda i: (i // tps, 0, i % tps, 0))
    bf = lambda shape: jax.ShapeDtypeStruct(shape, BF16)
    nat = bf((T, ATT_WIDTH))
    ph4 = bf((B, 4, S // 4, ATT_WIDTH))
    ph16 = bf((B, 16, S // 16, ATT_WIDTH))
    return pl.pallas_call(
        functools.partial(_inproj_kernel, n_first=n_first, tiles_per_seq=tps),
        grid=(T // tm,),
        in_specs=[xp_spec, xs_spec,
                  halo(n_first, 0, -1), halo(n_first, 0, per),
                  halo(n_second, n_first, -1), halo(n_second, n_first, per),
                  const((1, D_MODEL)), const((D_MODEL, IN_PAD)),
                  const((SUBLANES, 2 * ML_WIDTH)), const((1, 2 * ML_WIDTH)), const((1, LANES))],
        out_specs=[row(512), row(512), row(512),
                   phase(4, 512), phase(4, 512), phase(4, 512),
                   phase(16, 512), phase(16, 512), phase(16, 512),
                   row(512),
                   pl.BlockSpec((None, tm // ML_CHUNK, ML_WIDTH, ML_CHUNK), lambda i: (i // tps, i % tps, 0, 0)),
                   row(512), row(512),
                   pl.BlockSpec((None, tm // ML_CHUNK, N_GATES, ML_CHUNK), lambda i: (i // tps, i % tps, 0, 0))],
        out_shape=[nat, nat, nat, ph4, ph4, ph4, ph16, ph16, ph16,
                   bf((T, ML_WIDTH)),
                   bf((B, S // ML_CHUNK, ML_WIDTH, ML_CHUNK)),
                   bf((T, ML_WIDTH)),
                   jax.ShapeDtypeStruct((T, ML_WIDTH), F32),
                   jax.ShapeDtypeStruct((B, S // ML_CHUNK, N_GATES, ML_CHUNK), F32)],
        scratch_shapes=[pltpu.VMEM((tm + 2 * SUBLANES, 2 * ML_WIDTH), F32),
                        pltpu.VMEM((ATT_WIDTH // LANES, tm, LANES), F32),
                        pltpu.VMEM((ATT_WIDTH // LANES, tm, LANES), F32),
                        pltpu.VMEM((ATT_WIDTH // LANES, tm, LANES), F32)],
        compiler_params=_cparams(("arbitrary",)),
        name="inproj",
        interpret=interpret,
    )(xp, xs, xp3, xp3, xs3, xs3, g_mix.reshape(1, D_MODEL), w_in_p, cw_p, cb, bg_p)


def _attn_kernel(q_ref, k_ref, v_ref, o_ref, lse_ref, *, dil, qb, kb, m_len, n_sub, n_ph):
    lane = lax.broadcasted_iota(jnp.int32, (1, LANES), 1)
    lo = lane < ATT_HEAD_DIM
    zero = jnp.zeros((), BF16)
    row = lax.broadcasted_iota(jnp.int32, (qb, kb), 0)
    col = lax.broadcasted_iota(jnp.int32, (qb, kb), 1)
    for ph in range(n_ph):
        for sub in range(n_sub):
            blk = pl.program_id(2) * n_sub + sub
            rows = slice(sub * qb, (sub + 1) * qb)
            start = jnp.clip(blk * qb - ATT_HALF, 0, m_len - kb)
            start = pl.multiple_of(start, ATT_HALF)
            off = start - blk * qb
            q = q_ref[ph, rows, :]
            k = k_ref[ph, pl.ds(start, kb), :]
            v = v_ref[ph, pl.ds(start, kb), :]
            absd = jnp.abs(col - row + off).astype(F32)
            valid = absd <= float(ATT_HALF)
            lse_tile = jnp.zeros((qb, LANES), F32)
            for pair in range(ATT_HEADS // 2):
                sl = slice(pair * LANES, (pair + 1) * LANES)
                qp, kp, vp = q[:, sl], k[:, sl], v[:, sl]
                outs = []
                for hh in range(2):
                    h = 2 * pair + hh
                    slope = 2.0 ** (-(8.0 / ATT_HEADS) * (h + 1))
                    qh = jnp.where(lo if hh == 0 else jnp.logical_not(lo), qp, zero)
                    s = lax.dot_general(qh, kp, (((1,), (1,)), ((), ())), preferred_element_type=F32)
                    s = jnp.where(valid, s - absd * (slope * dil), NEG)
                    mx = jnp.max(s, axis=-1, keepdims=True)
                    p = jnp.exp(s - mx)
                    l = jnp.sum(p, axis=-1, keepdims=True)
                    o = jnp.dot(p.astype(BF16), vp, preferred_element_type=F32)
                    outs.append(o / l)
                    lse_tile = jnp.where(lane == h, mx + jnp.log(l), lse_tile)
                o_ref[ph, rows, sl] = jnp.where(lo, outs[0], outs[1]).astype(BF16)
            lse_ref[ph, rows, :] = lse_tile


def _attn_branch(q, k, v, dil, interpret):
    B, _, m_len, _ = q.shape
    qb = 128
    per_step = 8
    n_sub = min(per_step, m_len // qb)
    n_ph = min(per_step // n_sub, dil)
    kb = min(qb + 2 * ATT_HALF, m_len)
    qspec = pl.BlockSpec((None, n_ph, n_sub * qb, ATT_WIDTH), lambda b, p, i: (b, p, i, 0))
    kvspec = pl.BlockSpec((None, n_ph, m_len, ATT_WIDTH), lambda b, p, i: (b, p, 0, 0))
    return pl.pallas_call(
        functools.partial(_attn_kernel, dil=dil, qb=qb, kb=kb, m_len=m_len, n_sub=n_sub, n_ph=n_ph),
        grid=(B, dil // n_ph, m_len // (n_sub * qb)),
        in_specs=[qspec, kvspec, kvspec],
        out_specs=[qspec, pl.BlockSpec((None, n_ph, n_sub * qb, LANES), lambda b, p, i: (b, p, i, 0))],
        out_shape=[jax.ShapeDtypeStruct((B, dil, m_len, ATT_WIDTH), BF16),
                   jax.ShapeDtypeStruct((B, dil, m_len, LANES), F32)],
        compiler_params=_cparams(("arbitrary", "arbitrary", "arbitrary")),
        name=f"attn_d{dil}",
        interpret=interpret,
    )(q, k, v)


def _mlstm_kernel(qf_ref, ktf_ref, vf_ref, gf_ref, qb_ref, ktb_ref, vb_ref, gb_ref,
                  hf_ref, hb_ref, cn_st, m_st):
    L = ML_CHUNK
    hd = ML_HEAD_DIM

    @pl.when(pl.program_id(1) == 0)
    def _():
        cn_st[...] = jnp.zeros_like(cn_st)
        m_st[...] = jnp.full_like(m_st, NEG)

    t_idx = lax.broadcasted_iota(jnp.int32, (L, L), 0)
    s_idx = lax.broadcasted_iota(jnp.int32, (L, L), 1)
    ones_col = (lax.broadcasted_iota(jnp.int32, (L, LANES), 1) == 0).astype(BF16)
    streams = ((qf_ref, ktf_ref, vf_ref, gf_ref, hf_ref), (qb_ref, ktb_ref, vb_ref, gb_ref, hb_ref))

    prep = []
    for direction in range(2):
        tri = (s_idx <= t_idx) if direction == 0 else (s_idx >= t_idx)
        last = L - 1 if direction == 0 else 0
        rows8 = slice(direction * SUBLANES, (direction + 1) * SUBLANES)
        g8 = streams[direction][3][rows8, :]
        seen_by = (t_idx <= s_idx) if direction == 0 else (t_idx >= s_idx)
        b8 = jnp.dot(jax.nn.log_sigmoid(g8), seen_by.astype(F32), preferred_element_type=F32,
                     precision=lax.Precision.HIGHEST)
        b8 = pltpu.roll(b8, ML_HEADS, axis=0)
        nc8 = g8 - b8
        m_prev8 = m_st[rows8, :]
        b_end8 = b8[:, last:last + 1]
        a8 = b_end8 + nc8
        m_loc8 = jnp.max(a8, axis=-1, keepdims=True)
        w8 = jnp.exp(a8 - m_loc8)
        m_new8 = jnp.maximum(b_end8 + m_prev8[:, 0:1], m_loc8)
        sp8 = jnp.exp(b_end8 + m_prev8[:, 0:1] - m_new8)
        sl8 = jnp.exp(m_loc8 - m_new8)
        m_st[rows8, :] = jnp.broadcast_to(m_new8, (SUBLANES, LANES))
        b_cols = jnp.concatenate([b8, jnp.zeros((LANES - SUBLANES, L), F32)], axis=0).T
        prep.append((tri, nc8, m_prev8, w8, sp8, sl8, b_cols))

    outs = ([], [])
    for h in range(ML_HEADS):
        hs = slice(h * hd, (h + 1) * hd)
        for direction in range(2):
            q_ref, kt_ref, v_ref, _, _ = streams[direction]
            tri, nc8, m_prev8, w8, sp8, sl8, b_cols = prep[direction]
            st = direction * ML_HEADS + h
            q = q_ref[:, hs]
            kt = kt_ref[hs, :]
            v_aug = jnp.concatenate([v_ref[:, hs], ones_col], axis=-1)
            cn_prev = cn_st[st]
            m_prev = m_prev8[h:h + 1, 0:1]

            seen = jnp.where(tri, nc8[h:h + 1, :], NEG)
            mm = jnp.maximum(jnp.max(seen, axis=-1, keepdims=True), m_prev)
            dw = jnp.exp(seen - mm)
            iw = jnp.exp(m_prev - mm)
            qk_s = jnp.dot(q, kt, preferred_element_type=F32) * dw
            intra = jnp.dot(qk_s.astype(BF16), v_aug, preferred_element_type=F32)
            carried = jnp.dot(q, cn_prev.astype(BF16), preferred_element_type=F32)
            num = intra[:, :hd] + iw * carried[:, :hd]
            den = intra[:, hd:hd + 1] + iw * carried[:, hd:hd + 1]
            floor = jnp.exp(-(b_cols[:, h:h + 1] + mm))
            outs[direction].append(num / jnp.maximum(jnp.abs(den), floor))

            kw = (kt.astype(F32) * w8[h:h + 1, :]).astype(BF16)
            cn_loc = jnp.dot(kw, v_aug, preferred_element_type=F32)
            cn_st[st] = sp8[h:h + 1, :] * cn_prev + sl8[h:h + 1, :] * cn_loc
    hf_ref[...] = jnp.concatenate(outs[0], axis=-1)
    hb_ref[...] = jnp.concatenate(outs[1], axis=-1)


def _mlstm(qm, kt, vm, gates_t, B, S, interpret):
    L = ML_CHUNK
    nc = S // L
    fwd = lambda s: s
    bwd = lambda s: nc - 1 - s

    def specs(cidx):
        blk = lambda w: pl.BlockSpec((None, L, w), lambda b, s: (b, cidx(s), 0))
        chunk = lambda r: pl.BlockSpec((None, None, r, L), lambda b, s: (b, cidx(s), 0, 0))
        return [blk(ML_WIDTH), chunk(ML_WIDTH), blk(ML_WIDTH), chunk(N_GATES)]

    q3 = qm.reshape(B, S, ML_WIDTH)
    v3 = vm.reshape(B, S, ML_WIDTH)
    hblk = lambda cidx: pl.BlockSpec((None, L, ML_WIDTH), lambda b, s: (b, cidx(s), 0))
    return pl.pallas_call(
        _mlstm_kernel,
        grid=(B, nc),
        in_specs=specs(fwd) + specs(bwd),
        out_specs=[hblk(fwd), hblk(bwd)],
        out_shape=[jax.ShapeDtypeStruct((B, S, ML_WIDTH), F32),
                   jax.ShapeDtypeStruct((B, S, ML_WIDTH), F32)],
        scratch_shapes=[pltpu.VMEM((2 * ML_HEADS, ML_HEAD_DIM, 2 * ML_HEAD_DIM), F32),
                        pltpu.VMEM((2 * SUBLANES, LANES), F32)],
        compiler_params=_cparams(("arbitrary", "arbitrary")),
        name="mlstm",
        interpret=interpret,
    )(q3, kt, v3, gates_t, q3, kt, v3, gates_t)


def _outproj_kernel(xp_ref, xs_ref, o1_ref, o4_ref, o16_ref, l1_ref, l4_ref, l16_ref,
                    hf_ref, hb_ref, om_ref, gh_ref, wo_ref, gf_ref, wrt_ref, br_ref,
                    h_ref, hn_ref, idx_ref, gate_ref, cnt_ref,
                    carry, nat4, nat16, lnat4, lnat16, *, n_first):
    tm = TOK_TILE
    step = pl.program_id(0)

    @pl.when(step == 0)
    def _():
        carry[...] = jnp.zeros_like(carry)

    for d, o_ref, l_ref, nat, lnat in ((4, o4_ref, l4_ref, nat4, lnat4), (16, o16_ref, l16_ref, nat16, lnat16)):
        for ph in range(d):
            rows = pl.ds(ph, tm // d, stride=d)
            lnat[rows, :] = l_ref[ph]
            for j in range(ATT_WIDTH // LANES):
                nat.at[j][rows, :] = o_ref[ph, :, j * LANES:(j + 1) * LANES].astype(F32)

    l1, l2, l3 = l1_ref[...], lnat4[...], lnat16[...]
    mx = jnp.maximum(jnp.maximum(l1, l2), l3)
    e1, e2, e3 = jnp.exp(l1 - mx), jnp.exp(l2 - mx), jnp.exp(l3 - mx)
    inv = 1.0 / (e1 + e2 + e3)
    er = lax.broadcasted_iota(jnp.int32, (LANES, ATT_WIDTH), 0)
    ec = lax.broadcasted_iota(jnp.int32, (LANES, ATT_WIDTH), 1)
    expand = (ec // ATT_HEAD_DIM == er).astype(BF16)

    def widen(w):
        hi = w.astype(BF16)
        lo = (w - hi.astype(F32)).astype(BF16)
        return (jnp.dot(hi, expand, preferred_element_type=F32)
                + jnp.dot(lo, expand, preferred_element_type=F32))

    slabs = lambda nat: jnp.concatenate([nat[j] for j in range(ATT_WIDTH // LANES)], axis=-1)
    attn = (widen(e1 * inv) * o1_ref[...].astype(F32)
            + widen(e2 * inv) * slabs(nat4)
            + widen(e3 * inv) * slabs(nat16))

    hsum = hf_ref[...] + hb_ref[...]
    normed = []
    for hh in range(ML_HEADS):
        hv = hsum[:, hh * ML_HEAD_DIM:(hh + 1) * ML_HEAD_DIM]
        normed.append(hv * lax.rsqrt(jnp.mean(hv * hv, axis=-1, keepdims=True) + EPS))
    ml = jax.nn.sigmoid(om_ref[...]) * (jnp.concatenate(normed, axis=-1) * gh_ref[...])

    mix = jnp.concatenate([attn.astype(BF16), ml.astype(BF16)], axis=-1)
    x = jnp.where(step < n_first, xp_ref[...], xs_ref[...])
    h = x + jnp.dot(mix, wo_ref[...], preferred_element_type=F32)
    h_ref[...] = h
    hn = h * lax.rsqrt(jnp.mean(h * h, axis=-1, keepdims=True) + EPS) * gf_ref[...]
    for j in range(D_MODEL // LANES):
        hn_ref[pl.ds(j, tm, stride=SUBLANES), :] = hn[:, j * LANES:(j + 1) * LANES]

    logits = lax.dot_general(wrt_ref[...], hn.astype(BF16), (((1,), (1,)), ((), ())),
                             preferred_element_type=F32) + br_ref[:, 0:1]
    eid = lax.broadcasted_iota(jnp.int32, (N_EXPERTS, tm), 0)
    work = logits
    vals, idxs, hots = [], [], []
    for _ in range(TOP_K):
        top = jnp.max(work, axis=0, keepdims=True)
        idx = jnp.min(jnp.where(work == top, eid, N_EXPERTS), axis=0, keepdims=True)
        hot = eid == idx
        work = jnp.where(hot, -jnp.inf, work)
        vals.append(top)
        idxs.append(idx)
        hots.append(hot)
    exps = [jnp.exp(v - vals[0]) for v in vals]
    inv_den = 1.0 / (exps[0] + exps[1] + exps[2] + exps[3])

    cnt = jnp.where(hots[0] | hots[1] | hots[2] | hots[3], 1.0, 0.0)
    r_idx = lax.broadcasted_iota(jnp.int32, (tm, tm), 0)
    c_idx = lax.broadcasted_iota(jnp.int32, (tm, tm), 1)
    before = (r_idx < c_idx).astype(BF16)
    rank_all = jnp.dot(cnt.astype(BF16), before, preferred_element_type=F32) + carry[:, 0:1]
    carry[...] = carry[...] + jnp.sum(cnt, axis=1, keepdims=True)
    cnt_ref[...] = carry[...]

    ranks = [jnp.sum(jnp.where(hots[kk], rank_all, 0.0), axis=0, keepdims=True).astype(jnp.int32)
             for kk in range(TOP_K)]
    idx_ref[...] = jnp.concatenate(idxs + ranks, axis=0)
    gate_ref[...] = jnp.concatenate([e * inv_den for e in exps] + [jnp.zeros((TOP_K, tm), F32)], axis=0)


def _outproj(xp, xs, S, o1, o4, o16, l1, l4, l16, hf, hb, om, g_head, w_out_b, g_ffn, w_rt, b_r, interpret):
    tm = TOK_TILE
    n_first = xp.shape[0] // tm
    T = xp.shape[0] + xs.shape[0]
    tps = S // tm
    row = lambda w: pl.BlockSpec((tm, w), lambda i: (i, 0))
    col = lambda r: pl.BlockSpec((r, tm), lambda i: (0, i))
    const = lambda shape: pl.BlockSpec(shape, lambda i: (0,) * len(shape))
    phase = lambda d, w: pl.BlockSpec((None, d, tm // d, w), lambda i: (i // tps, 0, i % tps, 0))
    xp_spec, xs_spec = _dual_rows(n_first, D_MODEL)
    return pl.pallas_call(
        functools.partial(_outproj_kernel, n_first=n_first),
        grid=(T // tm,),
        in_specs=[xp_spec, xs_spec, row(512), phase(4, 512), phase(16, 512),
                  row(LANES), phase(4, LANES), phase(16, LANES),
                  row(512), row(512), row(512), const((1, ML_WIDTH)),
                  const((D_MODEL, D_MODEL)), const((1, D_MODEL)),
                  const((N_EXPERTS, D_MODEL)), const((N_EXPERTS, LANES))],
        out_specs=[row(D_MODEL), pl.BlockSpec((tm * ROW_TILE, LANES), lambda i: (i, 0)),
                   col(2 * TOP_K), col(2 * TOP_K), const((N_EXPERTS, LANES))],
        out_shape=[jax.ShapeDtypeStruct((T, D_MODEL), F32),
                   jax.ShapeDtypeStruct((T * ROW_TILE, LANES), F32),
                   jax.ShapeDtypeStruct((2 * TOP_K, T), jnp.int32),
                   jax.ShapeDtypeStruct((2 * TOP_K, T), F32),
                   jax.ShapeDtypeStruct((N_EXPERTS, LANES), F32)],
        scratch_shapes=[pltpu.VMEM((N_EXPERTS, LANES), F32),
                        pltpu.VMEM((ATT_WIDTH // LANES, tm, LANES), F32),
                        pltpu.VMEM((ATT_WIDTH // LANES, tm, LANES), F32),
                        pltpu.VMEM((tm, LANES), F32), pltpu.VMEM((tm, LANES), F32)],
        compiler_params=_cparams(("arbitrary",)),
        name="outproj_router",
        interpret=interpret,
    )(xp, xs, o1, o4, o16, l1, l4, l16, hf, hb, om, g_head.reshape(1, ML_WIDTH), w_out_b,
      g_ffn.reshape(1, D_MODEL), w_rt, b_r)


def _expert_kernel(be_ref, nu_ref, src_cur, src_next, hn_ref, wgu_ref, bgu_ref, wd_ref, bd_ref, y_ref,
                   wgu_b, wd_b, xbuf, gsem):
    j = pl.program_id(0)
    slot = lax.rem(j, 2)
    other = 1 - slot
    rows = MOE_BLOCK * ROW_TILE
    active = j < nu_ref[0]

    def gather_wait(s):
        pltpu.make_async_copy(hn_ref.at[pl.ds(0, rows)], xbuf.at[s], gsem.at[s]).wait()

    def gather_row(tab, r, s, queue=0):
        pltpu.make_async_copy(hn_ref.at[pl.ds(pl.multiple_of(tab[0, 0, r], ROW_TILE), ROW_TILE)],
                              xbuf.at[s, pl.ds(r * ROW_TILE, ROW_TILE)], gsem.at[s]).start(priority=queue)

    @pl.when(j == 0)
    def _():
        def first(r, carry):
            gather_row(src_cur, r, 0)
            return carry

        lax.fori_loop(0, MOE_BLOCK, first, 0, unroll=8)

    @pl.when(jnp.logical_and(active, jnp.logical_or(j == 0, be_ref[j] != be_ref[jnp.maximum(j - 1, 0)])))
    def _():
        wgu_b[...] = wgu_ref[0].astype(BF16)
        wd_b[...] = wd_ref[0].astype(BF16)

    def step(cur, nxt):
        gather_wait(cur)
        for r in range(MOE_BLOCK):
            gather_row(src_next, r, nxt, queue=r % 2)

        xv = xbuf.at[cur]
        x = jnp.concatenate([xv[pl.ds(c, MOE_BLOCK, stride=ROW_TILE), :] for c in range(ROW_TILE)],
                            axis=-1).astype(BF16)
        gu = jnp.dot(x, wgu_b[...], preferred_element_type=F32) + bgu_ref[0]
        g = jnp.minimum(gu[:, :D_FF], SWIGLU_LIMIT)
        u = jnp.clip(gu[:, D_FF:], -SWIGLU_LIMIT, SWIGLU_LIMIT)
        hdn = (u + 1.0) * (g * jax.nn.sigmoid(SWIGLU_ALPHA * g))
        y = jnp.dot(hdn.astype(BF16), wd_b[...], preferred_element_type=F32) + bd_ref[0]
        for c in range(ROW_TILE):
            y_ref[pl.ds(c, MOE_BLOCK, stride=ROW_TILE), :] = y[:, c * LANES:(c + 1) * LANES]

    for parity in range(2):
        @pl.when(jnp.logical_and(active, slot == parity))
        def _():
            step(parity, 1 - parity)

    @pl.when(j == nu_ref[0] - 1)
    def _():
        gather_wait(other)

    @pl.when(jnp.logical_not(active))
    def _():
        y_ref[...] = jnp.zeros_like(y_ref)


def _experts(hn_rt, src_tab, block_e, n_used, w_gate_up, b_gate_up, w_down, b_down, interpret):
    nb = src_tab.shape[0]
    tab = lambda f: pl.BlockSpec((1, 1, MOE_BLOCK), lambda j, be, nu: (f(j, nu), 0, 0), memory_space=pltpu.SMEM)
    grid_spec = pltpu.PrefetchScalarGridSpec(
        num_scalar_prefetch=2,
        grid=(nb,),
        in_specs=[tab(lambda j, nu: j), tab(lambda j, nu: jnp.minimum(j + 1, nu[0] - 1)),
                  pl.BlockSpec(memory_space=pl.ANY),
                  pl.BlockSpec((1, D_MODEL, 2 * D_FF), lambda j, be, nu: (be[j], 0, 0)),
                  pl.BlockSpec((1, 1, 2 * D_FF), lambda j, be, nu: (be[j], 0, 0)),
                  pl.BlockSpec((1, D_FF, D_MODEL), lambda j, be, nu: (be[j], 0, 0)),
                  pl.BlockSpec((1, 1, D_MODEL), lambda j, be, nu: (be[j], 0, 0))],
        out_specs=pl.BlockSpec((MOE_BLOCK * ROW_TILE, LANES), lambda j, be, nu: (j, 0)),
        scratch_shapes=[pltpu.VMEM((D_MODEL, 2 * D_FF), BF16),
                        pltpu.VMEM((D_FF, D_MODEL), BF16),
                        pltpu.VMEM((2, MOE_BLOCK * ROW_TILE, LANES), F32),
                        pltpu.SemaphoreType.DMA((2,))],
    )
    return pl.pallas_call(
        _expert_kernel,
        grid_spec=grid_spec,
        out_shape=jax.ShapeDtypeStruct((nb * MOE_BLOCK * ROW_TILE, LANES), F32),
        compiler_params=_cparams(("arbitrary",)),
        name="moe_experts",
        interpret=interpret,
    )(block_e, n_used, src_tab, src_tab, hn_rt, w_gate_up,
      b_gate_up.reshape(N_EXPERTS, 1, 2 * D_FF), w_down, b_down.reshape(N_EXPERTS, 1, D_MODEL))


def _combine_kernel(dest_ref, y_ref, h_ref, gate_ref, gfin_ref, outp_ref, outs_ref, ybuf, sem, *, tm, n_first):
    def issue(r, carry):
        for kk in range(TOP_K):
            d = pl.multiple_of(dest_ref[0, 0, kk * tm + r] * ROW_TILE, ROW_TILE)
            pltpu.make_async_copy(y_ref.at[pl.ds(d, ROW_TILE)],
                                  ybuf.at[kk, pl.ds(pl.multiple_of(r * ROW_TILE, ROW_TILE), ROW_TILE)],
                                  sem).start(priority=kk % 2)
        return carry

    lax.fori_loop(0, tm, issue, 0, unroll=4)
    for kk in range(TOP_K):
        pltpu.make_async_copy(y_ref.at[pl.ds(0, tm * ROW_TILE)], ybuf.at[kk], sem).wait()

    gate = gate_ref[...]
    parts = []
    for j in range(ROW_TILE):
        rows = pl.ds(j, tm, stride=ROW_TILE)
        part = gate[:, 0:1] * ybuf.at[0][rows, :]
        for kk in range(1, TOP_K):
            part = part + gate[:, kk:kk + 1] * ybuf.at[kk][rows, :]
        parts.append(part)
    h = h_ref[...] + jnp.concatenate(parts, axis=-1)
    res = h * lax.rsqrt(jnp.mean(h * h, axis=-1, keepdims=True) + EPS) * gfin_ref[...]

    @pl.when(pl.program_id(0) < n_first)
    def _():
        outp_ref[...] = res

    @pl.when(pl.program_id(0) >= n_first)
    def _():
        outs_ref[...] = res


def _combine(y, dest_tiles, h, gate_tm, g_final, t_first, tm, interpret):
    T = h.shape[0]
    n_first = t_first // tm
    outp_spec, outs_spec = _dual_rows(n_first, D_MODEL, tm)
    return pl.pallas_call(
        functools.partial(_combine_kernel, tm=tm, n_first=n_first),
        grid=(T // tm,),
        in_specs=[pl.BlockSpec((1, 1, tm * TOP_K), lambda i: (i, 0, 0), memory_space=pltpu.SMEM),
                  pl.BlockSpec(memory_space=pl.ANY),
                  pl.BlockSpec((tm, D_MODEL), lambda i: (i, 0)),
                  pl.BlockSpec((tm, 2 * TOP_K), lambda i: (i, 0)),
                  pl.BlockSpec((1, D_MODEL), lambda i: (0, 0))],
        out_specs=[outp_spec, outs_spec],
        out_shape=[jax.ShapeDtypeStruct((t_first, D_MODEL), F32),
                   jax.ShapeDtypeStruct((T - t_first, D_MODEL), F32)],
        scratch_shapes=[pltpu.VMEM((TOP_K, tm * ROW_TILE, LANES), F32), pltpu.SemaphoreType.DMA],
        compiler_params=_cparams(("arbitrary",)),
        name="moe_combine",
        interpret=interpret,
    )(dest_tiles, y, h, gate_tm, g_final.reshape(1, D_MODEL))


def _routing_tables(idx_rows, counts, T, tm):
    tk = T * TOP_K
    top_idx = idx_rows[:TOP_K]
    rank = idx_rows[TOP_K:]
    counts = counts.astype(jnp.int32)
    blocks_e = (counts + MOE_BLOCK - 1) // MOE_BLOCK
    cum_blocks = jnp.cumsum(blocks_e)
    start_row = (cum_blocks - blocks_e) * MOE_BLOCK
    experts = jnp.arange(N_EXPERTS, dtype=jnp.int32)
    slot_of = jnp.sum(jnp.where(top_idx[..., None] == experts, start_row, 0), axis=-1) + rank
    dest_tiles = slot_of.reshape(TOP_K, T // tm, tm).transpose(1, 0, 2).reshape(T // tm, 1, TOP_K * tm)
    n_blocks = -(-tk // MOE_BLOCK) + N_EXPERTS
    n_slots = n_blocks * MOE_BLOCK
    pad_len = jnp.concatenate([blocks_e * MOE_BLOCK - counts, n_slots - cum_blocks[-1:] * MOE_BLOCK])
    pad_base = jnp.concatenate([start_row + counts, cum_blocks[-1:] * MOE_BLOCK])
    pad_end = jnp.cumsum(pad_len)
    i = jnp.arange(n_slots - tk, dtype=jnp.int32)
    seg = i[:, None] >= pad_end[None, :]
    shift = jnp.concatenate([pad_base[:1], pad_base[1:] - pad_end[:-1]])
    incr = jnp.concatenate([shift[1:] - shift[:-1], jnp.zeros((1,), jnp.int32)])
    pad_slot = i + shift[0] + jnp.sum(jnp.where(seg, incr, 0), axis=-1)
    source = jnp.argsort(jnp.concatenate([slot_of.reshape(-1), pad_slot])).astype(jnp.int32)
    src_row = (jnp.where(source < tk, source % T, 0) * ROW_TILE).reshape(n_blocks, 1, MOE_BLOCK)
    blocks = jnp.arange(n_blocks, dtype=jnp.int32)
    block_e = jnp.sum((blocks[:, None] >= cum_blocks[None, :]).astype(jnp.int32), axis=-1)
    last_e = jnp.sum((cum_blocks[-1] - 1 >= cum_blocks).astype(jnp.int32))
    block_e = jnp.minimum(block_e, last_e).astype(jnp.int32)
    return dest_tiles, src_row, block_e, cum_blocks[-1:].astype(jnp.int32)


def _layer(xp, xs, S, g_mix, w_in, conv_w, conv_b, b_gates, g_head, w_out, g_ffn, w_router, b_router,
           w_gate_up, b_gate_up, w_down, b_down, g_final, interpret=False):
    T = xp.shape[0] + xs.shape[0]
    B = T // S

    w_in_p = jnp.pad(w_in, ((0, 0), (0, IN_PAD - w_in.shape[1]))).astype(BF16)
    cw_p = jnp.pad(conv_w, ((0, SUBLANES - CONV_W), (0, 0)))
    cb = conv_b.reshape(1, 2 * ML_WIDTH)
    bg_p = jnp.pad(b_gates, (0, LANES - N_GATES)).reshape(1, LANES)
    (q1, k1, v1, q4, k4, v4, q16, k16, v16, qm, kt, vm, om, gates_t) = _inproj(
        xp, xs, S, g_mix, w_in_p, cw_p, cb, bg_p, interpret)

    as_phase = lambda a: a.reshape(B, 1, S, ATT_WIDTH)
    o1, l1 = _attn_branch(as_phase(q1), as_phase(k1), as_phase(v1), 1, interpret)
    o4, l4 = _attn_branch(q4, k4, v4, 4, interpret)
    o16, l16 = _attn_branch(q16, k16, v16, 16, interpret)

    hf, hb = _mlstm(qm, kt, vm, gates_t, B, S, interpret)

    w_rt = w_router.T.astype(BF16)
    b_r = jnp.broadcast_to(b_router[:, None], (N_EXPERTS, LANES))
    h, hn, idx_rows, gate_rows, counts = _outproj(
        xp, xs, S, o1.reshape(T, ATT_WIDTH), o4, o16, l1.reshape(T, LANES), l4, l16,
        hf.reshape(T, ML_WIDTH), hb.reshape(T, ML_WIDTH), om, g_head, w_out.astype(BF16), g_ffn,
        w_rt, b_r, interpret)

    tm = 256
    dest_tiles, src_row, block_e, n_used = _routing_tables(idx_rows, counts[:, 0], T, tm)
    y = _experts(hn, src_row, block_e, n_used, w_gate_up, b_gate_up, w_down, b_down, interpret)
    return _combine(y, dest_tiles, h, gate_rows.T, g_final, xp.shape[0], tm, interpret)


def kernel(x_prompt, x_sample, g_mix, w_in, conv_w, conv_b, b_gates, g_head, w_out, g_ffn, w_router,
           b_router, w_gate_up, b_gate_up, w_down, b_down, g_final):
    S = x_prompt.shape[1]
    yp, ys = _layer(x_prompt.reshape(-1, D_MODEL), x_sample.reshape(-1, D_MODEL), S,
                    g_mix, w_in, conv_w, conv_b, b_gates, g_head, w_out, g_ffn, w_router, b_router,
                    w_gate_up, b_gate_up, w_down, b_down, g_final)
    return (yp.reshape(x_prompt.shape), ys.reshape(x_sample.shape))
```

```python
import functools

import jax
import jax.numpy as jnp
from jax import lax
from jax.experimental import pallas as pl
from jax.experimental.pallas import tpu as pltpu

F32 = jnp.float32
BF16 = jnp.bfloat16

D_MODEL = 1024
ATT_HEADS = 8
ATT_HEAD_DIM = 64
ATT_WIDTH = ATT_HEADS * ATT_HEAD_DIM
DILATIONS = (1, 4, 16)
ATT_HALF = 64
ML_HEADS = 4
ML_HEAD_DIM = 128
ML_WIDTH = ML_HEADS * ML_HEAD_DIM
ML_CHUNK = 128
CONV_W = 5
N_GATES = 2 * 2 * ML_HEADS
N_EXPERTS = 32
TOP_K = 4
D_FF = 1024
SWIGLU_LIMIT = 7.0
SWIGLU_ALPHA = 1.702
MOE_BLOCK = 512
EPS = 1e-6
NEG = -1e30

LANES = 128
SUBLANES = 8
IN_PAD = 7 * 512 + LANES
TOK_TILE = 512
ROW_TILE = D_MODEL // LANES
VMEM_LIMIT = 56 * 1024 * 1024


def _cparams(sem):
    return pltpu.CompilerParams(dimension_semantics=sem, vmem_limit_bytes=VMEM_LIMIT)


def _dual_rows(n_first, width, rows=TOK_TILE):
    first = pl.BlockSpec((rows, width), lambda i: (jnp.minimum(i, n_first - 1), 0))
    second = pl.BlockSpec((rows, width), lambda i: (jnp.maximum(i - n_first, 0), 0))
    return first, second


def _inproj_kernel(xp_ref, xs_ref, pp_ref, pn_ref, sp_ref, sn_ref, g_ref, w_ref, cw_ref, cb_ref, bg_ref,
                   q1_ref, k1_ref, v1_ref, q4_ref, k4_ref, v4_ref, q16_ref, k16_ref, v16_ref,
                   qm_ref, kt_ref, vm_ref, om_ref, gt_ref,
                   ext, stage_q, stage_k, stage_v, *, n_first, tiles_per_seq):
    i = pl.program_id(0)
    first = i < n_first
    ti = lax.rem(i, tiles_per_seq)
    tm = TOK_TILE

    def norm(v):
        vn = v * lax.rsqrt(jnp.mean(v * v, axis=-1, keepdims=True) + EPS)
        return (vn * g_ref[...]).astype(BF16)

    xn = norm(jnp.where(first, xp_ref[...], xs_ref[...]))
    prev = jnp.where(ti > 0, jnp.where(first, pp_ref[0], sp_ref[0]), 0.0)
    nxt = jnp.where(ti < tiles_per_seq - 1, jnp.where(first, pn_ref[0], sn_ref[0]), 0.0)
    xn_halo = norm(jnp.concatenate([prev, nxt], axis=0))

    def proj(lo, hi, lhs=xn):
        return jnp.dot(lhs, w_ref[:, lo:hi], preferred_element_type=F32)

    def emit_attn(col, scale, stage, r1, r4, r16):
        p = proj(col, col + ATT_WIDTH)
        if scale is not None:
            p = p * scale
        r1[...] = p.astype(BF16)
        for j in range(ATT_WIDTH // LANES):
            stage[j] = p[:, j * LANES:(j + 1) * LANES]
        for d, ref in ((4, r4), (16, r16)):
            for ph in range(d):
                for j in range(ATT_WIDTH // LANES):
                    ref[ph, :, j * LANES:(j + 1) * LANES] = (
                        stage.at[j][pl.ds(ph, tm // d, stride=d), :].astype(BF16))

    emit_attn(0, ATT_HEAD_DIM ** -0.5, stage_q, q1_ref, q4_ref, q16_ref)
    emit_attn(512, None, stage_k, k1_ref, k4_ref, k16_ref)
    emit_attn(1024, None, stage_v, v1_ref, v4_ref, v16_ref)

    halo_qk = proj(1536, 2560, xn_halo)
    ext[0:SUBLANES, :] = halo_qk[0:SUBLANES]
    ext[SUBLANES:SUBLANES + tm, :] = proj(1536, 2560)
    ext[SUBLANES + tm:, :] = halo_qk[SUBLANES:]
    base = SUBLANES - CONV_W // 2
    acc = cb_ref[...] + ext[base:base + tm, :] * cw_ref[0:1, :]
    for j in range(1, CONV_W):
        acc = acc + ext[base + j:base + j + tm, :] * cw_ref[j:j + 1, :]
    act = acc * jax.nn.sigmoid(acc)
    qm_ref[...] = (act[:, :ML_WIDTH] * (ML_HEAD_DIM ** -0.5)).astype(BF16)
    for c in range(tm // ML_CHUNK):
        kt_ref[c] = act[c * ML_CHUNK:(c + 1) * ML_CHUNK, ML_WIDTH:].T.astype(BF16)

    vm_ref[...] = proj(2560, 3072).astype(BF16)
    om_ref[...] = proj(3072, 3584)
    gates = proj(3584, IN_PAD) + bg_ref[...]
    for c in range(tm // ML_CHUNK):
        gt_ref[c] = gates[c * ML_CHUNK:(c + 1) * ML_CHUNK, :].T[0:N_GATES, :]


def _inproj(xp, xs, S, g_mix, w_in_p, cw_p, cb, bg_p, interpret):
    tm = TOK_TILE
    n_first = xp.shape[0] // tm
    T = xp.shape[0] + xs.shape[0]
    B = T // S
    tps = S // tm
    per = tm // SUBLANES
    row = lambda w: pl.BlockSpec((tm, w), lambda i: (i, 0))
    const = lambda shape: pl.BlockSpec(shape, lambda i: (0,) * len(shape))
    xp_spec, xs_spec = _dual_rows(n_first, D_MODEL)

    def halo(n_tiles, shift, offset):
        last = n_tiles * per - 1
        return pl.BlockSpec(
            (1, SUBLANES, D_MODEL),
            lambda i: (jnp.clip((jnp.clip(i - shift, 0, n_tiles - 1)) * per + offset, 0, last), 0, 0))

    n_second = xs.shape[0] // tm
    xp3 = xp.reshape(xp.shape[0] // SUBLANES, SUBLANES, D_MODEL)
    xs3 = xs.reshape(xs.shape[0] // SUBLANES, SUBLANES, D_MODEL)
    phase = lambda d, w: pl.BlockSpec((None, d, tm // d, w), lambda i: (i // tps, 0, i % tps, 0))
    bf = lambda shape: jax.ShapeDtypeStruct(shape, BF16)
    nat = bf((T, ATT_WIDTH))
    ph4 = bf((B, 4, S // 4, ATT_WIDTH))
    ph16 = bf((B, 16, S // 16, ATT_WIDTH))
    return pl.pallas_call(
        functools.partial(_inproj_kernel, n_first=n_first, tiles_per_seq=tps),
        grid=(T // tm,),
        in_specs=[xp_spec, xs_spec,
                  halo(n_first, 0, -1), halo(n_first, 0, per),
                  halo(n_second, n_first, -1), halo(n_second, n_first, per),
                  const((1, D_MODEL)), const((D_MODEL, IN_PAD)),
                  const((SUBLANES, 2 * ML_WIDTH)), const((1, 2 * ML_WIDTH)), const((1, LANES))],
        out_specs=[row(512), row(512), row(512),
                   phase(4, 512), phase(4, 512), phase(4, 512),
                   phase(16, 512), phase(16, 512), phase(16, 512),
                   row(512),
                   pl.BlockSpec((None, tm // ML_CHUNK, ML_WIDTH, ML_CHUNK), lambda i: (i // tps, i % tps, 0, 0)),
                   row(512), row(512),
                   pl.BlockSpec((None, tm // ML_CHUNK, N_GATES, ML_CHUNK), lambda i: (i // tps, i % tps, 0, 0))],
        out_shape=[nat, nat, nat, ph4, ph4, ph4, ph16, ph16, ph16,
                   bf((T, ML_WIDTH)),
                   bf((B, S // ML_CHUNK, ML_WIDTH, ML_CHUNK)),
                   bf((T, ML_WIDTH)),
                   jax.ShapeDtypeStruct((T, ML_WIDTH), F32),
                   jax.ShapeDtypeStruct((B, S // ML_CHUNK, N_GATES, ML_CHUNK), F32)],
        scratch_shapes=[pltpu.VMEM((tm + 2 * SUBLANES, 2 * ML_WIDTH), F32),
                        pltpu.VMEM((ATT_WIDTH // LANES, tm, LANES), F32),
                        pltpu.VMEM((ATT_WIDTH // LANES, tm, LANES), F32),
                        pltpu.VMEM((ATT_WIDTH // LANES, tm, LANES), F32)],
        compiler_params=_cparams(("arbitrary",)),
        name="inproj",
        interpret=interpret,
    )(xp, xs, xp3, xp3, xs3, xs3, g_mix.reshape(1, D_MODEL), w_in_p, cw_p, cb, bg_p)


def _attn_kernel(q_ref, k_ref, v_ref, o_ref, lse_ref, *, dil, qb, kb, m_len, n_sub, n_ph):
    lane = lax.broadcasted_iota(jnp.int32, (1, LANES), 1)
    lo = lane < ATT_HEAD_DIM
    zero = jnp.zeros((), BF16)
    row = lax.broadcasted_iota(jnp.int32, (qb, kb), 0)
    col = lax.broadcasted_iota(jnp.int32, (qb, kb), 1)
    for ph in range(n_ph):
        for sub in range(n_sub):
            blk = pl.program_id(2) * n_sub + sub
            rows = slice(sub * qb, (sub + 1) * qb)
            start = jnp.clip(blk * qb - ATT_HALF, 0, m_len - kb)
            start = pl.multiple_of(start, ATT_HALF)
            off = start - blk * qb
            q = q_ref[ph, rows, :]
            k = k_ref[ph, pl.ds(start, kb), :]
            v = v_ref[ph, pl.ds(start, kb), :]
            absd = jnp.abs(col - row + off).astype(F32)
            valid = absd <= float(ATT_HALF)
            lse_tile = jnp.zeros((qb, LANES), F32)
            for pair in range(ATT_HEADS // 2):
                sl = slice(pair * LANES, (pair + 1) * LANES)
                qp, kp, vp = q[:, sl], k[:, sl], v[:, sl]
                outs = []
                for hh in range(2):
                    h = 2 * pair + hh
                    slope = 2.0 ** (-(8.0 / ATT_HEADS) * (h + 1))
                    qh = jnp.where(lo if hh == 0 else jnp.logical_not(lo), qp, zero)
                    s = lax.dot_general(qh, kp, (((1,), (1,)), ((), ())), preferred_element_type=F32)
                    s = jnp.where(valid, s - absd * (slope * dil), NEG)
                    mx = jnp.max(s, axis=-1, keepdims=True)
                    p = jnp.exp(s - mx)
                    l = jnp.sum(p, axis=-1, keepdims=True)
                    o = jnp.dot(p.astype(BF16), vp, preferred_element_type=F32)
                    outs.append(o / l)
                    lse_tile = jnp.where(lane == h, mx + jnp.log(l), lse_tile)
                o_ref[ph, rows, sl] = jnp.where(lo, outs[0], outs[1]).astype(BF16)
            lse_ref[ph, rows, :] = lse_tile


def _attn_branch(q, k, v, dil, interpret):
    B, _, m_len, _ = q.shape
    qb = 128
    per_step = 8
    n_sub = min(per_step, m_len // qb)
    n_ph = min(per_step // n_sub, dil)
    kb = min(qb + 2 * ATT_HALF, m_len)
    qspec = pl.BlockSpec((None, n_ph, n_sub * qb, ATT_WIDTH), lambda b, p, i: (b, p, i, 0))
    kvspec = pl.BlockSpec((None, n_ph, m_len, ATT_WIDTH), lambda b, p, i: (b, p, 0, 0))
    return pl.pallas_call(
        functools.partial(_attn_kernel, dil=dil, qb=qb, kb=kb, m_len=m_len, n_sub=n_sub, n_ph=n_ph),
        grid=(B, dil // n_ph, m_len // (n_sub * qb)),
        in_specs=[qspec, kvspec, kvspec],
        out_specs=[qspec, pl.BlockSpec((None, n_ph, n_sub * qb, LANES), lambda b, p, i: (b, p, i, 0))],
        out_shape=[jax.ShapeDtypeStruct((B, dil, m_len, ATT_WIDTH), BF16),
                   jax.ShapeDtypeStruct((B, dil, m_len, LANES), F32)],
        compiler_params=_cparams(("arbitrary", "arbitrary", "arbitrary")),
        name=f"attn_d{dil}",
        interpret=interpret,
    )(q, k, v)


def _mlstm_kernel(qf_ref, ktf_ref, vf_ref, gf_ref, qb_ref, ktb_ref, vb_ref, gb_ref,
                  hf_ref, hb_ref, cn_st, m_st):
    L = ML_CHUNK
    hd = ML_HEAD_DIM

    @pl.when(pl.program_id(1) == 0)
    def _():
        cn_st[...] = jnp.zeros_like(cn_st)
        m_st[...] = jnp.full_like(m_st, NEG)

    t_idx = lax.broadcasted_iota(jnp.int32, (L, L), 0)
    s_idx = lax.broadcasted_iota(jnp.int32, (L, L), 1)
    ones_col = (lax.broadcasted_iota(jnp.int32, (L, LANES), 1) == 0).astype(BF16)
    streams = ((qf_ref, ktf_ref, vf_ref, gf_ref, hf_ref), (qb_ref, ktb_ref, vb_ref, gb_ref, hb_ref))

    prep = []
    for direction in range(2):
        tri = (s_idx <= t_idx) if direction == 0 else (s_idx >= t_idx)
        last = L - 1 if direction == 0 else 0
        rows8 = slice(direction * SUBLANES, (direction + 1) * SUBLANES)
        g8 = streams[direction][3][rows8, :]
        seen_by = (t_idx <= s_idx) if direction == 0 else (t_idx >= s_idx)
        b8 = jnp.dot(jax.nn.log_sigmoid(g8), seen_by.astype(F32), preferred_element_type=F32,
                     precision=lax.Precision.HIGHEST)
        b8 = pltpu.roll(b8, ML_HEADS, axis=0)
        nc8 = g8 - b8
        m_prev8 = m_st[rows8, :]
        b_end8 = b8[:, last:last + 1]
        a8 = b_end8 + nc8
        m_loc8 = jnp.max(a8, axis=-1, keepdims=True)
        w8 = jnp.exp(a8 - m_loc8)
        m_new8 = jnp.maximum(b_end8 + m_prev8[:, 0:1], m_loc8)
        sp8 = jnp.exp(b_end8 + m_prev8[:, 0:1] - m_new8)
        sl8 = jnp.exp(m_loc8 - m_new8)
        m_st[rows8, :] = jnp.broadcast_to(m_new8, (SUBLANES, LANES))
        b_cols = jnp.concatenate([b8, jnp.zeros((LANES - SUBLANES, L), F32)], axis=0).T
        prep.append((tri, nc8, m_prev8, w8, sp8, sl8, b_cols))

    outs = ([], [])
    for h in range(ML_HEADS):
        hs = slice(h * hd, (h + 1) * hd)
        for direction in range(2):
            q_ref, kt_ref, v_ref, _, _ = streams[direction]
            tri, nc8, m_prev8, w8, sp8, sl8, b_cols = prep[direction]
            st = direction * ML_HEADS + h
            q = q_ref[:, hs]
            kt = kt_ref[hs, :]
            v_aug = jnp.concatenate([v_ref[:, hs], ones_col], axis=-1)
            cn_prev = cn_st[st]
            m_prev = m_prev8[h:h + 1, 0:1]

            seen = jnp.where(tri, nc8[h:h + 1, :], NEG)
            mm = jnp.maximum(jnp.max(seen, axis=-1, keepdims=True), m_prev)
            dw = jnp.exp(seen - mm)
            iw = jnp.exp(m_prev - mm)
            qk_s = jnp.dot(q, kt, preferred_element_type=F32) * dw
            intra = jnp.dot(qk_s.astype(BF16), v_aug, preferred_element_type=F32)
            carried = jnp.dot(q, cn_prev.astype(BF16), preferred_element_type=F32)
            num = intra[:, :hd] + iw * carried[:, :hd]
            den = intra[:, hd:hd + 1] + iw * carried[:, hd:hd + 1]
            floor = jnp.exp(-(b_cols[:, h:h + 1] + mm))
            outs[direction].append(num / jnp.maximum(jnp.abs(den), floor))

            kw = (kt.astype(F32) * w8[h:h + 1, :]).astype(BF16)
            cn_loc = jnp.dot(kw, v_aug, preferred_element_type=F32)
            cn_st[st] = sp8[h:h + 1, :] * cn_prev + sl8[h:h + 1, :] * cn_loc
    hf_ref[...] = jnp.concatenate(outs[0], axis=-1)
    hb_ref[...] = jnp.concatenate(outs[1], axis=-1)


def _mlstm(qm, kt, vm, gates_t, B, S, interpret):
    L = ML_CHUNK
    nc = S // L
    fwd = lambda s: s
    bwd = lambda s: nc - 1 - s

    def specs(cidx):
        blk = lambda w: pl.BlockSpec((None, L, w), lambda b, s: (b, cidx(s), 0))
        chunk = lambda r: pl.BlockSpec((None, None, r, L), lambda b, s: (b, cidx(s), 0, 0))
        return [blk(ML_WIDTH), chunk(ML_WIDTH), blk(ML_WIDTH), chunk(N_GATES)]

    q3 = qm.reshape(B, S, ML_WIDTH)
    v3 = vm.reshape(B, S, ML_WIDTH)
    hblk = lambda cidx: pl.BlockSpec((None, L, ML_WIDTH), lambda b, s: (b, cidx(s), 0))
    return pl.pallas_call(
        _mlstm_kernel,
        grid=(B, nc),
        in_specs=specs(fwd) + specs(bwd),
        out_specs=[hblk(fwd), hblk(bwd)],
        out_shape=[jax.ShapeDtypeStruct((B, S, ML_WIDTH), F32),
                   jax.ShapeDtypeStruct((B, S, ML_WIDTH), F32)],
        scratch_shapes=[pltpu.VMEM((2 * ML_HEADS, ML_HEAD_DIM, 2 * ML_HEAD_DIM), F32),
                        pltpu.VMEM((2 * SUBLANES, LANES), F32)],
        compiler_params=_cparams(("arbitrary", "arbitrary")),
        name="mlstm",
        interpret=interpret,
    )(q3, kt, v3, gates_t, q3, kt, v3, gates_t)


def _outproj_kernel(xp_ref, xs_ref, o1_ref, o4_ref, o16_ref, l1_ref, l4_ref, l16_ref,
                    hf_ref, hb_ref, om_ref, gh_ref, wo_ref, gf_ref, wrt_ref, br_ref,
                    h_ref, hn_ref, idx_ref, gate_ref, cnt_ref,
                    carry, nat4, nat16, lnat4, lnat16, *, n_first):
    tm = TOK_TILE
    step = pl.program_id(0)

    @pl.when(step == 0)
    def _():
        carry[...] = jnp.zeros_like(carry)

    for d, o_ref, l_ref, nat, lnat in ((4, o4_ref, l4_ref, nat4, lnat4), (16, o16_ref, l16_ref, nat16, lnat16)):
        for ph in range(d):
            rows = pl.ds(ph, tm // d, stride=d)
            lnat[rows, :] = l_ref[ph]
            for j in range(ATT_WIDTH // LANES):
                nat.at[j][rows, :] = o_ref[ph, :, j * LANES:(j + 1) * LANES].astype(F32)

    l1, l2, l3 = l1_ref[...], lnat4[...], lnat16[...]
    mx = jnp.maximum(jnp.maximum(l1, l2), l3)
    e1, e2, e3 = jnp.exp(l1 - mx), jnp.exp(l2 - mx), jnp.exp(l3 - mx)
    inv = 1.0 / (e1 + e2 + e3)
    er = lax.broadcasted_iota(jnp.int32, (LANES, ATT_WIDTH), 0)
    ec = lax.broadcasted_iota(jnp.int32, (LANES, ATT_WIDTH), 1)
    expand = (ec // ATT_HEAD_DIM == er).astype(BF16)

    def widen(w):
        hi = w.astype(BF16)
        lo = (w - hi.astype(F32)).astype(BF16)
        return (jnp.dot(hi, expand, preferred_element_type=F32)
                + jnp.dot(lo, expand, preferred_element_type=F32))

    slabs = lambda nat: jnp.concatenate([nat[j] for j in range(ATT_WIDTH // LANES)], axis=-1)
    attn = (widen(e1 * inv) * o1_ref[...].astype(F32)
            + widen(e2 * inv) * slabs(nat4)
            + widen(e3 * inv) * slabs(nat16))

    hsum = hf_ref[...] + hb_ref[...]
    normed = []
    for hh in range(ML_HEADS):
        hv = hsum[:, hh * ML_HEAD_DIM:(hh + 1) * ML_HEAD_DIM]
        normed.append(hv * lax.rsqrt(jnp.mean(hv * hv, axis=-1, keepdims=True) + EPS))
    ml = jax.nn.sigmoid(om_ref[...]) * (jnp.concatenate(normed, axis=-1) * gh_ref[...])

    mix = jnp.concatenate([attn.astype(BF16), ml.astype(BF16)], axis=-1)
    x = jnp.where(step < n_first, xp_ref[...], xs_ref[...])
    h = x + jnp.dot(mix, wo_ref[...], preferred_element_type=F32)
    h_ref[...] = h
    hn = h * lax.rsqrt(jnp.mean(h * h, axis=-1, keepdims=True) + EPS) * gf_ref[...]
    for j in range(D_MODEL // LANES):
        hn_ref[pl.ds(j, tm, stride=SUBLANES), :] = hn[:, j * LANES:(j + 1) * LANES]

    logits = lax.dot_general(wrt_ref[...], hn.astype(BF16), (((1,), (1,)), ((), ())),
                             preferred_element_type=F32) + br_ref[:, 0:1]
    eid = lax.broadcasted_iota(jnp.int32, (N_EXPERTS, tm), 0)
    work = logits
    vals, idxs, hots = [], [], []
    for _ in range(TOP_K):
        top = jnp.max(work, axis=0, keepdims=True)
        idx = jnp.min(jnp.where(work == top, eid, N_EXPERTS), axis=0, keepdims=True)
        hot = eid == idx
        work = jnp.where(hot, -jnp.inf, work)
        vals.append(top)
        idxs.append(idx)
        hots.append(hot)
    exps = [jnp.exp(v - vals[0]) for v in vals]
    inv_den = 1.0 / (exps[0] + exps[1] + exps[2] + exps[3])

    cnt = jnp.where(hots[0] | hots[1] | hots[2] | hots[3], 1.0, 0.0)
    r_idx = lax.broadcasted_iota(jnp.int32, (tm, tm), 0)
    c_idx = lax.broadcasted_iota(jnp.int32, (tm, tm), 1)
    before = (r_idx < c_idx).astype(BF16)
    rank_all = jnp.dot(cnt.astype(BF16), before, preferred_element_type=F32) + carry[:, 0:1]
    carry[...] = carry[...] + jnp.sum(cnt, axis=1, keepdims=True)
    cnt_ref[...] = carry[...]

    ranks = [jnp.sum(jnp.where(hots[kk], rank_all, 0.0), axis=0, keepdims=True).astype(jnp.int32)
             for kk in range(TOP_K)]
    idx_ref[...] = jnp.concatenate(idxs + ranks, axis=0)
    gate_ref[...] = jnp.concatenate([e * inv_den for e in exps] + [jnp.zeros((TOP_K, tm), F32)], axis=0)


def _outproj(xp, xs, S, o1, o4, o16, l1, l4, l16, hf, hb, om, g_head, w_out_b, g_ffn, w_rt, b_r, interpret):
    tm = TOK_TILE
    n_first = xp.shape[0] // tm
    T = xp.shape[0] + xs.shape[0]
    tps = S // tm
    row = lambda w: pl.BlockSpec((tm, w), lambda i: (i, 0))
    col = lambda r: pl.BlockSpec((r, tm), lambda i: (0, i))
    const = lambda shape: pl.BlockSpec(shape, lambda i: (0,) * len(shape))
    phase = lambda d, w: pl.BlockSpec((None, d, tm // d, w), lambda i: (i // tps, 0, i % tps, 0))
    xp_spec, xs_spec = _dual_rows(n_first, D_MODEL)
    return pl.pallas_call(
        functools.partial(_outproj_kernel, n_first=n_first),
        grid=(T // tm,),
        in_specs=[xp_spec, xs_spec, row(512), phase(4, 512), phase(16, 512),
                  row(LANES), phase(4, LANES), phase(16, LANES),
                  row(512), row(512), row(512), const((1, ML_WIDTH)),
                  const((D_MODEL, D_MODEL)), const((1, D_MODEL)),
                  const((N_EXPERTS, D_MODEL)), const((N_EXPERTS, LANES))],
        out_specs=[row(D_MODEL), pl.BlockSpec((tm * ROW_TILE, LANES), lambda i: (i, 0)),
                   col(2 * TOP_K), col(2 * TOP_K), const((N_EXPERTS, LANES))],
        out_shape=[jax.ShapeDtypeStruct((T, D_MODEL), F32),
                   jax.ShapeDtypeStruct((T * ROW_TILE, LANES), F32),
                   jax.ShapeDtypeStruct((2 * TOP_K, T), jnp.int32),
                   jax.ShapeDtypeStruct((2 * TOP_K, T), F32),
                   jax.ShapeDtypeStruct((N_EXPERTS, LANES), F32)],
        scratch_shapes=[pltpu.VMEM((N_EXPERTS, LANES), F32),
                        pltpu.VMEM((ATT_WIDTH // LANES, tm, LANES), F32),
                        pltpu.VMEM((ATT_WIDTH // LANES, tm, LANES), F32),
                        pltpu.VMEM((tm, LANES), F32), pltpu.VMEM((tm, LANES), F32)],
        compiler_params=_cparams(("arbitrary",)),
        name="outproj_router",
        interpret=interpret,
    )(xp, xs, o1, o4, o16, l1, l4, l16, hf, hb, om, g_head.reshape(1, ML_WIDTH), w_out_b,
      g_ffn.reshape(1, D_MODEL), w_rt, b_r)


def _expert_kernel(be_ref, nu_ref, src_cur, src_next, hn_ref, wgu_ref, bgu_ref, wd_ref, bd_ref, y_ref,
                   wgu_b, wd_b, xbuf, gsem):
    j = pl.program_id(0)
    slot = lax.rem(j, 2)
    other = 1 - slot
    rows = MOE_BLOCK * ROW_TILE
    active = j < nu_ref[0]

    def gather_wait(s):
        pltpu.make_async_copy(hn_ref.at[pl.ds(0, rows)], xbuf.at[s], gsem.at[s]).wait()

    def gather_row(tab, r, s, queue=0):
        pltpu.make_async_copy(hn_ref.at[pl.ds(pl.multiple_of(tab[0, 0, r], ROW_TILE), ROW_TILE)],
                              xbuf.at[s, pl.ds(r * ROW_TILE, ROW_TILE)], gsem.at[s]).start(priority=queue)

    @pl.when(j == 0)
    def _():
        def first(r, carry):
            gather_row(src_cur, r, 0)
            return carry

        lax.fori_loop(0, MOE_BLOCK, first, 0, unroll=8)

    @pl.when(jnp.logical_and(active, jnp.logical_or(j == 0, be_ref[j] != be_ref[jnp.maximum(j - 1, 0)])))
    def _():
        wgu_b[...] = wgu_ref[0].astype(BF16)
        wd_b[...] = wd_ref[0].astype(BF16)

    def step(cur, nxt):
        gather_wait(cur)
        for r in range(MOE_BLOCK):
            gather_row(src_next, r, nxt, queue=r % 2)

        xv = xbuf.at[cur]
        x = jnp.concatenate([xv[pl.ds(c, MOE_BLOCK, stride=ROW_TILE), :] for c in range(ROW_TILE)],
                            axis=-1).astype(BF16)
        gu = jnp.dot(x, wgu_b[...], preferred_element_type=F32) + bgu_ref[0]
        g = jnp.minimum(gu[:, :D_FF], SWIGLU_LIMIT)
        u = jnp.clip(gu[:, D_FF:], -SWIGLU_LIMIT, SWIGLU_LIMIT)
        hdn = (u + 1.0) * (g * jax.nn.sigmoid(SWIGLU_ALPHA * g))
        y = jnp.dot(hdn.astype(BF16), wd_b[...], preferred_element_type=F32) + bd_ref[0]
        for c in range(ROW_TILE):
            y_ref[pl.ds(c, MOE_BLOCK, stride=ROW_TILE), :] = y[:, c * LANES:(c + 1) * LANES]

    for parity in range(2):
        @pl.when(jnp.logical_and(active, slot == parity))
        def _():
            step(parity, 1 - parity)

    @pl.when(j == nu_ref[0] - 1)
    def _():
        gather_wait(other)

    @pl.when(jnp.logical_not(active))
    def _():
        y_ref[...] = jnp.zeros_like(y_ref)


def _experts(hn_rt, src_tab, block_e, n_used, w_gate_up, b_gate_up, w_down, b_down, interpret):
    nb = src_tab.shape[0]
    tab = lambda f: pl.BlockSpec((1, 1, MOE_BLOCK), lambda j, be, nu: (f(j, nu), 0, 0), memory_space=pltpu.SMEM)
    grid_spec = pltpu.PrefetchScalarGridSpec(
        num_scalar_prefetch=2,
        grid=(nb,),
        in_specs=[tab(lambda j, nu: j), tab(lambda j, nu: jnp.minimum(j + 1, nu[0] - 1)),
                  pl.BlockSpec(memory_space=pl.ANY),
                  pl.BlockSpec((1, D_MODEL, 2 * D_FF), lambda j, be, nu: (be[j], 0, 0)),
                  pl.BlockSpec((1, 1, 2 * D_FF), lambda j, be, nu: (be[j], 0, 0)),
                  pl.BlockSpec((1, D_FF, D_MODEL), lambda j, be, nu: (be[j], 0, 0)),
                  pl.BlockSpec((1, 1, D_MODEL), lambda j, be, nu: (be[j], 0, 0))],
        out_specs=pl.BlockSpec((MOE_BLOCK * ROW_TILE, LANES), lambda j, be, nu: (j, 0)),
        scratch_shapes=[pltpu.VMEM((D_MODEL, 2 * D_FF), BF16),
                        pltpu.VMEM((D_FF, D_MODEL), BF16),
                        pltpu.VMEM((2, MOE_BLOCK * ROW_TILE, LANES), F32),
                        pltpu.SemaphoreType.DMA((2,))],
    )
    return pl.pallas_call(
        _expert_kernel,
        grid_spec=grid_spec,
        out_shape=jax.ShapeDtypeStruct((nb * MOE_BLOCK * ROW_TILE, LANES), F32),
        compiler_params=_cparams(("arbitrary",)),
        name="moe_experts",
        interpret=interpret,
    )(block_e, n_used, src_tab, src_tab, hn_rt, w_gate_up,
      b_gate_up.reshape(N_EXPERTS, 1, 2 * D_FF), w_down, b_down.reshape(N_EXPERTS, 1, D_MODEL))


def _combine_kernel(dest_ref, dnext_ref, y_ref, h_ref, gate_ref, gfin_ref, outp_ref, outs_ref, ybuf, sem,
                    *, tm, n_first, n_tiles):
    i = pl.program_id(0)
    slot = lax.rem(i, 2)

    def fetch(tab, s):
        def issue(r, carry):
            for kk in range(TOP_K):
                d = pl.multiple_of(tab[0, 0, kk * tm + r] * ROW_TILE, ROW_TILE)
                pltpu.make_async_copy(y_ref.at[pl.ds(d, ROW_TILE)],
                                      ybuf.at[s, kk, pl.ds(pl.multiple_of(r * ROW_TILE, ROW_TILE), ROW_TILE)],
                                      sem.at[s]).start(priority=kk % 2)
            return carry

        lax.fori_loop(0, tm, issue, 0, unroll=4)

    @pl.when(i == 0)
    def _():
        fetch(dest_ref, 0)

    @pl.when(i + 1 < n_tiles)
    def _():
        fetch(dnext_ref, 1 - slot)

    for kk in range(TOP_K):
        pltpu.make_async_copy(y_ref.at[pl.ds(0, tm * ROW_TILE)], ybuf.at[slot, kk], sem.at[slot]).wait()

    gate = gate_ref[...]
    parts = []
    for j in range(ROW_TILE):
        rows = pl.ds(j, tm, stride=ROW_TILE)
        part = gate[:, 0:1] * ybuf.at[slot, 0][rows, :]
        for kk in range(1, TOP_K):
            part = part + gate[:, kk:kk + 1] * ybuf.at[slot, kk][rows, :]
        parts.append(part)
    h = h_ref[...] + jnp.concatenate(parts, axis=-1)
    res = h * lax.rsqrt(jnp.mean(h * h, axis=-1, keepdims=True) + EPS) * gfin_ref[...]

    @pl.when(pl.program_id(0) < n_first)
    def _():
        outp_ref[...] = res

    @pl.when(pl.program_id(0) >= n_first)
    def _():
        outs_ref[...] = res


def _combine(y, dest_tiles, h, gate_tm, g_final, t_first, tm, interpret):
    T = h.shape[0]
    n_first = t_first // tm
    outp_spec, outs_spec = _dual_rows(n_first, D_MODEL, tm)
    n_tiles = T // tm
    return pl.pallas_call(
        functools.partial(_combine_kernel, tm=tm, n_first=n_first, n_tiles=n_tiles),
        grid=(n_tiles,),
        in_specs=[pl.BlockSpec((1, 1, tm * TOP_K), lambda i: (i, 0, 0), memory_space=pltpu.SMEM),
                  pl.BlockSpec((1, 1, tm * TOP_K), lambda i: (jnp.minimum(i + 1, n_tiles - 1), 0, 0),
                               memory_space=pltpu.SMEM),
                  pl.BlockSpec(memory_space=pl.ANY),
                  pl.BlockSpec((tm, D_MODEL), lambda i: (i, 0)),
                  pl.BlockSpec((tm, 2 * TOP_K), lambda i: (i, 0)),
                  pl.BlockSpec((1, D_MODEL), lambda i: (0, 0))],
        out_specs=[outp_spec, outs_spec],
        out_shape=[jax.ShapeDtypeStruct((t_first, D_MODEL), F32),
                   jax.ShapeDtypeStruct((T - t_first, D_MODEL), F32)],
        scratch_shapes=[pltpu.VMEM((2, TOP_K, tm * ROW_TILE, LANES), F32), pltpu.SemaphoreType.DMA((2,))],
        compiler_params=_cparams(("arbitrary",)),
        name="moe_combine",
        interpret=interpret,
    )(dest_tiles, dest_tiles, y, h, gate_tm, g_final.reshape(1, D_MODEL))


def _routing_tables(idx_rows, counts, T, tm):
    tk = T * TOP_K
    top_idx = idx_rows[:TOP_K]
    rank = idx_rows[TOP_K:]
    counts = counts.astype(jnp.int32)
    blocks_e = (counts + MOE_BLOCK - 1) // MOE_BLOCK
    cum_blocks = jnp.cumsum(blocks_e)
    start_row = (cum_blocks - blocks_e) * MOE_BLOCK
    experts = jnp.arange(N_EXPERTS, dtype=jnp.int32)
    slot_of = jnp.sum(jnp.where(top_idx[..., None] == experts, start_row, 0), axis=-1) + rank
    dest_tiles = slot_of.reshape(TOP_K, T // tm, tm).transpose(1, 0, 2).reshape(T // tm, 1, TOP_K * tm)
    n_blocks = -(-tk // MOE_BLOCK) + N_EXPERTS
    n_slots = n_blocks * MOE_BLOCK
    pad_len = jnp.concatenate([blocks_e * MOE_BLOCK - counts, n_slots - cum_blocks[-1:] * MOE_BLOCK])
    pad_base = jnp.concatenate([start_row + counts, cum_blocks[-1:] * MOE_BLOCK])
    pad_end = jnp.cumsum(pad_len)
    i = jnp.arange(n_slots - tk, dtype=jnp.int32)
    seg = i[:, None] >= pad_end[None, :]
    shift = jnp.concatenate([pad_base[:1], pad_base[1:] - pad_end[:-1]])
    incr = jnp.concatenate([shift[1:] - shift[:-1], jnp.zeros((1,), jnp.int32)])
    pad_slot = i + shift[0] + jnp.sum(jnp.where(seg, incr, 0), axis=-1)
    source = jnp.argsort(jnp.concatenate([slot_of.reshape(-1), pad_slot])).astype(jnp.int32)
    src_row = (jnp.where(source < tk, source % T, 0) * ROW_TILE).reshape(n_blocks, 1, MOE_BLOCK)
    blocks = jnp.arange(n_blocks, dtype=jnp.int32)
    block_e = jnp.sum((blocks[:, None] >= cum_blocks[None, :]).astype(jnp.int32), axis=-1)
    last_e = jnp.sum((cum_blocks[-1] - 1 >= cum_blocks).astype(jnp.int32))
    block_e = jnp.minimum(block_e, last_e).astype(jnp.int32)
    return dest_tiles, src_row, block_e, cum_blocks[-1:].astype(jnp.int32)


def _layer(xp, xs, S, g_mix, w_in, conv_w, conv_b, b_gates, g_head, w_out, g_ffn, w_router, b_router,
           w_gate_up, b_gate_up, w_down, b_down, g_final, interpret=False):
    T = xp.shape[0] + xs.shape[0]
    B = T // S

    w_in_p = jnp.pad(w_in, ((0, 0), (0, IN_PAD - w_in.shape[1]))).astype(BF16)
    cw_p = jnp.pad(conv_w, ((0, SUBLANES - CONV_W), (0, 0)))
    cb = conv_b.reshape(1, 2 * ML_WIDTH)
    bg_p = jnp.pad(b_gates, (0, LANES - N_GATES)).reshape(1, LANES)
    (q1, k1, v1, q4, k4, v4, q16, k16, v16, qm, kt, vm, om, gates_t) = _inproj(
        xp, xs, S, g_mix, w_in_p, cw_p, cb, bg_p, interpret)

    as_phase = lambda a: a.reshape(B, 1, S, ATT_WIDTH)
    o1, l1 = _attn_branch(as_phase(q1), as_phase(k1), as_phase(v1), 1, interpret)
    o4, l4 = _attn_branch(q4, k4, v4, 4, interpret)
    o16, l16 = _attn_branch(q16, k16, v16, 16, interpret)

    hf, hb = _mlstm(qm, kt, vm, gates_t, B, S, interpret)

    w_rt = w_router.T.astype(BF16)
    b_r = jnp.broadcast_to(b_router[:, None], (N_EXPERTS, LANES))
    h, hn, idx_rows, gate_rows, counts = _outproj(
        xp, xs, S, o1.reshape(T, ATT_WIDTH), o4, o16, l1.reshape(T, LANES), l4, l16,
        hf.reshape(T, ML_WIDTH), hb.reshape(T, ML_WIDTH), om, g_head, w_out.astype(BF16), g_ffn,
        w_rt, b_r, interpret)

    tm = 256
    dest_tiles, src_row, block_e, n_used = _routing_tables(idx_rows, counts[:, 0], T, tm)
    y = _experts(hn, src_row, block_e, n_used, w_gate_up, b_gate_up, w_down, b_down, interpret)
    return _combine(y, dest_tiles, h, gate_rows.T, g_final, xp.shape[0], tm, interpret)


def kernel(x_prompt, x_sample, g_mix, w_in, conv_w, conv_b, b_gates, g_head, w_out, g_ffn, w_router,
           b_router, w_gate_up, b_gate_up, w_down, b_down, g_final):
    S = x_prompt.shape[1]
    yp, ys = _layer(x_prompt.reshape(-1, D_MODEL), x_sample.reshape(-1, D_MODEL), S,
                    g_mix, w_in, conv_w, conv_b, b_gates, g_head, w_out, g_ffn, w_router, b_router,
                    w_gate_up, b_gate_up, w_down, b_down, g_final)
    return (yp.reshape(x_prompt.shape), ys.reshape(x_sample.shape))
```

```python
import functools

import jax
import jax.numpy as jnp
from jax import lax
from jax.experimental import pallas as pl
from jax.experimental.pallas import tpu as pltpu

F32 = jnp.float32
BF16 = jnp.bfloat16

D_MODEL = 1024
ATT_HEADS = 8
ATT_HEAD_DIM = 64
ATT_WIDTH = ATT_HEADS * ATT_HEAD_DIM
DILATIONS = (1, 4, 16)
ATT_HALF = 64
ML_HEADS = 4
ML_HEAD_DIM = 128
ML_WIDTH = ML_HEADS * ML_HEAD_DIM
ML_CHUNK = 128
CONV_W = 5
N_GATES = 2 * 2 * ML_HEADS
N_EXPERTS = 32
TOP_K = 4
D_FF = 1024
SWIGLU_LIMIT = 7.0
SWIGLU_ALPHA = 1.702
MOE_BLOCK = 512
EPS = 1e-6
NEG = -1e30

LANES = 128
SUBLANES = 8
IN_PAD = 7 * 512 + LANES
TOK_TILE = 512
ROW_TILE = D_MODEL // LANES
VMEM_LIMIT = 56 * 1024 * 1024


def _cparams(sem):
    return pltpu.CompilerParams(dimension_semantics=sem, vmem_limit_bytes=VMEM_LIMIT)


def _dual_rows(n_first, width, rows=TOK_TILE):
    first = pl.BlockSpec((rows, width), lambda i: (jnp.minimum(i, n_first - 1), 0))
    second = pl.BlockSpec((rows, width), lambda i: (jnp.maximum(i - n_first, 0), 0))
    return first, second


def _inproj_kernel(xp_ref, xs_ref, pp_ref, pn_ref, sp_ref, sn_ref, g_ref, w_ref, cw_ref, cb_ref, bg_ref,
                   q1_ref, k1_ref, v1_ref, q4_ref, k4_ref, v4_ref, q16_ref, k16_ref, v16_ref,
                   qm_ref, kt_ref, vm_ref, om_ref, gt_ref,
                   ext, stage_q, stage_k, stage_v, *, n_first, tiles_per_seq):
    i = pl.program_id(0)
    first = i < n_first
    ti = lax.rem(i, tiles_per_seq)
    tm = TOK_TILE

    def norm(v):
        vn = v * lax.rsqrt(jnp.mean(v * v, axis=-1, keepdims=True) + EPS)
        return (vn * g_ref[...]).astype(BF16)

    xn = norm(jnp.where(first, xp_ref[...], xs_ref[...]))
    prev = jnp.where(ti > 0, jnp.where(first, pp_ref[0], sp_ref[0]), 0.0)
    nxt = jnp.where(ti < tiles_per_seq - 1, jnp.where(first, pn_ref[0], sn_ref[0]), 0.0)
    xn_halo = norm(jnp.concatenate([prev, nxt], axis=0))

    def proj(lo, hi, lhs=xn):
        return jnp.dot(lhs, w_ref[:, lo:hi], preferred_element_type=F32)

    def emit_attn(col, scale, stage, r1, r4, r16):
        p = proj(col, col + ATT_WIDTH)
        if scale is not None:
            p = p * scale
        r1[...] = p.astype(BF16)
        for j in range(ATT_WIDTH // LANES):
            stage[j] = p[:, j * LANES:(j + 1) * LANES]
        for d, ref in ((4, r4), (16, r16)):
            for ph in range(d):
                for j in range(ATT_WIDTH // LANES):
                    ref[ph, :, j * LANES:(j + 1) * LANES] = (
                        stage.at[j][pl.ds(ph, tm // d, stride=d), :].astype(BF16))

    emit_attn(0, ATT_HEAD_DIM ** -0.5, stage_q, q1_ref, q4_ref, q16_ref)
    emit_attn(512, None, stage_k, k1_ref, k4_ref, k16_ref)
    emit_attn(1024, None, stage_v, v1_ref, v4_ref, v16_ref)

    halo_qk = proj(1536, 2560, xn_halo)
    ext[0:SUBLANES, :] = halo_qk[0:SUBLANES]
    ext[SUBLANES:SUBLANES + tm, :] = proj(1536, 2560)
    ext[SUBLANES + tm:, :] = halo_qk[SUBLANES:]
    base = SUBLANES - CONV_W // 2
    acc = cb_ref[...] + ext[base:base + tm, :] * cw_ref[0:1, :]
    for j in range(1, CONV_W):
        acc = acc + ext[base + j:base + j + tm, :] * cw_ref[j:j + 1, :]
    act = acc * jax.nn.sigmoid(acc)
    qm_ref[...] = (act[:, :ML_WIDTH] * (ML_HEAD_DIM ** -0.5)).astype(BF16)
    for c in range(tm // ML_CHUNK):
        kt_ref[c] = act[c * ML_CHUNK:(c + 1) * ML_CHUNK, ML_WIDTH:].T.astype(BF16)

    vm_ref[...] = proj(2560, 3072).astype(BF16)
    om_ref[...] = proj(3072, 3584)
    gates = proj(3584, IN_PAD) + bg_ref[...]
    for c in range(tm // ML_CHUNK):
        gt_ref[c] = gates[c * ML_CHUNK:(c + 1) * ML_CHUNK, :].T[0:N_GATES, :]


def _inproj(xp, xs, S, g_mix, w_in_p, cw_p, cb, bg_p, interpret):
    tm = TOK_TILE
    n_first = xp.shape[0] // tm
    T = xp.shape[0] + xs.shape[0]
    B = T // S
    tps = S // tm
    per = tm // SUBLANES
    row = lambda w: pl.BlockSpec((tm, w), lambda i: (i, 0))
    const = lambda shape: pl.BlockSpec(shape, lambda i: (0,) * len(shape))
    xp_spec, xs_spec = _dual_rows(n_first, D_MODEL)

    def halo(n_tiles, shift, offset):
        last = n_tiles * per - 1
        return pl.BlockSpec(
            (1, SUBLANES, D_MODEL),
            lambda i: (jnp.clip((jnp.clip(i - shift, 0, n_tiles - 1)) * per + offset, 0, last), 0, 0))

    n_second = xs.shape[0] // tm
    xp3 = xp.reshape(xp.shape[0] // SUBLANES, SUBLANES, D_MODEL)
    xs3 = xs.reshape(xs.shape[0] // SUBLANES, SUBLANES, D_MODEL)
    phase = lambda d, w: pl.BlockSpec((None, d, tm // d, w), lambda i: (i // tps, 0, i % tps, 0))
    bf = lambda shape: jax.ShapeDtypeStruct(shape, BF16)
    nat = bf((T, ATT_WIDTH))
    ph4 = bf((B, 4, S // 4, ATT_WIDTH))
    ph16 = bf((B, 16, S // 16, ATT_WIDTH))
    return pl.pallas_call(
        functools.partial(_inproj_kernel, n_first=n_first, tiles_per_seq=tps),
        grid=(T // tm,),
        in_specs=[xp_spec, xs_spec,
                  halo(n_first, 0, -1), halo(n_first, 0, per),
                  halo(n_second, n_first, -1), halo(n_second, n_first, per),
                  const((1, D_MODEL)), const((D_MODEL, IN_PAD)),
                  const((SUBLANES, 2 * ML_WIDTH)), const((1, 2 * ML_WIDTH)), const((1, LANES))],
        out_specs=[row(512), row(512), row(512),
                   phase(4, 512), phase(4, 512), phase(4, 512),
                   phase(16, 512), phase(16, 512), phase(16, 512),
                   row(512),
                   pl.BlockSpec((None, tm // ML_CHUNK, ML_WIDTH, ML_CHUNK), lambda i: (i // tps, i % tps, 0, 0)),
                   row(512), row(512),
                   pl.BlockSpec((None, tm // ML_CHUNK, N_GATES, ML_CHUNK), lambda i: (i // tps, i % tps, 0, 0))],
        out_shape=[nat, nat, nat, ph4, ph4, ph4, ph16, ph16, ph16,
                   bf((T, ML_WIDTH)),
                   bf((B, S // ML_CHUNK, ML_WIDTH, ML_CHUNK)),
                   bf((T, ML_WIDTH)),
                   jax.ShapeDtypeStruct((T, ML_WIDTH), F32),
                   jax.ShapeDtypeStruct((B, S // ML_CHUNK, N_GATES, ML_CHUNK), F32)],
        scratch_shapes=[pltpu.VMEM((tm + 2 * SUBLANES, 2 * ML_WIDTH), F32),
                        pltpu.VMEM((ATT_WIDTH // LANES, tm, LANES), F32),
                        pltpu.VMEM((ATT_WIDTH // LANES, tm, LANES), F32),
                        pltpu.VMEM((ATT_WIDTH // LANES, tm, LANES), F32)],
        compiler_params=_cparams(("arbitrary",)),
        name="inproj",
        interpret=interpret,
    )(xp, xs, xp3, xp3, xs3, xs3, g_mix.reshape(1, D_MODEL), w_in_p, cw_p, cb, bg_p)


def _attn_kernel(q_ref, k_ref, v_ref, o_ref, lse_ref, *, dil, qb, kb, m_len, n_sub, n_ph):
    lane = lax.broadcasted_iota(jnp.int32, (1, LANES), 1)
    lo = lane < ATT_HEAD_DIM
    zero = jnp.zeros((), BF16)
    row = lax.broadcasted_iota(jnp.int32, (qb, kb), 0)
    col = lax.broadcasted_iota(jnp.int32, (qb, kb), 1)
    for ph in range(n_ph):
        for sub in range(n_sub):
            blk = pl.program_id(2) * n_sub + sub
            rows = slice(sub * qb, (sub + 1) * qb)
            start = jnp.clip(blk * qb - ATT_HALF, 0, m_len - kb)
            start = pl.multiple_of(start, ATT_HALF)
            off = start - blk * qb
            q = q_ref[ph, rows, :]
            k = k_ref[ph, pl.ds(start, kb), :]
            v = v_ref[ph, pl.ds(start, kb), :]
            absd = jnp.abs(col - row + off).astype(F32)
            valid = absd <= float(ATT_HALF)
            lse_tile = jnp.zeros((qb, LANES), F32)
            for pair in range(ATT_HEADS // 2):
                sl = slice(pair * LANES, (pair + 1) * LANES)
                qp, kp, vp = q[:, sl], k[:, sl], v[:, sl]
                outs = []
                for hh in range(2):
                    h = 2 * pair + hh
                    slope = 2.0 ** (-(8.0 / ATT_HEADS) * (h + 1))
                    qh = jnp.where(lo if hh == 0 else jnp.logical_not(lo), qp, zero)
                    s = lax.dot_general(qh, kp, (((1,), (1,)), ((), ())), preferred_element_type=F32)
                    s = jnp.where(valid, s - absd * (slope * dil), NEG)
                    mx = jnp.max(s, axis=-1, keepdims=True)
                    p = jnp.exp(s - mx)
                    l = jnp.sum(p, axis=-1, keepdims=True)
                    o = jnp.dot(p.astype(BF16), vp, preferred_element_type=F32)
                    outs.append(o / l)
                    lse_tile = jnp.where(lane == h, mx + jnp.log(l), lse_tile)
                o_ref[ph, rows, sl] = jnp.where(lo, outs[0], outs[1]).astype(BF16)
            lse_ref[ph, rows, :] = lse_tile


def _attn_branch(q, k, v, dil, interpret):
    B, _, m_len, _ = q.shape
    qb = 128
    per_step = 16
    n_sub = min(per_step, m_len // qb)
    n_ph = min(per_step // n_sub, dil)
    kb = min(qb + 2 * ATT_HALF, m_len)
    qspec = pl.BlockSpec((None, n_ph, n_sub * qb, ATT_WIDTH), lambda b, p, i: (b, p, i, 0))
    kvspec = pl.BlockSpec((None, n_ph, m_len, ATT_WIDTH), lambda b, p, i: (b, p, 0, 0))
    return pl.pallas_call(
        functools.partial(_attn_kernel, dil=dil, qb=qb, kb=kb, m_len=m_len, n_sub=n_sub, n_ph=n_ph),
        grid=(B, dil // n_ph, m_len // (n_sub * qb)),
        in_specs=[qspec, kvspec, kvspec],
        out_specs=[qspec, pl.BlockSpec((None, n_ph, n_sub * qb, LANES), lambda b, p, i: (b, p, i, 0))],
        out_shape=[jax.ShapeDtypeStruct((B, dil, m_len, ATT_WIDTH), BF16),
                   jax.ShapeDtypeStruct((B, dil, m_len, LANES), F32)],
        compiler_params=_cparams(("arbitrary", "arbitrary", "arbitrary")),
        name=f"attn_d{dil}",
        interpret=interpret,
    )(q, k, v)


def _mlstm_kernel(qf_ref, ktf_ref, vf_ref, gf_ref, qb_ref, ktb_ref, vb_ref, gb_ref,
                  hf_ref, hb_ref, cn_st, m_st):
    L = ML_CHUNK
    hd = ML_HEAD_DIM

    @pl.when(pl.program_id(1) == 0)
    def _():
        cn_st[...] = jnp.zeros_like(cn_st)
        m_st[...] = jnp.full_like(m_st, NEG)

    t_idx = lax.broadcasted_iota(jnp.int32, (L, L), 0)
    s_idx = lax.broadcasted_iota(jnp.int32, (L, L), 1)
    ones_col = (lax.broadcasted_iota(jnp.int32, (L, LANES), 1) == 0).astype(BF16)
    streams = ((qf_ref, ktf_ref, vf_ref, gf_ref, hf_ref), (qb_ref, ktb_ref, vb_ref, gb_ref, hb_ref))

    prep = []
    for direction in range(2):
        tri = (s_idx <= t_idx) if direction == 0 else (s_idx >= t_idx)
        last = L - 1 if direction == 0 else 0
        rows8 = slice(direction * SUBLANES, (direction + 1) * SUBLANES)
        g8 = streams[direction][3][rows8, :]
        seen_by = (t_idx <= s_idx) if direction == 0 else (t_idx >= s_idx)
        b8 = jnp.dot(jax.nn.log_sigmoid(g8), seen_by.astype(F32), preferred_element_type=F32,
                     precision=lax.Precision.HIGHEST)
        b8 = pltpu.roll(b8, ML_HEADS, axis=0)
        nc8 = g8 - b8
        m_prev8 = m_st[rows8, :]
        b_end8 = b8[:, last:last + 1]
        a8 = b_end8 + nc8
        m_loc8 = jnp.max(a8, axis=-1, keepdims=True)
        w8 = jnp.exp(a8 - m_loc8)
        m_new8 = jnp.maximum(b_end8 + m_prev8[:, 0:1], m_loc8)
        sp8 = jnp.exp(b_end8 + m_prev8[:, 0:1] - m_new8)
        sl8 = jnp.exp(m_loc8 - m_new8)
        m_st[rows8, :] = jnp.broadcast_to(m_new8, (SUBLANES, LANES))
        b_cols = jnp.concatenate([b8, jnp.zeros((LANES - SUBLANES, L), F32)], axis=0).T
        prep.append((tri, nc8, m_prev8, w8, sp8, sl8, b_cols))

    outs = ([], [])
    for h in range(ML_HEADS):
        hs = slice(h * hd, (h + 1) * hd)
        for direction in range(2):
            q_ref, kt_ref, v_ref, _, _ = streams[direction]
            tri, nc8, m_prev8, w8, sp8, sl8, b_cols = prep[direction]
            st = direction * ML_HEADS + h
            q = q_ref[:, hs]
            kt = kt_ref[hs, :]
            v_aug = jnp.concatenate([v_ref[:, hs], ones_col], axis=-1)
            cn_prev = cn_st[st]
            m_prev = m_prev8[h:h + 1, 0:1]

            seen = jnp.where(tri, nc8[h:h + 1, :], NEG)
            mm = jnp.maximum(jnp.max(seen, axis=-1, keepdims=True), m_prev)
            dw = jnp.exp(seen - mm)
            iw = jnp.exp(m_prev - mm)
            qk_s = jnp.dot(q, kt, preferred_element_type=F32) * dw
            intra = jnp.dot(qk_s.astype(BF16), v_aug, preferred_element_type=F32)
            carried = jnp.dot(q, cn_prev.astype(BF16), preferred_element_type=F32)
            num = intra[:, :hd] + iw * carried[:, :hd]
            den = intra[:, hd:hd + 1] + iw * carried[:, hd:hd + 1]
            floor = jnp.exp(-(b_cols[:, h:h + 1] + mm))
            outs[direction].append(num / jnp.maximum(jnp.abs(den), floor))

            kw = (kt.astype(F32) * w8[h:h + 1, :]).astype(BF16)
            cn_loc = jnp.dot(kw, v_aug, preferred_element_type=F32)
            cn_st[st] = sp8[h:h + 1, :] * cn_prev + sl8[h:h + 1, :] * cn_loc
    hf_ref[...] = jnp.concatenate(outs[0], axis=-1)
    hb_ref[...] = jnp.concatenate(outs[1], axis=-1)


def _mlstm(qm, kt, vm, gates_t, B, S, interpret):
    L = ML_CHUNK
    nc = S // L
    fwd = lambda s: s
    bwd = lambda s: nc - 1 - s

    def specs(cidx):
        blk = lambda w: pl.BlockSpec((None, L, w), lambda b, s: (b, cidx(s), 0))
        chunk = lambda r: pl.BlockSpec((None, None, r, L), lambda b, s: (b, cidx(s), 0, 0))
        return [blk(ML_WIDTH), chunk(ML_WIDTH), blk(ML_WIDTH), chunk(N_GATES)]

    q3 = qm.reshape(B, S, ML_WIDTH)
    v3 = vm.reshape(B, S, ML_WIDTH)
    hblk = lambda cidx: pl.BlockSpec((None, L, ML_WIDTH), lambda b, s: (b, cidx(s), 0))
    return pl.pallas_call(
        _mlstm_kernel,
        grid=(B, nc),
        in_specs=specs(fwd) + specs(bwd),
        out_specs=[hblk(fwd), hblk(bwd)],
        out_shape=[jax.ShapeDtypeStruct((B, S, ML_WIDTH), F32),
                   jax.ShapeDtypeStruct((B, S, ML_WIDTH), F32)],
        scratch_shapes=[pltpu.VMEM((2 * ML_HEADS, ML_HEAD_DIM, 2 * ML_HEAD_DIM), F32),
                        pltpu.VMEM((2 * SUBLANES, LANES), F32)],
        compiler_params=_cparams(("arbitrary", "arbitrary")),
        name="mlstm",
        interpret=interpret,
    )(q3, kt, v3, gates_t, q3, kt, v3, gates_t)


def _outproj_kernel(xp_ref, xs_ref, o1_ref, o4_ref, o16_ref, l1_ref, l4_ref, l16_ref,
                    hf_ref, hb_ref, om_ref, gh_ref, wo_ref, gf_ref, wrt_ref, br_ref,
                    h_ref, hn_ref, idx_ref, gate_ref, cnt_ref,
                    carry, nat4, nat16, lnat4, lnat16, *, n_first):
    tm = TOK_TILE
    step = pl.program_id(0)

    @pl.when(step == 0)
    def _():
        carry[...] = jnp.zeros_like(carry)

    for d, o_ref, l_ref, nat, lnat in ((4, o4_ref, l4_ref, nat4, lnat4), (16, o16_ref, l16_ref, nat16, lnat16)):
        for ph in range(d):
            rows = pl.ds(ph, tm // d, stride=d)
            lnat[rows, :] = l_ref[ph]
            for j in range(ATT_WIDTH // LANES):
                nat.at[j][rows, :] = o_ref[ph, :, j * LANES:(j + 1) * LANES].astype(F32)

    l1, l2, l3 = l1_ref[...], lnat4[...], lnat16[...]
    mx = jnp.maximum(jnp.maximum(l1, l2), l3)
    e1, e2, e3 = jnp.exp(l1 - mx), jnp.exp(l2 - mx), jnp.exp(l3 - mx)
    inv = 1.0 / (e1 + e2 + e3)
    er = lax.broadcasted_iota(jnp.int32, (LANES, ATT_WIDTH), 0)
    ec = lax.broadcasted_iota(jnp.int32, (LANES, ATT_WIDTH), 1)
    expand = (ec // ATT_HEAD_DIM == er).astype(BF16)

    def widen(w):
        hi = w.astype(BF16)
        lo = (w - hi.astype(F32)).astype(BF16)
        return (jnp.dot(hi, expand, preferred_element_type=F32)
                + jnp.dot(lo, expand, preferred_element_type=F32))

    slabs = lambda nat: jnp.concatenate([nat[j] for j in range(ATT_WIDTH // LANES)], axis=-1)
    attn = (widen(e1 * inv) * o1_ref[...].astype(F32)
            + widen(e2 * inv) * slabs(nat4)
            + widen(e3 * inv) * slabs(nat16))

    hsum = hf_ref[...] + hb_ref[...]
    normed = []
    for hh in range(ML_HEADS):
        hv = hsum[:, hh * ML_HEAD_DIM:(hh + 1) * ML_HEAD_DIM]
        normed.append(hv * lax.rsqrt(jnp.mean(hv * hv, axis=-1, keepdims=True) + EPS))
    ml = jax.nn.sigmoid(om_ref[...]) * (jnp.concatenate(normed, axis=-1) * gh_ref[...])

    mix = jnp.concatenate([attn.astype(BF16), ml.astype(BF16)], axis=-1)
    x = jnp.where(step < n_first, xp_ref[...], xs_ref[...])
    h = x + jnp.dot(mix, wo_ref[...], preferred_element_type=F32)
    h_ref[...] = h
    hn = h * lax.rsqrt(jnp.mean(h * h, axis=-1, keepdims=True) + EPS) * gf_ref[...]
    for j in range(D_MODEL // LANES):
        hn_ref[pl.ds(j, tm, stride=SUBLANES), :] = hn[:, j * LANES:(j + 1) * LANES]

    logits = lax.dot_general(wrt_ref[...], hn.astype(BF16), (((1,), (1,)), ((), ())),
                             preferred_element_type=F32) + br_ref[:, 0:1]
    eid = lax.broadcasted_iota(jnp.int32, (N_EXPERTS, tm), 0)
    work = logits
    vals, idxs, hots = [], [], []
    for _ in range(TOP_K):
        top = jnp.max(work, axis=0, keepdims=True)
        idx = jnp.min(jnp.where(work == top, eid, N_EXPERTS), axis=0, keepdims=True)
        hot = eid == idx
        work = jnp.where(hot, -jnp.inf, work)
        vals.append(top)
        idxs.append(idx)
        hots.append(hot)
    exps = [jnp.exp(v - vals[0]) for v in vals]
    inv_den = 1.0 / (exps[0] + exps[1] + exps[2] + exps[3])

    cnt = jnp.where(hots[0] | hots[1] | hots[2] | hots[3], 1.0, 0.0)
    r_idx = lax.broadcasted_iota(jnp.int32, (tm, tm), 0)
    c_idx = lax.broadcasted_iota(jnp.int32, (tm, tm), 1)
    before = (r_idx < c_idx).astype(BF16)
    rank_all = jnp.dot(cnt.astype(BF16), before, preferred_element_type=F32) + carry[:, 0:1]
    carry[...] = carry[...] + jnp.sum(cnt, axis=1, keepdims=True)
    cnt_ref[...] = carry[...]

    ranks = [jnp.sum(jnp.where(hots[kk], rank_all, 0.0), axis=0, keepdims=True).astype(jnp.int32)
             for kk in range(TOP_K)]
    idx_ref[...] = jnp.concatenate(idxs + ranks, axis=0)
    gate_ref[...] = jnp.concatenate([e * inv_den for e in exps] + [jnp.zeros((TOP_K, tm), F32)], axis=0)


def _outproj(xp, xs, S, o1, o4, o16, l1, l4, l16, hf, hb, om, g_head, w_out_b, g_ffn, w_rt, b_r, interpret):
    tm = TOK_TILE
    n_first = xp.shape[0] // tm
    T = xp.shape[0] + xs.shape[0]
    tps = S // tm
    row = lambda w: pl.BlockSpec((tm, w), lambda i: (i, 0))
    col = lambda r: pl.BlockSpec((r, tm), lambda i: (0, i))
    const = lambda shape: pl.BlockSpec(shape, lambda i: (0,) * len(shape))
    phase = lambda d, w: pl.BlockSpec((None, d, tm // d, w), lambda i: (i // tps, 0, i % tps, 0))
    xp_spec, xs_spec = _dual_rows(n_first, D_MODEL)
    return pl.pallas_call(
        functools.partial(_outproj_kernel, n_first=n_first),
        grid=(T // tm,),
        in_specs=[xp_spec, xs_spec, row(512), phase(4, 512), phase(16, 512),
                  row(LANES), phase(4, LANES), phase(16, LANES),
                  row(512), row(512), row(512), const((1, ML_WIDTH)),
                  const((D_MODEL, D_MODEL)), const((1, D_MODEL)),
                  const((N_EXPERTS, D_MODEL)), const((N_EXPERTS, LANES))],
        out_specs=[row(D_MODEL), pl.BlockSpec((tm * ROW_TILE, LANES), lambda i: (i, 0)),
                   col(2 * TOP_K), col(2 * TOP_K), const((N_EXPERTS, LANES))],
        out_shape=[jax.ShapeDtypeStruct((T, D_MODEL), F32),
                   jax.ShapeDtypeStruct((T * ROW_TILE, LANES), F32),
                   jax.ShapeDtypeStruct((2 * TOP_K, T), jnp.int32),
                   jax.ShapeDtypeStruct((2 * TOP_K, T), F32),
                   jax.ShapeDtypeStruct((N_EXPERTS, LANES), F32)],
        scratch_shapes=[pltpu.VMEM((N_EXPERTS, LANES), F32),
                        pltpu.VMEM((ATT_WIDTH // LANES, tm, LANES), F32),
                        pltpu.VMEM((ATT_WIDTH // LANES, tm, LANES), F32),
                        pltpu.VMEM((tm, LANES), F32), pltpu.VMEM((tm, LANES), F32)],
        compiler_params=_cparams(("arbitrary",)),
        name="outproj_router",
        interpret=interpret,
    )(xp, xs, o1, o4, o16, l1, l4, l16, hf, hb, om, g_head.reshape(1, ML_WIDTH), w_out_b,
      g_ffn.reshape(1, D_MODEL), w_rt, b_r)


def _expert_kernel(be_ref, nu_ref, src_cur, src_next, hn_ref, wgu_ref, bgu_ref, wd_ref, bd_ref, y_ref,
                   wgu_b, wd_b, xbuf, gsem):
    j = pl.program_id(0)
    slot = lax.rem(j, 2)
    other = 1 - slot
    rows = MOE_BLOCK * ROW_TILE
    active = j < nu_ref[0]

    def gather_wait(s):
        pltpu.make_async_copy(hn_ref.at[pl.ds(0, rows)], xbuf.at[s], gsem.at[s]).wait()

    def gather_row(tab, r, s, queue=0):
        pltpu.make_async_copy(hn_ref.at[pl.ds(pl.multiple_of(tab[0, 0, r], ROW_TILE), ROW_TILE)],
                              xbuf.at[s, pl.ds(r * ROW_TILE, ROW_TILE)], gsem.at[s]).start(priority=queue)

    @pl.when(j == 0)
    def _():
        def first(r, carry):
            gather_row(src_cur, r, 0)
            return carry

        lax.fori_loop(0, MOE_BLOCK, first, 0, unroll=8)

    @pl.when(jnp.logical_and(active, jnp.logical_or(j == 0, be_ref[j] != be_ref[jnp.maximum(j - 1, 0)])))
    def _():
        wgu_b[...] = wgu_ref[0].astype(BF16)
        wd_b[...] = wd_ref[0].astype(BF16)

    def step(cur, nxt):
        gather_wait(cur)
        for r in range(MOE_BLOCK):
            gather_row(src_next, r, nxt, queue=r % 2)

        xv = xbuf.at[cur]
        x = jnp.concatenate([xv[pl.ds(c, MOE_BLOCK, stride=ROW_TILE), :] for c in range(ROW_TILE)],
                            axis=-1).astype(BF16)
        gu = jnp.dot(x, wgu_b[...], preferred_element_type=F32) + bgu_ref[0]
        g = jnp.minimum(gu[:, :D_FF], SWIGLU_LIMIT)
        u = jnp.clip(gu[:, D_FF:], -SWIGLU_LIMIT, SWIGLU_LIMIT)
        hdn = (u + 1.0) * (g * jax.nn.sigmoid(SWIGLU_ALPHA * g))
        y = jnp.dot(hdn.astype(BF16), wd_b[...], preferred_element_type=F32) + bd_ref[0]
        for c in range(ROW_TILE):
            y_ref[pl.ds(c, MOE_BLOCK, stride=ROW_TILE), :] = y[:, c * LANES:(c + 1) * LANES]

    for parity in range(2):
        @pl.when(jnp.logical_and(active, slot == parity))
        def _():
            step(parity, 1 - parity)

    @pl.when(j == nu_ref[0] - 1)
    def _():
        gather_wait(other)

    @pl.when(jnp.logical_not(active))
    def _():
        y_ref[...] = jnp.zeros_like(y_ref)


def _experts(hn_rt, src_tab, block_e, n_used, w_gate_up, b_gate_up, w_down, b_down, interpret):
    nb = src_tab.shape[0]
    tab = lambda f: pl.BlockSpec((1, 1, MOE_BLOCK), lambda j, be, nu: (f(j, nu), 0, 0), memory_space=pltpu.SMEM)
    grid_spec = pltpu.PrefetchScalarGridSpec(
        num_scalar_prefetch=2,
        grid=(nb,),
        in_specs=[tab(lambda j, nu: j), tab(lambda j, nu: jnp.minimum(j + 1, nu[0] - 1)),
                  pl.BlockSpec(memory_space=pl.ANY),
                  pl.BlockSpec((1, D_MODEL, 2 * D_FF), lambda j, be, nu: (be[j], 0, 0)),
                  pl.BlockSpec((1, 1, 2 * D_FF), lambda j, be, nu: (be[j], 0, 0)),
                  pl.BlockSpec((1, D_FF, D_MODEL), lambda j, be, nu: (be[j], 0, 0)),
                  pl.BlockSpec((1, 1, D_MODEL), lambda j, be, nu: (be[j], 0, 0))],
        out_specs=pl.BlockSpec((MOE_BLOCK * ROW_TILE, LANES), lambda j, be, nu: (j, 0)),
        scratch_shapes=[pltpu.VMEM((D_MODEL, 2 * D_FF), BF16),
                        pltpu.VMEM((D_FF, D_MODEL), BF16),
                        pltpu.VMEM((2, MOE_BLOCK * ROW_TILE, LANES), F32),
                        pltpu.SemaphoreType.DMA((2,))],
    )
    return pl.pallas_call(
        _expert_kernel,
        grid_spec=grid_spec,
        out_shape=jax.ShapeDtypeStruct((nb * MOE_BLOCK * ROW_TILE, LANES), F32),
        compiler_params=_cparams(("arbitrary",)),
        name="moe_experts",
        interpret=interpret,
    )(block_e, n_used, src_tab, src_tab, hn_rt, w_gate_up,
      b_gate_up.reshape(N_EXPERTS, 1, 2 * D_FF), w_down, b_down.reshape(N_EXPERTS, 1, D_MODEL))


def _combine_kernel(dest_ref, dnext_ref, y_ref, h_ref, gate_ref, gfin_ref, outp_ref, outs_ref, ybuf, sem,
                    *, tm, n_first, n_tiles):
    i = pl.program_id(0)
    slot = lax.rem(i, 2)

    def fetch(tab, s):
        def issue(r, carry):
            for kk in range(TOP_K):
                d = pl.multiple_of(tab[0, 0, kk * tm + r] * ROW_TILE, ROW_TILE)
                pltpu.make_async_copy(y_ref.at[pl.ds(d, ROW_TILE)],
                                      ybuf.at[s, kk, pl.ds(pl.multiple_of(r * ROW_TILE, ROW_TILE), ROW_TILE)],
                                      sem.at[s]).start(priority=kk % 2)
            return carry

        lax.fori_loop(0, tm, issue, 0, unroll=4)

    @pl.when(i == 0)
    def _():
        fetch(dest_ref, 0)

    @pl.when(i + 1 < n_tiles)
    def _():
        fetch(dnext_ref, 1 - slot)

    for kk in range(TOP_K):
        pltpu.make_async_copy(y_ref.at[pl.ds(0, tm * ROW_TILE)], ybuf.at[slot, kk], sem.at[slot]).wait()

    gate = gate_ref[...]
    parts = []
    for j in range(ROW_TILE):
        rows = pl.ds(j, tm, stride=ROW_TILE)
        part = gate[:, 0:1] * ybuf.at[slot, 0][rows, :]
        for kk in range(1, TOP_K):
            part = part + gate[:, kk:kk + 1] * ybuf.at[slot, kk][rows, :]
        parts.append(part)
    h = h_ref[...] + jnp.concatenate(parts, axis=-1)
    res = h * lax.rsqrt(jnp.mean(h * h, axis=-1, keepdims=True) + EPS) * gfin_ref[...]

    @pl.when(pl.program_id(0) < n_first)
    def _():
        outp_ref[...] = res

    @pl.when(pl.program_id(0) >= n_first)
    def _():
        outs_ref[...] = res


def _combine(y, dest_tiles, h, gate_tm, g_final, t_first, tm, interpret):
    T = h.shape[0]
    n_first = t_first // tm
    outp_spec, outs_spec = _dual_rows(n_first, D_MODEL, tm)
    n_tiles = T // tm
    return pl.pallas_call(
        functools.partial(_combine_kernel, tm=tm, n_first=n_first, n_tiles=n_tiles),
        grid=(n_tiles,),
        in_specs=[pl.BlockSpec((1, 1, tm * TOP_K), lambda i: (i, 0, 0), memory_space=pltpu.SMEM),
                  pl.BlockSpec((1, 1, tm * TOP_K), lambda i: (jnp.minimum(i + 1, n_tiles - 1), 0, 0),
                               memory_space=pltpu.SMEM),
                  pl.BlockSpec(memory_space=pl.ANY),
                  pl.BlockSpec((tm, D_MODEL), lambda i: (i, 0)),
                  pl.BlockSpec((tm, 2 * TOP_K), lambda i: (i, 0)),
                  pl.BlockSpec((1, D_MODEL), lambda i: (0, 0))],
        out_specs=[outp_spec, outs_spec],
        out_shape=[jax.ShapeDtypeStruct((t_first, D_MODEL), F32),
                   jax.ShapeDtypeStruct((T - t_first, D_MODEL), F32)],
        scratch_shapes=[pltpu.VMEM((2, TOP_K, tm * ROW_TILE, LANES), F32), pltpu.SemaphoreType.DMA((2,))],
        compiler_params=_cparams(("arbitrary",)),
        name="moe_combine",
        interpret=interpret,
    )(dest_tiles, dest_tiles, y, h, gate_tm, g_final.reshape(1, D_MODEL))


def _routing_tables(idx_rows, counts, T, tm):
    tk = T * TOP_K
    top_idx = idx_rows[:TOP_K]
    rank = idx_rows[TOP_K:]
    counts = counts.astype(jnp.int32)
    blocks_e = (counts + MOE_BLOCK - 1) // MOE_BLOCK
    cum_blocks = jnp.cumsum(blocks_e)
    start_row = (cum_blocks - blocks_e) * MOE_BLOCK
    experts = jnp.arange(N_EXPERTS, dtype=jnp.int32)
    slot_of = jnp.sum(jnp.where(top_idx[..., None] == experts, start_row, 0), axis=-1) + rank
    dest_tiles = slot_of.reshape(TOP_K, T // tm, tm).transpose(1, 0, 2).reshape(T // tm, 1, TOP_K * tm)
    n_blocks = -(-tk // MOE_BLOCK) + N_EXPERTS
    n_slots = n_blocks * MOE_BLOCK
    pad_len = jnp.concatenate([blocks_e * MOE_BLOCK - counts, n_slots - cum_blocks[-1:] * MOE_BLOCK])
    pad_base = jnp.concatenate([start_row + counts, cum_blocks[-1:] * MOE_BLOCK])
    pad_end = jnp.cumsum(pad_len)
    i = jnp.arange(n_slots - tk, dtype=jnp.int32)
    seg = i[:, None] >= pad_end[None, :]
    shift = jnp.concatenate([pad_base[:1], pad_base[1:] - pad_end[:-1]])
    incr = jnp.concatenate([shift[1:] - shift[:-1], jnp.zeros((1,), jnp.int32)])
    pad_slot = i + shift[0] + jnp.sum(jnp.where(seg, incr, 0), axis=-1)
    source = jnp.argsort(jnp.concatenate([slot_of.reshape(-1), pad_slot])).astype(jnp.int32)
    src_row = (jnp.where(source < tk, source % T, 0) * ROW_TILE).reshape(n_blocks, 1, MOE_BLOCK)
    blocks = jnp.arange(n_blocks, dtype=jnp.int32)
    block_e = jnp.sum((blocks[:, None] >= cum_blocks[None, :]).astype(jnp.int32), axis=-1)
    last_e = jnp.sum((cum_blocks[-1] - 1 >= cum_blocks).astype(jnp.int32))
    block_e = jnp.minimum(block_e, last_e).astype(jnp.int32)
    return dest_tiles, src_row, block_e, cum_blocks[-1:].astype(jnp.int32)


def _layer(xp, xs, S, g_mix, w_in, conv_w, conv_b, b_gates, g_head, w_out, g_ffn, w_router, b_router,
           w_gate_up, b_gate_up, w_down, b_down, g_final, interpret=False):
    T = xp.shape[0] + xs.shape[0]
    B = T // S

    w_in_p = jnp.pad(w_in, ((0, 0), (0, IN_PAD - w_in.shape[1]))).astype(BF16)
    cw_p = jnp.pad(conv_w, ((0, SUBLANES - CONV_W), (0, 0)))
    cb = conv_b.reshape(1, 2 * ML_WIDTH)
    bg_p = jnp.pad(b_gates, (0, LANES - N_GATES)).reshape(1, LANES)
    (q1, k1, v1, q4, k4, v4, q16, k16, v16, qm, kt, vm, om, gates_t) = _inproj(
        xp, xs, S, g_mix, w_in_p, cw_p, cb, bg_p, interpret)

    as_phase = lambda a: a.reshape(B, 1, S, ATT_WIDTH)
    o1, l1 = _attn_branch(as_phase(q1), as_phase(k1), as_phase(v1), 1, interpret)
    o4, l4 = _attn_branch(q4, k4, v4, 4, interpret)
    o16, l16 = _attn_branch(q16, k16, v16, 16, interpret)

    hf, hb = _mlstm(qm, kt, vm, gates_t, B, S, interpret)

    w_rt = w_router.T.astype(BF16)
    b_r = jnp.broadcast_to(b_router[:, None], (N_EXPERTS, LANES))
    h, hn, idx_rows, gate_rows, counts = _outproj(
        xp, xs, S, o1.reshape(T, ATT_WIDTH), o4, o16, l1.reshape(T, LANES), l4, l16,
        hf.reshape(T, ML_WIDTH), hb.reshape(T, ML_WIDTH), om, g_head, w_out.astype(BF16), g_ffn,
        w_rt, b_r, interpret)

    tm = 512
    dest_tiles, src_row, block_e, n_used = _routing_tables(idx_rows, counts[:, 0], T, tm)
    y = _experts(hn, src_row, block_e, n_used, w_gate_up, b_gate_up, w_down, b_down, interpret)
    return _combine(y, dest_tiles, h, gate_rows.T, g_final, xp.shape[0], tm, interpret)


def kernel(x_prompt, x_sample, g_mix, w_in, conv_w, conv_b, b_gates, g_head, w_out, g_ffn, w_router,
           b_router, w_gate_up, b_gate_up, w_down, b_down, g_final):
    S = x_prompt.shape[1]
    yp, ys = _layer(x_prompt.reshape(-1, D_MODEL), x_sample.reshape(-1, D_MODEL), S,
                    g_mix, w_in, conv_w, conv_b, b_gates, g_head, w_out, g_ffn, w_router, b_router,
                    w_gate_up, b_gate_up, w_down, b_down, g_final)
    return (yp.reshape(x_prompt.shape), ys.reshape(x_sample.shape))
```
